```python
import math
import jax
import jax.numpy as jnp
from jax import lax
import numpy as np

D_MODEL = 4096
BATCH = 4
SEQ = 2048
DEPTH = 4
DEC_BATCH = 8
DEC_SEQ = 8
PAST_LEN = 8192
PAGE_SIZE = 128

HEAD_DIM = 128
W_A = D_MODEL // 2
H_A = W_A // HEAD_DIM
CONV_A = 4
GDN_CHUNK = 64
W_B = D_MODEL // 2
CONV_B = 3
W_C = D_MODEL
H_C = W_C // HEAD_DIM
SWA_PATTERNS = ((128, 1), (512, 4), (2048, 16))
W_MAX = max(w for w, _ in SWA_PATTERNS)
SWA_BLOCK = 128
N_MEM = 256
H_M = 4
W_M = H_M * HEAD_DIM

N_EVEN = (DEPTH + 1) // 2
N_ODD = DEPTH // 2
EPS = 1e-6
EVEN_SPLITS = (3 * W_A, W_A, H_A, H_A, W_B, W_B, W_B, W_B, W_M, W_M)
ODD_SPLITS = (W_C, W_C, W_C, W_C, W_M, W_M)
EVEN_IN = sum(EVEN_SPLITS)
ODD_IN = sum(ODD_SPLITS)
EVEN_OUT = W_A + W_B + W_M
ODD_OUT = W_C + W_M

kernel_name = 'hybrid_gdn_shortconv_dilated_swa_memory_step'


def _split(a, sizes):
    cuts = [int(c) for c in np.cumsum(sizes)[:-1]]
    return jnp.split(a, cuts, axis=-1)


def rmsnorm(x, g):
    xf = x.astype(jnp.float32)
    y = xf * lax.rsqrt(jnp.mean(xf * xf, axis=-1, keepdims=True) + EPS)
    return (y * g.astype(jnp.float32)).astype(x.dtype)


def l2norm(x):
    xf = x.astype(jnp.float32)
    return xf * lax.rsqrt(jnp.sum(xf * xf, axis=-1, keepdims=True) + EPS)


def causal_dwconv(x, buf, w):
    k_w = w.shape[0]
    t = x.shape[1]
    xx = jnp.concatenate([buf.astype(x.dtype), x], axis=1)
    y = xx[:, 0:t] * w[0]
    for j in range(1, k_w):
        y = y + xx[:, j:j + t] * w[j]
    return y, xx[:, t:]


def gated_delta_rule(q, k, v, g, beta, s0):
    f32 = jnp.float32
    bn, t, h, dk = q.shape
    cs = min(GDN_CHUNK, t)
    nc = -(-t // cs)
    pad = nc * cs - t

    def prep(a):
        a = jnp.pad(a.astype(f32), [(0, 0), (0, pad)] + [(0, 0)] * (a.ndim - 2))
        a = a.reshape((bn, nc, cs) + a.shape[2:])
        return jnp.moveaxis(a, 3, 1)

    qc, kc, vc, gc, bc = prep(q), prep(k), prep(v), prep(g), prep(beta)
    gam = jnp.cumsum(gc, axis=-1)
    causal = jnp.tril(jnp.ones((cs, cs), bool))
    strict = jnp.tril(jnp.ones((cs, cs), bool), -1)
    diff = gam[..., :, None] - gam[..., None, :]
    decay_ij = jnp.where(causal, jnp.exp(jnp.where(causal, diff, 0.0)), 0.0)
    kb = kc * bc[..., None]
    a_mat = jnp.where(strict, jnp.einsum('bhnid,bhnjd->bhnij', kb, kc) * decay_ij, 0.0)
    ia = a_mat + jnp.eye(cs, dtype=f32)
    u = lax.linalg.triangular_solve(ia, vc * bc[..., None], left_side=True, lower=True, unit_diagonal=True)
    w = lax.linalg.triangular_solve(ia, kb * jnp.exp(gam)[..., None], left_side=True, lower=True, unit_diagonal=True)
    qk = jnp.einsum('bhnid,bhnjd->bhnij', qc, kc) * decay_ij
    q_dec = qc * jnp.exp(gam)[..., None]
    k_dec = kc * jnp.exp(gam[..., -1:] - gam)[..., None]
    last = jnp.exp(gam[..., -1])

    def step(s, xs):
        qd, kd, u_c, w_c, qk_c, l_c = xs
        v_new = u_c - jnp.einsum('bhid,bhde->bhie', w_c, s)
        o = jnp.einsum('bhid,bhde->bhie', qd, s) + jnp.einsum('bhij,bhje->bhie', qk_c, v_new)
        s = s * l_c[..., None, None] + jnp.einsum('bhid,bhie->bhde', kd, v_new)
        return s, o

    xs = tuple(jnp.moveaxis(a, 2, 0) for a in (q_dec, k_dec, u, w, qk, last))
    s_fin, o = lax.scan(step, s0.astype(f32), xs)
    o = jnp.moveaxis(o, 0, 2).transpose(0, 2, 3, 1, 4).reshape(bn, nc * cs, h, v.shape[-1])[:, :t]
    return o, s_fin


def dilated_prompt(q, k, v, window, dil):
    f32 = jnp.float32
    bn, s_len, h, e = q.shape
    nk = window // dil
    l_sub = s_len // dil
    nb = -(-l_sub // SWA_BLOCK)
    lp = nb * SWA_BLOCK

    def strided(a):
        a = a.reshape(bn, l_sub, dil, h, e).transpose(0, 2, 1, 3, 4)
        return jnp.pad(a, ((0, 0), (0, 0), (0, lp - l_sub), (0, 0), (0, 0)))

    blk = (bn, dil, nb, SWA_BLOCK, h, e)

    def with_prev(a):
        prev = jnp.pad(a, ((0, 0), (0, 0), (SWA_BLOCK, 0), (0, 0), (0, 0)))[:, :, :lp]
        return jnp.concatenate([prev.reshape(blk), a.reshape(blk)], axis=3)

    qb = strided(q).reshape(blk)
    kb = with_prev(strided(k))
    vb = with_prev(strided(v))
    sc = jnp.einsum('brcqhe,brckhe->brchqk', qb, kb, preferred_element_type=f32) * (e ** -0.5)
    qi = jnp.arange(SWA_BLOCK)[:, None]
    ki = jnp.arange(2 * SWA_BLOCK)[None, :]
    band = (ki <= qi + SWA_BLOCK) & (ki >= qi + SWA_BLOCK - nk)
    started = (jnp.arange(nb)[:, None, None] > 0) | (ki >= SWA_BLOCK)[None]
    mask = (band[None] & started)[:, None]
    sc = jnp.where(mask, sc, -jnp.inf)
    m = jnp.max(sc, axis=-1, keepdims=True)
    p = jnp.exp(sc - m)
    den = jnp.sum(p, axis=-1, keepdims=True)
    o = jnp.einsum('brchqk,brckhe->brcqhe', (p / den).astype(v.dtype), vb, preferred_element_type=f32)
    lse = (m + jnp.log(den))[..., 0]
    o = o.reshape(bn, dil, lp, h, e)[:, :, :l_sub].transpose(0, 2, 1, 3, 4).reshape(bn, s_len, h, e)
    lse = lse.transpose(0, 1, 2, 4, 3).reshape(bn, dil, lp, h)[:, :, :l_sub].transpose(0, 2, 1, 3).reshape(bn, s_len, h)
    return o, lse


def dilated_sample(q, kx, vx, wb, window, dil):
    f32 = jnp.float32
    t = q.shape[1]
    e = q.shape[-1]
    nk = window // dil
    idx = wb + jnp.arange(t)[:, None] - dil * jnp.arange(nk + 1)[None, :]
    valid = idx >= 0
    idx = jnp.maximum(idx, 0)
    kg = jnp.take(kx, idx, axis=1)
    vg = jnp.take(vx, idx, axis=1)
    sc = jnp.einsum('bthe,btkhe->bthk', q, kg, preferred_element_type=f32) * (e ** -0.5)
    sc = jnp.where(valid[None, :, None, :], sc, -jnp.inf)
    m = jnp.max(sc, axis=-1, keepdims=True)
    p = jnp.exp(sc - m)
    den = jnp.sum(p, axis=-1, keepdims=True)
    o = jnp.einsum('bthk,btkhe->bthe', (p / den).astype(vx.dtype), vg, preferred_element_type=f32)
    return o, (m + jnp.log(den))[..., 0]


def mix_by_denominator(results):
    o = jnp.stack([r[0] for r in results])
    lse = jnp.stack([r[1] for r in results])
    wgt = jax.nn.softmax(lse, axis=0)
    return jnp.einsum('pbth,pbthe->bthe', wgt, o)


def mem_project(mem, g, w_kv, k_gain):
    bn = mem.shape[0]
    m = rmsnorm(mem, g)
    k, v = jnp.split(jnp.einsum('bnd,de->bne', m, w_kv), 2, axis=-1)
    k = rmsnorm(k.reshape(bn, N_MEM, H_M, HEAD_DIM), k_gain)
    return k, v.reshape(bn, N_MEM, H_M, HEAD_DIM)


def mem_attend(q_m, mem_k, mem_v, q_gain):
    bn, t, _ = q_m.shape
    q = rmsnorm(q_m.reshape(bn, t, H_M, HEAD_DIM), q_gain)
    sc = jnp.einsum('bthe,bnhe->bhtn', q, mem_k.astype(q.dtype), preferred_element_type=jnp.float32) * (HEAD_DIM ** -0.5)
    p = jax.nn.softmax(sc, axis=-1)
    o = jnp.einsum('bhtn,bnhe->bthe', p.astype(q.dtype), mem_v.astype(q.dtype))
    return o.reshape(bn, t, W_M)


def even_layer(x, conv_a_buf, s0, conv_b_buf, mem_k, mem_v,
               g_norm, w_in, conv_a_w, a_log, dt_bias, gdn_gain, conv_b_w, w_out, q_gain_m):
    f32 = jnp.float32
    bn, t, _ = x.shape
    h = rmsnorm(x, g_norm)
    proj = jnp.einsum('btd,de->bte', h, w_in)
    qkv, z_a, beta_raw, a_raw, b_gate, c_gate, x_b, z_b, q_m, z_m = _split(proj, EVEN_SPLITS)
    qkv_c, new_conv_a = causal_dwconv(qkv, conv_a_buf, conv_a_w)
    q, k, v = jnp.split(jax.nn.silu(qkv_c), 3, axis=-1)
    q = l2norm(q.reshape(bn, t, H_A, HEAD_DIM)) * (HEAD_DIM ** -0.5)
    k = l2norm(k.reshape(bn, t, H_A, HEAD_DIM))
    v = v.reshape(bn, t, H_A, HEAD_DIM).astype(f32)
    beta = jax.nn.sigmoid(beta_raw.astype(f32))
    g = -jnp.exp(a_log.astype(f32)) * jax.nn.softplus(a_raw.astype(f32) + dt_bias.astype(f32))
    o_a, new_s = gated_delta_rule(q, k, v, g, beta, s0)
    o_a = rmsnorm(o_a, gdn_gain).reshape(bn, t, W_A).astype(x.dtype) * jax.nn.silu(z_a)
    y_b, new_conv_b = causal_dwconv(c_gate * x_b, conv_b_buf, conv_b_w)
    o_b = b_gate * y_b * jax.nn.silu(z_b)
    o_m = mem_attend(q_m, mem_k, mem_v, q_gain_m) * jax.nn.silu(z_m)
    out = jnp.einsum('bte,ed->btd', jnp.concatenate([o_a, o_b, o_m], axis=-1), w_out)
    return x + out, new_conv_a, new_s.astype(s0.dtype), new_conv_b


def odd_layer(x, cache_k, cache_v, mem_k, mem_v, g_norm, w_in, q_gain_c, k_gain_c, w_out, q_gain_m):
    bn, t, _ = x.shape
    h = rmsnorm(x, g_norm)
    proj = jnp.einsum('btd,de->bte', h, w_in)
    q, k, v, z_c, q_m, z_m = _split(proj, ODD_SPLITS)
    q = rmsnorm(q.reshape(bn, t, H_C, HEAD_DIM), q_gain_c)
    k = rmsnorm(k.reshape(bn, t, H_C, HEAD_DIM), k_gain_c)
    v = v.reshape(bn, t, H_C, HEAD_DIM)
    if cache_k is None:
        res = [dilated_prompt(q, k, v, w, d) for w, d in SWA_PATTERNS]
        n_keep = min(W_MAX, t)
        new_k, new_v = k[:, t - n_keep:], v[:, t - n_keep:]
    else:
        wb = cache_k.shape[1]
        kx = jnp.concatenate([cache_k.astype(k.dtype), k], axis=1)
        vx = jnp.concatenate([cache_v.astype(v.dtype), v], axis=1)
        res = [dilated_sample(q, kx, vx, wb, w, d) for w, d in SWA_PATTERNS]
        new_k, new_v = kx[:, t:], vx[:, t:]
    o_c = mix_by_denominator(res).reshape(bn, t, W_C).astype(x.dtype) * jax.nn.silu(z_c)
    o_m = mem_attend(q_m, mem_k, mem_v, q_gain_m) * jax.nn.silu(z_m)
    out = jnp.einsum('bte,ed->btd', jnp.concatenate([o_c, o_m], axis=-1), w_out)
    return x + out, new_k, new_v


def setup_inputs(seed: int = 0) -> dict:
    key = jax.random.key(seed)
    ks = jax.random.split(key, 32)

    def nrm(i, shape, scale):
        return jax.random.normal(ks[i], shape, jnp.float32) * scale

    wb = min(W_MAX, PAST_LEN)
    dt0 = jnp.exp(jax.random.uniform(ks[13], (N_EVEN, H_A), dtype=jnp.float32, minval=math.log(1e-3), maxval=math.log(1e-1)))
    return {
        'x_prompt': nrm(0, (BATCH, SEQ, D_MODEL), 1.0),
        'x_sample': nrm(1, (DEC_BATCH, DEC_SEQ, D_MODEL), 1.0),
        'state_gdn': nrm(2, (N_EVEN, DEC_BATCH, H_A, HEAD_DIM, HEAD_DIM), 0.1),
        'state_gdn_conv': nrm(3, (N_EVEN, DEC_BATCH, CONV_A - 1, 3 * W_A), 1.0),
        'state_sconv': nrm(4, (N_EVEN, DEC_BATCH, CONV_B - 1, W_B), 1.0),
        'cache_swa_k': nrm(5, (N_ODD, DEC_BATCH, wb, H_C, HEAD_DIM), 1.0),
        'cache_swa_v': nrm(6, (N_ODD, DEC_BATCH, wb, H_C, HEAD_DIM), 1.0),
        'cache_mem_k': nrm(7, (DEPTH, DEC_BATCH, N_MEM, H_M, HEAD_DIM), 1.0),
        'cache_mem_v': nrm(8, (DEPTH, DEC_BATCH, N_MEM, H_M, HEAD_DIM), 1.0),
        'mem_prompt': nrm(9, (BATCH, N_MEM, D_MODEL), 1.0),
        'norm_even': 1.0 + nrm(10, (N_EVEN, D_MODEL), 0.02),
        'w_in_even': nrm(11, (N_EVEN, D_MODEL, EVEN_IN), D_MODEL ** -0.5),
        'conv_a': nrm(12, (N_EVEN, CONV_A, 3 * W_A), CONV_A ** -0.5),
        'a_log': jnp.log(jax.random.uniform(ks[14], (N_EVEN, H_A), dtype=jnp.float32, minval=1.0, maxval=16.0)),
        'dt_bias': dt0 + jnp.log(-jnp.expm1(-dt0)),
        'gdn_norm': 1.0 + nrm(15, (N_EVEN, HEAD_DIM), 0.02),
        'conv_b': nrm(16, (N_EVEN, CONV_B, W_B), CONV_B ** -0.5),
        'w_out_even': nrm(17, (N_EVEN, EVEN_OUT, D_MODEL), EVEN_OUT ** -0.5),
        'norm_odd': 1.0 + nrm(18, (N_ODD, D_MODEL), 0.02),
        'w_in_odd': nrm(19, (N_ODD, D_MODEL, ODD_IN), D_MODEL ** -0.5),
        'q_norm_c': 1.0 + nrm(20, (N_ODD, HEAD_DIM), 0.02),
        'k_norm_c': 1.0 + nrm(21, (N_ODD, HEAD_DIM), 0.02),
        'w_out_odd': nrm(22, (N_ODD, ODD_OUT, D_MODEL), ODD_OUT ** -0.5),
        'mem_norm': 1.0 + nrm(23, (DEPTH, D_MODEL), 0.02),
        'w_mem_kv': nrm(24, (DEPTH, D_MODEL, 2 * W_M), D_MODEL ** -0.5),
        'q_norm_m': 1.0 + nrm(25, (DEPTH, HEAD_DIM), 0.02),
        'k_norm_m': 1.0 + nrm(26, (DEPTH, HEAD_DIM), 0.02),
    }


def reference(x_prompt, x_sample, state_gdn, state_gdn_conv, state_sconv, cache_swa_k, cache_swa_v,
              cache_mem_k, cache_mem_v, mem_prompt,
              norm_even, w_in_even, conv_a, a_log, dt_bias, gdn_norm, conv_b, w_out_even,
              norm_odd, w_in_odd, q_norm_c, k_norm_c, w_out_odd,
              mem_norm, w_mem_kv, q_norm_m, k_norm_m):
    bp = x_prompt.shape[0]
    y_p, y_s = x_prompt, x_sample
    gdn_p, gdn_s, gconv_p, gconv_s, sconv_p, sconv_s = [], [], [], [], [], []
    swk_p, swk_s, swv_p, swv_s, memk_p, memv_p = [], [], [], [], [], []
    for i in range(DEPTH):
        mk_p, mv_p = mem_project(mem_prompt, mem_norm[i], w_mem_kv[i], k_norm_m[i])
        memk_p.append(mk_p)
        memv_p.append(mv_p)
        if i % 2 == 0:
            e = i // 2
            wts = (norm_even[e], w_in_even[e], conv_a[e], a_log[e], dt_bias[e], gdn_norm[e],
                   conv_b[e], w_out_even[e], q_norm_m[i])
            y_p, ca, sa, cb = even_layer(
                y_p,
                jnp.zeros((bp, CONV_A - 1, 3 * W_A), state_gdn_conv.dtype),
                jnp.zeros((bp, H_A, HEAD_DIM, HEAD_DIM), state_gdn.dtype),
                jnp.zeros((bp, CONV_B - 1, W_B), state_sconv.dtype),
                mk_p, mv_p, *wts)
            gconv_p.append(ca)
            gdn_p.append(sa)
            sconv_p.append(cb)
            y_s, ca, sa, cb = even_layer(y_s, state_gdn_conv[e], state_gdn[e], state_sconv[e],
                                         cache_mem_k[i], cache_mem_v[i], *wts)
            gconv_s.append(ca)
            gdn_s.append(sa)
            sconv_s.append(cb)
        else:
            o = i // 2
            wts = (norm_odd[o], w_in_odd[o], q_norm_c[o], k_norm_c[o], w_out_odd[o], q_norm_m[i])
            y_p, nk, nv = odd_layer(y_p, None, None, mk_p, mv_p, *wts)
            swk_p.append(nk)
            swv_p.append(nv)
            y_s, nk, nv = odd_layer(y_s, cache_swa_k[o], cache_swa_v[o], cache_mem_k[i], cache_mem_v[i], *wts)
            swk_s.append(nk)
            swv_s.append(nv)
    gdn_state_p = jnp.stack(gdn_p)
    gdn_state_s = jnp.stack(gdn_s)
    gdn_conv_p = jnp.stack(gconv_p)
    gdn_conv_s = jnp.stack(gconv_s)
    sconv_state_p = jnp.stack(sconv_p)
    sconv_state_s = jnp.stack(sconv_s)
    swa_k_p = jnp.stack(swk_p)
    swa_k_s = jnp.stack(swk_s)
    swa_v_p = jnp.stack(swv_p)
    swa_v_s = jnp.stack(swv_s)
    mem_k_p = jnp.stack(memk_p)
    mem_v_p = jnp.stack(memv_p)
    return (y_p, y_s, gdn_state_p, gdn_state_s, gdn_conv_p, gdn_conv_s, sconv_state_p, sconv_state_s,
            swa_k_p, swa_k_s, swa_v_p, swa_v_s, mem_k_p, mem_v_p)
```

```python
import functools
import math

import jax
import jax.numpy as jnp
from jax import lax
from jax.experimental import pallas as pl
from jax.experimental.pallas import tpu as pltpu

F32 = jnp.float32
BF16 = jnp.bfloat16
EPS = 1e-6
HEAD = 128
SUBLANES = 8
CONV_A = 4
CONV_B = 3
GDN_CHUNK = 64
SWA_PATTERNS = ((128, 1), (512, 4), (2048, 16))
SWA_BLOCK = 128
H_M = 4
VMEM_LIMIT = 48 * 1024 * 1024
HI = lax.Precision.HIGHEST


def _cparams(n_axes):
    return pltpu.CompilerParams(dimension_semantics=("arbitrary",) * n_axes,
                                vmem_limit_bytes=VMEM_LIMIT)


def _silu(x):
    return x * jax.nn.sigmoid(x)


def _dot(a, b, precision=None):
    return jnp.dot(a, b, preferred_element_type=F32, precision=precision)


def _dot_nt(a, b, precision=None):
    return lax.dot_general(a, b, (((1,), (1,)), ((), ())), preferred_element_type=F32,
                           precision=precision)


def _dot_tn(a, b, precision=None):
    return lax.dot_general(a, b, (((0,), (0,)), ((), ())), preferred_element_type=F32,
                           precision=precision)


def _rms_cast_kernel(x_ref, g_ref, o_ref):
    x = x_ref[...]
    ms = jnp.mean(x * x, axis=-1, keepdims=True)
    o_ref[...] = (x * lax.rsqrt(ms + EPS) * g_ref[...]).astype(o_ref.dtype)


def rms_cast(x2d, g, tm):
    m, d = x2d.shape
    return pl.pallas_call(
        _rms_cast_kernel,
        grid=(m // tm,),
        in_specs=[pl.BlockSpec((tm, d), lambda i: (i, 0)),
                  pl.BlockSpec((1, d), lambda i: (0, 0))],
        out_specs=pl.BlockSpec((tm, d), lambda i: (i, 0)),
        out_shape=jax.ShapeDtypeStruct((m, d), BF16),
        compiler_params=_cparams(1),
        name="rms_cast",
    )(x2d, g.reshape(1, d))


def _mm_kernel(*refs, n_in, has_res):
    o_ref = refs[-1]
    acc = _dot(refs[0][...], refs[n_in][...])
    for i in range(1, n_in):
        acc = acc + _dot(refs[i][...], refs[n_in + i][...])
    if has_res:
        acc = acc + refs[2 * n_in][...]
    o_ref[...] = acc.astype(o_ref.dtype)


def matmul(a_list, w, n_off, n_size, tm, tn, res=None):
    m = a_list[0].shape[0]
    n_in = len(a_list)
    in_specs, w_specs = [], []
    row = 0
    for a in a_list:
        k = a.shape[1]
        assert row % k == 0 and n_off % tn == 0 and n_size % tn == 0 and m % tm == 0
        in_specs.append(pl.BlockSpec((tm, k), lambda n, i: (i, 0)))
        w_specs.append(pl.BlockSpec((k, tn), functools.partial(
            lambda n, i, rb, nb: (rb, n + nb), rb=row // k, nb=n_off // tn)))
        row += k
    args = list(a_list) + [w] * n_in
    specs = in_specs + w_specs
    if res is not None:
        specs.append(pl.BlockSpec((tm, tn), lambda n, i: (i, n)))
        args.append(res)
    return pl.pallas_call(
        functools.partial(_mm_kernel, n_in=n_in, has_res=res is not None),
        grid=(n_size // tn, m // tm),
        in_specs=specs,
        out_specs=pl.BlockSpec((tm, tn), lambda n, i: (i, n)),
        out_shape=jax.ShapeDtypeStruct((m, n_size), F32),
        compiler_params=_cparams(2),
        name="matmul",
    )(*args)


def _headnorm_kernel(x_ref, g_ref, o_ref, *, heads):
    g = g_ref[...]
    for h in range(heads):
        cols = slice(h * HEAD, (h + 1) * HEAD)
        x = x_ref[:, cols]
        ms = jnp.mean(x * x, axis=-1, keepdims=True)
        o_ref[:, cols] = x * lax.rsqrt(ms + EPS) * g


def headnorm(x2d, col_off, width, gain, tm, cw):
    m = x2d.shape[0]
    assert col_off % cw == 0 and width % cw == 0 and m % tm == 0
    return pl.pallas_call(
        functools.partial(_headnorm_kernel, heads=cw // HEAD),
        grid=(m // tm, width // cw),
        in_specs=[pl.BlockSpec((tm, cw), lambda i, c: (i, c + col_off // cw)),
                  pl.BlockSpec((1, HEAD), lambda i, c: (0, 0))],
        out_specs=pl.BlockSpec((tm, cw), lambda i, c: (i, c)),
        out_shape=jax.ShapeDtypeStruct((m, width), F32),
        compiler_params=_cparams(2),
        name="headnorm",
    )(x2d, gain.reshape(1, HEAD))


def _gate_kernel(x_ref, alog_ref, dt_ref, o_ref, *, n_heads):
    x = x_ref[...]
    lane = lax.broadcasted_iota(jnp.int32, x.shape, 1)
    beta = jax.nn.sigmoid(x)
    g = -jnp.exp(alog_ref[...]) * jax.nn.softplus(x + dt_ref[...])
    o_ref[...] = jnp.where(lane < n_heads, beta, g)


def gdn_gates(small2d, a_log, dt_bias, tm):
    m = small2d.shape[0]
    n_heads = a_log.shape[0]
    pad = HEAD - 2 * n_heads
    alog_row = jnp.concatenate([jnp.zeros((n_heads,), F32), a_log, jnp.zeros((pad,), F32)]).reshape(1, HEAD)
    dt_row = jnp.concatenate([jnp.zeros((n_heads,), F32), dt_bias, jnp.zeros((pad,), F32)]).reshape(1, HEAD)
    return pl.pallas_call(
        functools.partial(_gate_kernel, n_heads=n_heads),
        grid=(m // tm,),
        in_specs=[pl.BlockSpec((tm, HEAD), lambda i: (i, 0)),
                  pl.BlockSpec((1, HEAD), lambda i: (0, 0)),
                  pl.BlockSpec((1, HEAD), lambda i: (0, 0))],
        out_specs=pl.BlockSpec((tm, HEAD), lambda i: (i, 0)),
        out_shape=jax.ShapeDtypeStruct((m, HEAD), F32),
        compiler_params=_cparams(1),
        name="gdn_gates",
    )(small2d, alog_row, dt_row)


def _conv_qkv_kernel(x_ref, halo_ref, st_ref, w_ref, o_ref, buf_ref, *, tt, n_q, n_qk, heads):
    c = pl.program_id(1)
    t = pl.program_id(2)
    buf_ref[0:SUBLANES, :] = jnp.where(t == 0, st_ref[0], halo_ref[0])
    buf_ref[SUBLANES:SUBLANES + tt, :] = x_ref[0]
    w = w_ref[...]
    first = SUBLANES - (CONV_A - 1)
    y = buf_ref[first:first + tt, :] * w[0:1, :]
    for j in range(1, CONV_A):
        y = y + buf_ref[first + j:first + j + tt, :] * w[j:j + 1, :]
    y = _silu(y)

    @pl.when(c < n_qk)
    def _():
        scale = jnp.where(c < n_q, HEAD ** -0.5, 1.0).astype(F32)
        for h in range(heads):
            cols = slice(h * HEAD, (h + 1) * HEAD)
            yh = y[:, cols]
            ss = jnp.sum(yh * yh, axis=-1, keepdims=True)
            o_ref[0, :, cols] = yh * lax.rsqrt(ss + EPS) * scale

    @pl.when(c >= n_qk)
    def _():
        o_ref[0] = y


def conv_qkv(proj3d, state_pad, conv_w, width, tt, cw):
    bn, t, _ = proj3d.shape
    third = width // 3
    assert third % cw == 0 and t % tt == 0 and tt % SUBLANES == 0
    rb = tt // SUBLANES
    return pl.pallas_call(
        functools.partial(_conv_qkv_kernel, tt=tt, n_q=third // cw, n_qk=2 * third // cw, heads=cw // HEAD),
        grid=(bn, width // cw, t // tt),
        in_specs=[pl.BlockSpec((1, tt, cw), lambda b, c, i: (b, i, c)),
                  pl.BlockSpec((1, SUBLANES, cw), lambda b, c, i: (b, jnp.maximum(i * rb - 1, 0), c)),
                  pl.BlockSpec((1, SUBLANES, cw), lambda b, c, i: (b, 0, c)),
                  pl.BlockSpec((CONV_A, cw), lambda b, c, i: (0, c))],
        out_specs=pl.BlockSpec((1, tt, cw), lambda b, c, i: (b, i, c)),
        out_shape=jax.ShapeDtypeStruct((bn, t, width), F32),
        scratch_shapes=[pltpu.VMEM((tt + SUBLANES, cw), F32)],
        compiler_params=_cparams(3),
        name="conv_qkv",
    )(proj3d, proj3d, state_pad, conv_w)


def _sconv_kernel(bg_ref, cg_ref, xb_ref, zb_ref, hc_ref, hx_ref, st_ref, w_ref, o_ref, nb_ref, buf_ref, *, tt):
    t = pl.program_id(2)
    buf_ref[0:SUBLANES, :] = jnp.where(t == 0, st_ref[0], hc_ref[0] * hx_ref[0])
    buf_ref[SUBLANES:SUBLANES + tt, :] = cg_ref[0] * xb_ref[0]
    w = w_ref[...]
    first = SUBLANES - (CONV_B - 1)
    y = buf_ref[first:first + tt, :] * w[0:1, :]
    for j in range(1, CONV_B):
        y = y + buf_ref[first + j:first + j + tt, :] * w[j:j + 1, :]
    o_ref[0] = (bg_ref[0] * y * _silu(zb_ref[0])).astype(o_ref.dtype)
    nb_ref[0] = buf_ref[tt:tt + SUBLANES, :]


def short_conv(proj3d, off_b, off_c, off_x, off_z, width, state_pad, conv_w, tt, cw):
    bn, t, _ = proj3d.shape
    assert t % tt == 0 and width % cw == 0
    rb = tt // SUBLANES

    def col(off):
        assert off % cw == 0
        return functools.partial(lambda b, c, i, o: (b, i, c + o), o=off // cw)

    def halo(off):
        return functools.partial(lambda b, c, i, o: (b, jnp.maximum(i * rb - 1, 0), c + o), o=off // cw)

    blk = (1, tt, cw)
    hblk = (1, SUBLANES, cw)
    return pl.pallas_call(
        functools.partial(_sconv_kernel, tt=tt),
        grid=(bn, width // cw, t // tt),
        in_specs=[pl.BlockSpec(blk, col(off_b)), pl.BlockSpec(blk, col(off_c)),
                  pl.BlockSpec(blk, col(off_x)), pl.BlockSpec(blk, col(off_z)),
                  pl.BlockSpec(hblk, halo(off_c)), pl.BlockSpec(hblk, halo(off_x)),
                  pl.BlockSpec(hblk, lambda b, c, i: (b, 0, c)),
                  pl.BlockSpec((CONV_B, cw), lambda b, c, i: (0, c))],
        out_specs=[pl.BlockSpec(blk, lambda b, c, i: (b, i, c)),
                   pl.BlockSpec(hblk, lambda b, c, i: (b, 0, c))],
        out_shape=[jax.ShapeDtypeStruct((bn, t, width), BF16),
                   jax.ShapeDtypeStruct((bn, SUBLANES, width), F32)],
        scratch_shapes=[pltpu.VMEM((tt + SUBLANES, cw), F32)],
        compiler_params=_cparams(3),
        name="short_conv",
    )(proj3d, proj3d, proj3d, proj3d, proj3d, proj3d, state_pad, conv_w)


def _gdn_kernel(q_ref, k_ref, v_ref, z_ref, g_ref, b_ref, s0_ref, gain_ref, o_ref, s_ref, *, hb, cs, nc):
    s_ref[...] = s0_ref[...]
    ii = lax.broadcasted_iota(jnp.int32, (cs, cs), 0)
    jj = lax.broadcasted_iota(jnp.int32, (cs, cs), 1)
    causal = ii >= jj
    strict = ii > jj
    eye = ii == jj
    eye_f = eye.astype(F32)
    gain = gain_ref[...]
    n_double = int(math.log2(cs)) - 1

    def chunk(n, carry):
        r0 = pl.multiple_of(n * cs, cs)
        rows = pl.ds(r0, cs)
        for h in range(hb):
            cols = slice(h * HEAD, (h + 1) * HEAD)
            q = q_ref[0, rows, cols]
            k = k_ref[0, rows, cols]
            v = v_ref[0, rows, cols]
            g_lane = jnp.broadcast_to(g_ref[0, h, pl.ds(n, 1), :], (cs, cs))
            b_lane = jnp.broadcast_to(b_ref[0, h, pl.ds(n, 1), :], (cs, cs))
            gam_col = jnp.sum(jnp.where(causal, g_lane, 0.0), axis=1, keepdims=True)
            g_col = jnp.sum(jnp.where(eye, g_lane, 0.0), axis=1, keepdims=True)
            beta_col = jnp.sum(jnp.where(eye, b_lane, 0.0), axis=1, keepdims=True)
            gam_row = jnp.sum(jnp.where(ii <= jj, jnp.broadcast_to(g_col, (cs, cs)), 0.0),
                              axis=0, keepdims=True)
            gam_last = jnp.sum(g_lane[0:1, :], axis=1, keepdims=True)
            decay = jnp.where(causal, jnp.exp(jnp.where(causal, gam_col - gam_row, 0.0)), 0.0)
            kb = k * beta_col
            a_mat = jnp.where(strict, _dot_nt(kb, k, HI) * decay, 0.0)
            pw = -a_mat
            inv = eye_f + pw
            for _ in range(n_double):
                pw = _dot(pw, pw, HI)
                inv = inv + _dot(inv, pw, HI)
            e_gam = jnp.exp(gam_col)
            uw = _dot(inv, jnp.concatenate([v * beta_col, kb * e_gam], axis=1), HI)
            u = uw[:, :HEAD]
            w = uw[:, HEAD:]
            qk = jnp.where(causal, _dot_nt(q, k, HI) * decay, 0.0)
            q_dec = q * e_gam
            k_dec = k * jnp.exp(gam_last - gam_col)
            s = s_ref[0, h]
            v_new = u - _dot(w, s, HI)
            o = _dot(q_dec, s, HI) + _dot(qk, v_new, HI)
            s_ref[0, h] = s * jnp.exp(gam_last) + _dot_tn(k_dec, v_new, HI)
            ms = jnp.mean(o * o, axis=-1, keepdims=True)
            on = o * lax.rsqrt(ms + EPS) * gain
            o_ref[0, rows, cols] = (on * _silu(z_ref[0, rows, cols])).astype(o_ref.dtype)
        return carry

    lax.fori_loop(0, nc, chunk, 0)


def gdn(qkv_act, proj3d, z_off, g4, beta4, s0, gain, hb):
    bn, t, w3 = qkv_act.shape
    n_heads = w3 // (3 * HEAD)
    nc, cs = g4.shape[2], g4.shape[3]
    cw = hb * HEAD
    ng = n_heads // hb
    assert z_off % cw == 0 and n_heads % hb == 0
    blk = (1, t, cw)
    return pl.pallas_call(
        functools.partial(_gdn_kernel, hb=hb, cs=cs, nc=nc),
        grid=(bn, ng),
        in_specs=[pl.BlockSpec(blk, lambda b, h: (b, 0, h)),
                  pl.BlockSpec(blk, lambda b, h: (b, 0, ng + h)),
                  pl.BlockSpec(blk, lambda b, h: (b, 0, 2 * ng + h)),
                  pl.BlockSpec(blk, lambda b, h: (b, 0, z_off // cw + h)),
                  pl.BlockSpec((1, hb, nc, cs), lambda b, h: (b, h, 0, 0)),
                  pl.BlockSpec((1, hb, nc, cs), lambda b, h: (b, h, 0, 0)),
                  pl.BlockSpec((1, hb, HEAD, HEAD), lambda b, h: (b, h, 0, 0)),
                  pl.BlockSpec((1, HEAD), lambda b, h: (0, 0))],
        out_specs=[pl.BlockSpec(blk, lambda b, h: (b, 0, h)),
                   pl.BlockSpec((1, hb, HEAD, HEAD), lambda b, h: (b, h, 0, 0))],
        out_shape=[jax.ShapeDtypeStruct((bn, t, n_heads * HEAD), BF16),
                   jax.ShapeDtypeStruct((bn, n_heads, HEAD, HEAD), F32)],
        compiler_params=_cparams(2),
        name="gdn",
    )(qkv_act, qkv_act, qkv_act, proj3d, g4, beta4, s0, gain.reshape(1, HEAD))


def _mem_attn_kernel(q_ref, z_ref, mk_ref, mv_ref, gain_ref, o_ref):
    gain = gain_ref[...]
    for h in range(H_M):
        cols = slice(h * HEAD, (h + 1) * HEAD)
        q = q_ref[0, :, cols]
        ms = jnp.mean(q * q, axis=-1, keepdims=True)
        qn = (q * lax.rsqrt(ms + EPS) * gain).astype(BF16)
        sc = _dot_nt(qn, mk_ref[0, :, cols].astype(BF16)) * (HEAD ** -0.5)
        m = jnp.max(sc, axis=-1, keepdims=True)
        p = jnp.exp(sc - m)
        p = p / jnp.sum(p, axis=-1, keepdims=True)
        o = _dot(p.astype(BF16), mv_ref[0, :, cols].astype(BF16))
        o_ref[0, :, cols] = (o * _silu(z_ref[0, :, cols])).astype(o_ref.dtype)


def mem_attn(proj3d, q_off, z_off, mem_k, mem_v, q_gain, tt):
    bn, t, _ = proj3d.shape
    n_mem = mem_k.shape[1]
    wm = H_M * HEAD
    assert q_off % wm == 0 and z_off % wm == 0 and t % tt == 0
    return pl.pallas_call(
        _mem_attn_kernel,
        grid=(bn, t // tt),
        in_specs=[pl.BlockSpec((1, tt, wm), lambda b, i: (b, i, q_off // wm)),
                  pl.BlockSpec((1, tt, wm), lambda b, i: (b, i, z_off // wm)),
                  pl.BlockSpec((1, n_mem, wm), lambda b, i: (b, 0, 0)),
                  pl.BlockSpec((1, n_mem, wm), lambda b, i: (b, 0, 0)),
                  pl.BlockSpec((1, HEAD), lambda b, i: (0, 0))],
        out_specs=pl.BlockSpec((1, tt, wm), lambda b, i: (b, i, 0)),
        out_shape=jax.ShapeDtypeStruct((bn, t, wm), BF16),
        compiler_params=_cparams(2),
        name="mem_attn",
    )(proj3d, proj3d, mem_k.reshape(bn, n_mem, wm), mem_v.reshape(bn, n_mem, wm), q_gain.reshape(1, HEAD))


def _dil_prompt_kernel(q_ref, k_ref, v_ref, z_ref, o_ref, acc_ref, lse_ref, *, hg, t):
    qi = lax.broadcasted_iota(jnp.int32, (SWA_BLOCK, SWA_BLOCK), 0)
    ki = lax.broadcasted_iota(jnp.int32, (SWA_BLOCK, SWA_BLOCK), 1)
    n_pat = len(SWA_PATTERNS)
    for h in range(hg):
        cols = slice(h * HEAD, (h + 1) * HEAD)
        for p, (window, dil) in enumerate(SWA_PATTERNS):
            assert window // dil == SWA_BLOCK
            l_sub = t // dil
            nb = l_sub // SWA_BLOCK
            assert nb * SWA_BLOCK * dil == t

            for r in range(dil):
                for c in range(nb):
                    cur = pl.ds(r + c * (SWA_BLOCK * dil), SWA_BLOCK, stride=dil)
                    q = q_ref[0, cur, cols].astype(BF16)
                    s_cur = jnp.where(ki <= qi, _dot_nt(q, k_ref[0, cur, cols].astype(BF16)) * (HEAD ** -0.5), -jnp.inf)
                    m = jnp.max(s_cur, axis=-1, keepdims=True)
                    if c > 0:
                        prev = pl.ds(r + (c - 1) * (SWA_BLOCK * dil), SWA_BLOCK, stride=dil)
                        s_prev = jnp.where(ki >= qi, _dot_nt(q, k_ref[0, prev, cols].astype(BF16)) * (HEAD ** -0.5), -jnp.inf)
                        m = jnp.maximum(m, jnp.max(s_prev, axis=-1, keepdims=True))
                    e_cur = jnp.exp(s_cur - m)
                    den = jnp.sum(e_cur, axis=-1, keepdims=True)
                    if c > 0:
                        e_prev = jnp.exp(s_prev - m)
                        den = den + jnp.sum(e_prev, axis=-1, keepdims=True)
                    o = _dot((e_cur / den).astype(BF16), v_ref[0, cur, cols].astype(BF16))
                    if c > 0:
                        o = o + _dot((e_prev / den).astype(BF16), v_ref[0, prev, cols].astype(BF16))
                    acc_ref[p, cur, :] = o
                    lse_ref[p, cur, :] = jnp.broadcast_to(m + jnp.log(den), (SWA_BLOCK, HEAD))
        lse = [lse_ref[p] for p in range(n_pat)]
        top = functools.reduce(jnp.maximum, lse)
        wgt = [jnp.exp(x - top) for x in lse]
        tot = functools.reduce(lambda a, b: a + b, wgt)
        mix = functools.reduce(lambda a, b: a + b, [(wgt[p] / tot) * acc_ref[p] for p in range(n_pat)])
        o_ref[0, :, cols] = (mix * _silu(z_ref[0, :, cols])).astype(o_ref.dtype)


def dil_prompt(qn, kn, proj3d, v_off, z_off, hg):
    bn, t, wc = qn.shape
    cw = hg * HEAD
    assert v_off % cw == 0 and z_off % cw == 0 and wc % cw == 0
    blk = (1, t, cw)
    return pl.pallas_call(
        functools.partial(_dil_prompt_kernel, hg=hg, t=t),
        grid=(bn, wc // cw),
        in_specs=[pl.BlockSpec(blk, lambda b, h: (b, 0, h)),
                  pl.BlockSpec(blk, lambda b, h: (b, 0, h)),
                  pl.BlockSpec(blk, lambda b, h: (b, 0, v_off // cw + h)),
                  pl.BlockSpec(blk, lambda b, h: (b, 0, z_off // cw + h))],
        out_specs=pl.BlockSpec(blk, lambda b, h: (b, 0, h)),
        out_shape=jax.ShapeDtypeStruct((bn, t, wc), BF16),
        scratch_shapes=[pltpu.VMEM((len(SWA_PATTERNS), t, HEAD), F32),
                        pltpu.VMEM((len(SWA_PATTERNS), t, HEAD), F32)],
        compiler_params=_cparams(2),
        name="dil_prompt",
    )(qn, kn, proj3d, proj3d)


def _dil_sample_kernel(q_ref, kn_ref, vn_ref, z_ref, kc_ref, vc_ref, o_ref, *, hg, t, wb):
    row_c = lax.broadcasted_iota(jnp.int32, (t, wb), 0)
    col_c = lax.broadcasted_iota(jnp.int32, (t, wb), 1)
    delta_c = wb + row_c - col_c
    row_n = lax.broadcasted_iota(jnp.int32, (t, t), 0)
    col_n = lax.broadcasted_iota(jnp.int32, (t, t), 1)
    delta_n = row_n - col_n
    for h in range(hg):
        cols = slice(h * HEAD, (h + 1) * HEAD)
        q = q_ref[0, :, cols].astype(BF16)
        vc = vc_ref[0, :, cols].astype(BF16)
        vn = vn_ref[0, :, cols].astype(BF16)
        s_c = _dot_nt(q, kc_ref[0, :, cols].astype(BF16)) * (HEAD ** -0.5)
        s_n = _dot_nt(q, kn_ref[0, :, cols].astype(BF16)) * (HEAD ** -0.5)
        outs, lses = [], []
        for window, dil in SWA_PATTERNS:
            ok_c = ((delta_c & (dil - 1)) == 0) & (delta_c <= window)
            ok_n = (delta_n >= 0) & ((delta_n & (dil - 1)) == 0) & (delta_n <= window)
            m_c = jnp.where(ok_c, s_c, -jnp.inf)
            m_n = jnp.where(ok_n, s_n, -jnp.inf)
            m = jnp.maximum(jnp.max(m_c, axis=-1, keepdims=True), jnp.max(m_n, axis=-1, keepdims=True))
            e_c = jnp.exp(m_c - m)
            e_n = jnp.exp(m_n - m)
            den = jnp.sum(e_c, axis=-1, keepdims=True) + jnp.sum(e_n, axis=-1, keepdims=True)
            outs.append(_dot((e_c / den).astype(BF16), vc) + _dot((e_n / den).astype(BF16), vn))
            lses.append(m + jnp.log(den))
        top = functools.reduce(jnp.maximum, lses)
        wgt = [jnp.exp(x - top) for x in lses]
        tot = functools.reduce(lambda a, b: a + b, wgt)
        mix = functools.reduce(lambda a, b: a + b, [(w / tot) * o for w, o in zip(wgt, outs)])
        o_ref[0, :, cols] = (mix * _silu(z_ref[0, :, cols])).astype(o_ref.dtype)


def dil_sample(qn, kn, proj3d, v_off, z_off, cache_k, cache_v, hg):
    bn, t, wc = qn.shape
    wb = cache_k.shape[1]
    cw = hg * HEAD
    assert v_off % cw == 0 and z_off % cw == 0 and wc % cw == 0
    for window, dil in SWA_PATTERNS:
        assert dil & (dil - 1) == 0
    blk = (1, t, cw)
    cblk = (1, wb, cw)
    return pl.pallas_call(
        functools.partial(_dil_sample_kernel, hg=hg, t=t, wb=wb),
        grid=(bn, wc // cw),
        in_specs=[pl.BlockSpec(blk, lambda b, h: (b, 0, h)),
                  pl.BlockSpec(blk, lambda b, h: (b, 0, h)),
                  pl.BlockSpec(blk, lambda b, h: (b, 0, v_off // cw + h)),
                  pl.BlockSpec(blk, lambda b, h: (b, 0, z_off // cw + h)),
                  pl.BlockSpec(cblk, lambda b, h: (b, 0, h)),
                  pl.BlockSpec(cblk, lambda b, h: (b, 0, h))],
        out_specs=pl.BlockSpec(blk, lambda b, h: (b, 0, h)),
        out_shape=jax.ShapeDtypeStruct((bn, t, wc), BF16),
        compiler_params=_cparams(2),
        name="dil_sample",
    )(qn, kn, proj3d, proj3d, cache_k, cache_v)


def _row_tile(m, pref):
    return pref if m % pref == 0 else m


def _in_proj(x2d, g_norm, w_bf16, n_main):
    m = x2d.shape[0]
    h = rms_cast(x2d, g_norm, _row_tile(m, 256))
    return h, matmul([h], w_bf16, 0, n_main, _row_tile(m, 512), 1024)


def _pad_state(buf):
    return jnp.pad(buf, ((0, 0), (SUBLANES - buf.shape[1], 0), (0, 0)))


def even_layer(x, conv_a_buf, s0, conv_b_buf, mem_k, mem_v, g_norm, w_main, w_small, conv_a_w, a_log,
               dt_bias, gdn_gain, conv_b_w, w_out, q_gain_m):
    bn, t, d = x.shape
    m = bn * t
    n_heads = a_log.shape[0]
    w_a = n_heads * HEAD
    w_b = conv_b_w.shape[1]
    w_m = H_M * HEAD
    off_za = 3 * w_a
    off_b = off_za + w_a
    off_c, off_x, off_zb = off_b + w_b, off_b + 2 * w_b, off_b + 3 * w_b
    off_qm = off_b + 4 * w_b
    off_zm = off_qm + w_m
    n_main = off_zm + w_m
    x2d = x.reshape(m, d)
    h, proj = _in_proj(x2d, g_norm, w_main, n_main)
    small = matmul([h], w_small, 0, HEAD, _row_tile(m, 512), HEAD)
    proj3d = proj.reshape(bn, t, n_main)
    tt = _row_tile(t, 512)
    qkv_act = conv_qkv(proj3d, _pad_state(conv_a_buf), conv_a_w, 3 * w_a, tt, 512)
    gates = gdn_gates(small, a_log, dt_bias, _row_tile(m, 512)).reshape(bn, t, HEAD)
    cs = min(GDN_CHUNK, t)
    nc = t // cs
    beta4 = gates[:, :, :n_heads].transpose(0, 2, 1).reshape(bn, n_heads, nc, cs)
    g4 = gates[:, :, n_heads:2 * n_heads].transpose(0, 2, 1).reshape(bn, n_heads, nc, cs)
    o_a, new_s = gdn(qkv_act, proj3d, off_za, g4, beta4, s0, gdn_gain, 2)
    new_conv_a = proj3d[:, t - (CONV_A - 1):, :3 * w_a]
    o_b, nb8 = short_conv(proj3d, off_b, off_c, off_x, off_zb, w_b, _pad_state(conv_b_buf), conv_b_w, tt, 512)
    new_conv_b = nb8[:, SUBLANES - (CONV_B - 1):, :]
    o_m = mem_attn(proj3d, off_qm, off_zm, mem_k, mem_v, q_gain_m, tt)
    y = matmul([o_a.reshape(m, w_a), o_b.reshape(m, w_b), o_m.reshape(m, w_m)], w_out, 0, d,
               _row_tile(m, 512), 1024, res=x2d)
    return y.reshape(bn, t, d), new_conv_a, new_s, new_conv_b


def odd_layer(x, cache_k, cache_v, mem_k, mem_v, g_norm, w_main, q_gain_c, k_gain_c, w_out, q_gain_m):
    bn, t, d = x.shape
    m = bn * t
    w_c = d
    n_heads = w_c // HEAD
    w_m = H_M * HEAD
    off_v, off_z = 2 * w_c, 3 * w_c
    off_qm = 4 * w_c
    off_zm = off_qm + w_m
    n_main = off_zm + w_m
    x2d = x.reshape(m, d)
    _, proj = _in_proj(x2d, g_norm, w_main, n_main)
    proj3d = proj.reshape(bn, t, n_main)
    tm = _row_tile(m, 512)
    qn = headnorm(proj, 0, w_c, q_gain_c, tm, 512).reshape(bn, t, w_c)
    kn = headnorm(proj, w_c, w_c, k_gain_c, tm, 512).reshape(bn, t, w_c)
    v = proj3d[:, :, off_v:off_v + w_c]
    if cache_k is None:
        o_c = dil_prompt(qn, kn, proj3d, off_v, off_z, 1)
        n_keep = min(max(w for w, _ in SWA_PATTERNS), t)
        new_k = kn[:, t - n_keep:].reshape(bn, n_keep, n_heads, HEAD)
        new_v = v[:, t - n_keep:].reshape(bn, n_keep, n_heads, HEAD)
    else:
        wb = cache_k.shape[1]
        ck = cache_k.reshape(bn, wb, w_c)
        cv = cache_v.reshape(bn, wb, w_c)
        o_c = dil_sample(qn, kn, proj3d, off_v, off_z, ck, cv, 4)
        new_k = jnp.concatenate([ck[:, t:], kn], axis=1).reshape(bn, wb, n_heads, HEAD)
        new_v = jnp.concatenate([cv[:, t:], v], axis=1).reshape(bn, wb, n_heads, HEAD)
    o_m = mem_attn(proj3d, off_qm, off_zm, mem_k, mem_v, q_gain_m, _row_tile(t, 512))
    y = matmul([o_c.reshape(m, w_c), o_m.reshape(m, w_m)], w_out, 0, d, tm, 1024, res=x2d)
    return y.reshape(bn, t, d), new_k, new_v


def mem_project(mem, g, w_kv_bf16, k_gain):
    bn, n_mem, d = mem.shape
    m = bn * n_mem
    w_m = H_M * HEAD
    hm = rms_cast(mem.reshape(m, d), g, _row_tile(m, 256))
    kv = matmul([hm], w_kv_bf16, 0, 2 * w_m, _row_tile(m, 512), w_m)
    k = headnorm(kv, 0, w_m, k_gain, _row_tile(m, 512), w_m)
    return k.reshape(bn, n_mem, H_M, HEAD), kv[:, w_m:].reshape(bn, n_mem, H_M, HEAD)


def _prep_even_weights(w_in, n_heads):
    w_a = n_heads * HEAD
    cut = 4 * w_a
    w_main = jnp.concatenate([w_in[:, :cut], w_in[:, cut + 2 * n_heads:]], axis=1).astype(BF16)
    w_small = jnp.pad(w_in[:, cut:cut + 2 * n_heads], ((0, 0), (0, HEAD - 2 * n_heads))).astype(BF16)
    return w_main, w_small


def kernel(x_prompt, x_sample, state_gdn, state_gdn_conv, state_sconv, cache_swa_k, cache_swa_v, cache_mem_k, cache_mem_v, mem_prompt, norm_even, w_in_even, conv_a, a_log, dt_bias, gdn_norm, conv_b, w_out_even, norm_odd, w_in_odd, q_norm_c, k_norm_c, w_out_odd, mem_norm, w_mem_kv, q_norm_m, k_norm_m):
    bp = x_prompt.shape[0]
    depth = mem_norm.shape[0]
    n_heads_a = a_log.shape[1]
    y_p, y_s = x_prompt, x_sample
    gdn_p, gdn_s, gconv_p, gconv_s, sconv_p, sconv_s = [], [], [], [], [], []
    swk_p, swk_s, swv_p, swv_s, memk_p, memv_p = [], [], [], [], [], []
    for i in range(depth):
        mk_p, mv_p = mem_project(mem_prompt, mem_norm[i], w_mem_kv[i].astype(BF16), k_norm_m[i])
        memk_p.append(mk_p)
        memv_p.append(mv_p)
        if i % 2 == 0:
            e = i // 2
            w_main, w_small = _prep_even_weights(w_in_even[e], n_heads_a)
            wts = (norm_even[e], w_main, w_small, conv_a[e], a_log[e], dt_bias[e], gdn_norm[e], conv_b[e],
                   w_out_even[e].astype(BF16), q_norm_m[i])
            y_p, ca, sa, cb = even_layer(
                y_p,
                jnp.zeros((bp,) + state_gdn_conv.shape[2:], state_gdn_conv.dtype),
                jnp.zeros((bp,) + state_gdn.shape[2:], state_gdn.dtype),
                jnp.zeros((bp,) + state_sconv.shape[2:], state_sconv.dtype),
                mk_p, mv_p, *wts)
            gconv_p.append(ca)
            gdn_p.append(sa)
            sconv_p.append(cb)
            y_s, ca, sa, cb = even_layer(y_s, state_gdn_conv[e], state_gdn[e], state_sconv[e],
                                         cache_mem_k[i], cache_mem_v[i], *wts)
            gconv_s.append(ca)
            gdn_s.append(sa)
            sconv_s.append(cb)
        else:
            o = i // 2
            wts = (norm_odd[o], w_in_odd[o].astype(BF16), q_norm_c[o], k_norm_c[o],
                   w_out_odd[o].astype(BF16), q_norm_m[i])
            y_p, nk, nv = odd_layer(y_p, None, None, mk_p, mv_p, *wts)
            swk_p.append(nk)
            swv_p.append(nv)
            y_s, nk, nv = odd_layer(y_s, cache_swa_k[o], cache_swa_v[o], cache_mem_k[i], cache_mem_v[i], *wts)
            swk_s.append(nk)
            swv_s.append(nv)
    return (y_p, y_s, jnp.stack(gdn_p), jnp.stack(gdn_s), jnp.stack(gconv_p), jnp.stack(gconv_s),
            jnp.stack(sconv_p), jnp.stack(sconv_s), jnp.stack(swk_p), jnp.stack(swk_s),
            jnp.stack(swv_p), jnp.stack(swv_s), jnp.stack(memk_p), jnp.stack(memv_p))
```

```python
import functools
import math

import jax
import jax.numpy as jnp
from jax import lax
from jax.experimental import pallas as pl
from jax.experimental.pallas import tpu as pltpu

F32 = jnp.float32
BF16 = jnp.bfloat16
EPS = 1e-6
HEAD = 128
SUBLANES = 8
CONV_A = 4
CONV_B = 3
GDN_CHUNK = 64
SWA_PATTERNS = ((128, 1), (512, 4), (2048, 16))
SWA_BLOCK = 128
H_M = 4
VMEM_LIMIT = 48 * 1024 * 1024
HI = lax.Precision.HIGHEST


def _cparams(n_axes):
    return pltpu.CompilerParams(dimension_semantics=("arbitrary",) * n_axes,
                                vmem_limit_bytes=VMEM_LIMIT)


def _silu(x):
    return x * jax.nn.sigmoid(x)


def _dot(a, b, precision=None):
    return jnp.dot(a, b, preferred_element_type=F32, precision=precision)


def _dot_nt(a, b, precision=None):
    return lax.dot_general(a, b, (((1,), (1,)), ((), ())), preferred_element_type=F32,
                           precision=precision)


def _dot_tn(a, b, precision=None):
    return lax.dot_general(a, b, (((0,), (0,)), ((), ())), preferred_element_type=F32,
                           precision=precision)


def _rms_cast_kernel(x_ref, g_ref, o_ref):
    x = x_ref[...]
    ms = jnp.mean(x * x, axis=-1, keepdims=True)
    o_ref[...] = (x * lax.rsqrt(ms + EPS) * g_ref[...]).astype(o_ref.dtype)


def rms_cast(x2d, g, tm):
    m, d = x2d.shape
    return pl.pallas_call(
        _rms_cast_kernel,
        grid=(m // tm,),
        in_specs=[pl.BlockSpec((tm, d), lambda i: (i, 0)),
                  pl.BlockSpec((1, d), lambda i: (0, 0))],
        out_specs=pl.BlockSpec((tm, d), lambda i: (i, 0)),
        out_shape=jax.ShapeDtypeStruct((m, d), BF16),
        compiler_params=_cparams(1),
        name="rms_cast",
    )(x2d, g.reshape(1, d))


def _mm_kernel(*refs, n_in, has_res):
    o_ref = refs[-1]
    acc = _dot(refs[0][...], refs[n_in][...])
    for i in range(1, n_in):
        acc = acc + _dot(refs[i][...], refs[n_in + i][...])
    if has_res:
        acc = acc + refs[2 * n_in][...]
    o_ref[...] = acc.astype(o_ref.dtype)


def matmul(a_list, w, layer, n_off, n_size, tm, tn, res=None):
    m = a_list[0].shape[0]
    n_in = len(a_list)
    in_specs, w_specs = [], []
    row = 0
    for a in a_list:
        k = a.shape[1]
        assert row % k == 0 and n_off % tn == 0 and n_size % tn == 0 and m % tm == 0
        in_specs.append(pl.BlockSpec((tm, k), lambda n, i: (i, 0)))
        w_specs.append(pl.BlockSpec((None, k, tn), functools.partial(
            lambda n, i, rb, nb: (layer, rb, n + nb), rb=row // k, nb=n_off // tn)))
        row += k
    args = list(a_list) + [w] * n_in
    specs = in_specs + w_specs
    if res is not None:
        specs.append(pl.BlockSpec((tm, tn), lambda n, i: (i, n)))
        args.append(res)
    return pl.pallas_call(
        functools.partial(_mm_kernel, n_in=n_in, has_res=res is not None),
        grid=(n_size // tn, m // tm),
        in_specs=specs,
        out_specs=pl.BlockSpec((tm, tn), lambda n, i: (i, n)),
        out_shape=jax.ShapeDtypeStruct((m, n_size), F32),
        compiler_params=_cparams(2),
        name="matmul",
    )(*args)


def _headnorm_kernel(x_ref, g_ref, o_ref, *, heads):
    g = g_ref[...]
    for h in range(heads):
        cols = slice(h * HEAD, (h + 1) * HEAD)
        x = x_ref[:, cols]
        ms = jnp.mean(x * x, axis=-1, keepdims=True)
        o_ref[:, cols] = x * lax.rsqrt(ms + EPS) * g


def headnorm(x2d, col_off, width, gain, tm, cw):
    m = x2d.shape[0]
    assert col_off % cw == 0 and width % cw == 0 and m % tm == 0
    return pl.pallas_call(
        functools.partial(_headnorm_kernel, heads=cw // HEAD),
        grid=(m // tm, width // cw),
        in_specs=[pl.BlockSpec((tm, cw), lambda i, c: (i, c + col_off // cw)),
                  pl.BlockSpec((1, HEAD), lambda i, c: (0, 0))],
        out_specs=pl.BlockSpec((tm, cw), lambda i, c: (i, c)),
        out_shape=jax.ShapeDtypeStruct((m, width), F32),
        compiler_params=_cparams(2),
        name="headnorm",
    )(x2d, gain.reshape(1, HEAD))


def _odd_qkv_kernel(*refs, heads, aliased):
    q_ref, k_ref, v_ref, qg_ref, kg_ref = refs[:5]
    qo_ref, ko_ref, vo_ref = refs[5 + (2 if aliased else 0):]
    qg = qg_ref[...]
    kg = kg_ref[...]
    for h in range(heads):
        cols = slice(h * HEAD, (h + 1) * HEAD)
        q = q_ref[:, cols]
        qo_ref[:, cols] = q * lax.rsqrt(jnp.mean(q * q, axis=-1, keepdims=True) + EPS) * qg
        k = k_ref[:, cols]
        ko_ref[:, cols] = k * lax.rsqrt(jnp.mean(k * k, axis=-1, keepdims=True) + EPS) * kg
    vo_ref[...] = v_ref[...]


def odd_qkv(proj, w_c, q_gain, k_gain, layer, n_layers, k_prev, v_prev, tm, cw):
    m = proj.shape[0]
    aliased = k_prev is not None
    assert w_c % cw == 0 and m % tm == 0
    nb = w_c // cw
    blk = (tm, cw)
    in_specs = [pl.BlockSpec(blk, lambda i, c: (i, c)),
                pl.BlockSpec(blk, lambda i, c: (i, c + nb)),
                pl.BlockSpec(blk, lambda i, c: (i, c + 2 * nb)),
                pl.BlockSpec((1, HEAD), lambda i, c: (0, 0)),
                pl.BlockSpec((1, HEAD), lambda i, c: (0, 0))]
    args = [proj, proj, proj, q_gain.reshape(1, HEAD), k_gain.reshape(1, HEAD)]
    aliases = {}
    if aliased:
        in_specs += [pl.BlockSpec(memory_space=pl.ANY), pl.BlockSpec(memory_space=pl.ANY)]
        args += [k_prev, v_prev]
        aliases = {5: 1, 6: 2}
    stack = jax.ShapeDtypeStruct((n_layers, m, w_c), F32)
    return pl.pallas_call(
        functools.partial(_odd_qkv_kernel, heads=cw // HEAD, aliased=aliased),
        grid=(m // tm, nb),
        in_specs=in_specs,
        out_specs=[pl.BlockSpec(blk, lambda i, c: (i, c)),
                   pl.BlockSpec((None,) + blk, lambda i, c: (layer, i, c)),
                   pl.BlockSpec((None,) + blk, lambda i, c: (layer, i, c))],
        out_shape=[jax.ShapeDtypeStruct((m, w_c), F32), stack, stack],
        input_output_aliases=aliases,
        compiler_params=_cparams(2),
        name="odd_qkv",
    )(*args)


def _gate_kernel(x_ref, alog_ref, dt_ref, beta_ref, gam_ref, gsm_ref, *, n_heads, cs):
    x = x_ref[...]
    tm = x.shape[0]
    beta = jax.nn.sigmoid(x)
    g = -jnp.exp(alog_ref[...]) * jax.nn.softplus(x + dt_ref[...])
    ri = lax.broadcasted_iota(jnp.int32, (tm, tm), 0)
    ci = lax.broadcasted_iota(jnp.int32, (tm, tm), 1)
    tri = jnp.where((ci <= ri) & (ci >= (ri // cs) * cs), 1.0, 0.0).astype(F32)
    gam = _dot(tri, g, HI)
    gsm_ref[...] = gam
    wide = n_heads * HEAD
    src = lax.broadcasted_iota(jnp.int32, (HEAD, wide), 0)
    dst = lax.broadcasted_iota(jnp.int32, (HEAD, wide), 1) // HEAD
    beta_ref[...] = _dot(beta, jnp.where(src == dst, 1.0, 0.0).astype(F32), HI)
    gam_ref[...] = _dot(gam, jnp.where(src == dst + n_heads, 1.0, 0.0).astype(F32), HI)


def gdn_gates(small2d, a_log, dt_bias, tm, cs):
    m = small2d.shape[0]
    n_heads = a_log.shape[0]
    assert tm % cs == 0 and m % tm == 0
    pad = HEAD - 2 * n_heads
    alog_row = jnp.concatenate([jnp.zeros((n_heads,), F32), a_log, jnp.zeros((pad,), F32)]).reshape(1, HEAD)
    dt_row = jnp.concatenate([jnp.zeros((n_heads,), F32), dt_bias, jnp.zeros((pad,), F32)]).reshape(1, HEAD)
    wide = n_heads * HEAD
    return pl.pallas_call(
        functools.partial(_gate_kernel, n_heads=n_heads, cs=cs),
        grid=(m // tm,),
        in_specs=[pl.BlockSpec((tm, HEAD), lambda i: (i, 0)),
                  pl.BlockSpec((1, HEAD), lambda i: (0, 0)),
                  pl.BlockSpec((1, HEAD), lambda i: (0, 0))],
        out_specs=[pl.BlockSpec((tm, wide), lambda i: (i, 0)),
                   pl.BlockSpec((tm, wide), lambda i: (i, 0)),
                   pl.BlockSpec((tm, HEAD), lambda i: (i, 0))],
        out_shape=[jax.ShapeDtypeStruct((m, wide), F32), jax.ShapeDtypeStruct((m, wide), F32),
                   jax.ShapeDtypeStruct((m, HEAD), F32)],
        compiler_params=_cparams(1),
        name="gdn_gates",
    )(small2d, alog_row, dt_row)


def _conv_qkv_kernel(x_ref, halo_ref, st_ref, w_ref, o_ref, buf_ref, *, tt, n_q, n_qk, heads):
    c = pl.program_id(1)
    t = pl.program_id(2)
    buf_ref[0:SUBLANES, :] = jnp.where(t == 0, st_ref[0], halo_ref[0])
    buf_ref[SUBLANES:SUBLANES + tt, :] = x_ref[0]
    w = w_ref[...]
    first = SUBLANES - (CONV_A - 1)
    y = buf_ref[first:first + tt, :] * w[0:1, :]
    for j in range(1, CONV_A):
        y = y + buf_ref[first + j:first + j + tt, :] * w[j:j + 1, :]
    y = _silu(y)

    @pl.when(c < n_qk)
    def _():
        scale = jnp.where(c < n_q, HEAD ** -0.5, 1.0).astype(F32)
        for h in range(heads):
            cols = slice(h * HEAD, (h + 1) * HEAD)
            yh = y[:, cols]
            ss = jnp.sum(yh * yh, axis=-1, keepdims=True)
            o_ref[0, :, cols] = yh * lax.rsqrt(ss + EPS) * scale

    @pl.when(c >= n_qk)
    def _():
        o_ref[0] = y


def conv_qkv(proj3d, state_pad, conv_w, width, tt, cw):
    bn, t, _ = proj3d.shape
    third = width // 3
    assert third % cw == 0 and t % tt == 0 and tt % SUBLANES == 0
    rb = tt // SUBLANES
    return pl.pallas_call(
        functools.partial(_conv_qkv_kernel, tt=tt, n_q=third // cw, n_qk=2 * third // cw, heads=cw // HEAD),
        grid=(bn, width // cw, t // tt),
        in_specs=[pl.BlockSpec((1, tt, cw), lambda b, c, i: (b, i, c)),
                  pl.BlockSpec((1, SUBLANES, cw), lambda b, c, i: (b, jnp.maximum(i * rb - 1, 0), c)),
                  pl.BlockSpec((1, SUBLANES, cw), lambda b, c, i: (b, 0, c)),
                  pl.BlockSpec((CONV_A, cw), lambda b, c, i: (0, c))],
        out_specs=pl.BlockSpec((1, tt, cw), lambda b, c, i: (b, i, c)),
        out_shape=jax.ShapeDtypeStruct((bn, t, width), F32),
        scratch_shapes=[pltpu.VMEM((tt + SUBLANES, cw), F32)],
        compiler_params=_cparams(3),
        name="conv_qkv",
    )(proj3d, proj3d, state_pad, conv_w)


def _sconv_kernel(bg_ref, cg_ref, xb_ref, zb_ref, hc_ref, hx_ref, st_ref, w_ref, o_ref, nb_ref, buf_ref, *, tt):
    t = pl.program_id(2)
    buf_ref[0:SUBLANES, :] = jnp.where(t == 0, st_ref[0], hc_ref[0] * hx_ref[0])
    buf_ref[SUBLANES:SUBLANES + tt, :] = cg_ref[0] * xb_ref[0]
    w = w_ref[...]
    first = SUBLANES - (CONV_B - 1)
    y = buf_ref[first:first + tt, :] * w[0:1, :]
    for j in range(1, CONV_B):
        y = y + buf_ref[first + j:first + j + tt, :] * w[j:j + 1, :]
    o_ref[0] = (bg_ref[0] * y * _silu(zb_ref[0])).astype(o_ref.dtype)
    nb_ref[0] = buf_ref[tt:tt + SUBLANES, :]


def short_conv(proj3d, off_b, off_c, off_x, off_z, width, state_pad, conv_w, tt, cw):
    bn, t, _ = proj3d.shape
    assert t % tt == 0 and width % cw == 0
    rb = tt // SUBLANES

    def col(off):
        assert off % cw == 0
        return functools.partial(lambda b, c, i, o: (b, i, c + o), o=off // cw)

    def halo(off):
        return functools.partial(lambda b, c, i, o: (b, jnp.maximum(i * rb - 1, 0), c + o), o=off // cw)

    blk = (1, tt, cw)
    hblk = (1, SUBLANES, cw)
    return pl.pallas_call(
        functools.partial(_sconv_kernel, tt=tt),
        grid=(bn, width // cw, t // tt),
        in_specs=[pl.BlockSpec(blk, col(off_b)), pl.BlockSpec(blk, col(off_c)),
                  pl.BlockSpec(blk, col(off_x)), pl.BlockSpec(blk, col(off_z)),
                  pl.BlockSpec(hblk, halo(off_c)), pl.BlockSpec(hblk, halo(off_x)),
                  pl.BlockSpec(hblk, lambda b, c, i: (b, 0, c)),
                  pl.BlockSpec((CONV_B, cw), lambda b, c, i: (0, c))],
        out_specs=[pl.BlockSpec(blk, lambda b, c, i: (b, i, c)),
                   pl.BlockSpec(hblk, lambda b, c, i: (b, 0, c))],
        out_shape=[jax.ShapeDtypeStruct((bn, t, width), BF16),
                   jax.ShapeDtypeStruct((bn, SUBLANES, width), F32)],
        scratch_shapes=[pltpu.VMEM((tt + SUBLANES, cw), F32)],
        compiler_params=_cparams(3),
        name="short_conv",
    )(proj3d, proj3d, proj3d, proj3d, proj3d, proj3d, state_pad, conv_w)


def _split_bf16(x):
    hi = x.astype(BF16)
    return hi, (x - hi.astype(F32)).astype(BF16)


def _dot_3pass(a, b):
    a_hi, a_lo = _split_bf16(a)
    b_hi, b_lo = _split_bf16(b)
    return _dot(a_hi, b_hi) + (_dot(a_hi, b_lo) + _dot(a_lo, b_hi))


def _gdn_kernel(q_ref, k_ref, v_ref, z_ref, beta_ref, gam_ref, grow_ref, s0_ref, gain_ref, o_ref, s_ref,
                *, hb, cs, nct):
    ti = pl.program_id(2)

    @pl.when(ti == 0)
    def _():
        s_ref[...] = s0_ref[...]

    ii = lax.broadcasted_iota(jnp.int32, (cs, cs), 0)
    jj = lax.broadcasted_iota(jnp.int32, (cs, cs), 1)
    causal = ii >= jj
    strict = ii > jj
    eye_f = (ii == jj).astype(F32)
    gain = gain_ref[...]
    n_double = int(math.log2(cs)) - 1
    chains = [(j, h) for j in range(nct) for h in range(hb)]

    def tile(ref, c):
        return ref[0, c[0] * cs:(c[0] + 1) * cs, c[1] * HEAD:(c[1] + 1) * HEAD]

    k = [tile(k_ref, c) for c in chains]
    beta = [tile(beta_ref, c) for c in chains]
    gam = [tile(gam_ref, c) for c in chains]
    kb = [a * b for a, b in zip(k, beta)]
    raw = [_dot_nt(jnp.concatenate([b, tile(q_ref, c)], axis=0).astype(BF16), a.astype(BF16))
           for a, b, c in zip(k, kb, chains)]
    decay = []
    for g, (j, h) in zip(gam, chains):
        gam_row = grow_ref[0, h, pl.ds(ti * nct + j, 1), :]
        decay.append(jnp.where(causal, jnp.exp(jnp.where(causal, g[:, :cs] - gam_row, 0.0)), 0.0))
    qk = [jnp.where(causal, r[cs:] * d, 0.0) for r, d in zip(raw, decay)]
    pw = [jnp.where(strict, -(r[:cs] * d), 0.0) for r, d in zip(raw, decay)]
    inv = [eye_f + p for p in pw]
    pw = [_dot_3pass(p, p) for p in pw]
    for _ in range(n_double - 1):
        prod = [_dot_3pass(jnp.concatenate([p, x], axis=0), p) for p, x in zip(pw, inv)]
        pw = [p[:cs] for p in prod]
        inv = [x + p[cs:] for x, p in zip(inv, prod)]
    inv = [x + _dot_3pass(x, p) for x, p in zip(inv, pw)]
    e_gam = [jnp.exp(g) for g in gam]
    uw = [_dot_3pass(x, jnp.concatenate([tile(v_ref, c) * b, a * e], axis=1))
          for x, c, b, a, e in zip(inv, chains, beta, kb, e_gam)]
    wq = [jnp.concatenate([x[:, HEAD:], tile(q_ref, c) * e], axis=0).astype(BF16)
          for x, c, e in zip(uw, chains, e_gam)]
    k_dec_t = [(a * jnp.exp(g[cs - 1:cs, :] - g)).T.astype(BF16) for a, g in zip(k, gam)]
    state = [s_ref[0, h] for h in range(hb)]
    for j in range(nct):
        idx = [j * hb + h for h in range(hb)]
        ws = [_dot(wq[i], state[h].astype(BF16)) for h, i in enumerate(idx)]
        v16 = [(uw[i][:, :HEAD] - x[:cs]).astype(BF16) for x, i in zip(ws, idx)]
        o = [x[cs:] + _dot(qk[i].astype(BF16), y) for x, y, i in zip(ws, v16, idx)]
        state = [s * jnp.exp(gam[i][cs - 1:cs, :]) + _dot(k_dec_t[i], y) for s, y, i in zip(state, v16, idx)]
        for h, i in enumerate(idx):
            ms = jnp.mean(o[h] * o[h], axis=-1, keepdims=True)
            on = o[h] * lax.rsqrt(ms + EPS) * gain
            o_ref[0, j * cs:(j + 1) * cs, h * HEAD:(h + 1) * HEAD] = (
                on * _silu(tile(z_ref, (j, h)))).astype(o_ref.dtype)
    for h in range(hb):
        s_ref[0, h] = state[h]


def gdn(qkv_act, proj3d, z_off, beta_rep, gam_rep, gam_row4, s0, gain, hb, tt):
    bn, t, w3 = qkv_act.shape
    n_heads = w3 // (3 * HEAD)
    nc, cs = gam_row4.shape[2], gam_row4.shape[3]
    cw = hb * HEAD
    ng = n_heads // hb
    assert z_off % cw == 0 and n_heads % hb == 0 and t % tt == 0 and tt % cs == 0
    blk = (1, tt, cw)
    sblk = (1, hb, HEAD, HEAD)
    return pl.pallas_call(
        functools.partial(_gdn_kernel, hb=hb, cs=cs, nct=tt // cs),
        grid=(bn, ng, t // tt),
        in_specs=[pl.BlockSpec(blk, lambda b, h, i: (b, i, h)),
                  pl.BlockSpec(blk, lambda b, h, i: (b, i, ng + h)),
                  pl.BlockSpec(blk, lambda b, h, i: (b, i, 2 * ng + h)),
                  pl.BlockSpec(blk, lambda b, h, i: (b, i, z_off // cw + h)),
                  pl.BlockSpec(blk, lambda b, h, i: (b, i, h)),
                  pl.BlockSpec(blk, lambda b, h, i: (b, i, h)),
                  pl.BlockSpec((1, hb, nc, cs), lambda b, h, i: (b, h, 0, 0)),
                  pl.BlockSpec(sblk, lambda b, h, i: (b, h, 0, 0)),
                  pl.BlockSpec((1, HEAD), lambda b, h, i: (0, 0))],
        out_specs=[pl.BlockSpec(blk, lambda b, h, i: (b, i, h)),
                   pl.BlockSpec(sblk, lambda b, h, i: (b, h, 0, 0))],
        out_shape=[jax.ShapeDtypeStruct((bn, t, n_heads * HEAD), BF16),
                   jax.ShapeDtypeStruct((bn, n_heads, HEAD, HEAD), F32)],
        compiler_params=_cparams(3),
        name="gdn",
    )(qkv_act, qkv_act, qkv_act, proj3d, beta_rep, gam_rep, gam_row4, s0, gain.reshape(1, HEAD))


def _mem_attn_kernel(q_ref, z_ref, mk_ref, mv_ref, gain_ref, o_ref):
    gain = gain_ref[...]
    for h in range(H_M):
        cols = slice(h * HEAD, (h + 1) * HEAD)
        q = q_ref[0, :, cols]
        ms = jnp.mean(q * q, axis=-1, keepdims=True)
        qn = (q * lax.rsqrt(ms + EPS) * gain).astype(BF16)
        sc = _dot_nt(qn, mk_ref[0, :, cols].astype(BF16)) * (HEAD ** -0.5)
        m = jnp.max(sc, axis=-1, keepdims=True)
        p = jnp.exp(sc - m)
        p = p / jnp.sum(p, axis=-1, keepdims=True)
        o = _dot(p.astype(BF16), mv_ref[0, :, cols].astype(BF16))
        o_ref[0, :, cols] = (o * _silu(z_ref[0, :, cols])).astype(o_ref.dtype)


def mem_attn(proj3d, q_off, z_off, mem_k, mem_v, q_gain, tt):
    bn, t, _ = proj3d.shape
    n_mem = mem_k.shape[1]
    wm = H_M * HEAD
    assert q_off % wm == 0 and z_off % wm == 0 and t % tt == 0
    return pl.pallas_call(
        _mem_attn_kernel,
        grid=(bn, t // tt),
        in_specs=[pl.BlockSpec((1, tt, wm), lambda b, i: (b, i, q_off // wm)),
                  pl.BlockSpec((1, tt, wm), lambda b, i: (b, i, z_off // wm)),
                  pl.BlockSpec((1, n_mem, wm), lambda b, i: (b, 0, 0)),
                  pl.BlockSpec((1, n_mem, wm), lambda b, i: (b, 0, 0)),
                  pl.BlockSpec((1, HEAD), lambda b, i: (0, 0))],
        out_specs=pl.BlockSpec((1, tt, wm), lambda b, i: (b, i, 0)),
        out_shape=jax.ShapeDtypeStruct((bn, t, wm), BF16),
        compiler_params=_cparams(2),
        name="mem_attn",
    )(proj3d, proj3d, mem_k.reshape(bn, n_mem, wm), mem_v.reshape(bn, n_mem, wm), q_gain.reshape(1, HEAD))


def _dil_prompt_kernel(q_ref, k_ref, v_ref, z_ref, o_ref, acc_ref, lse_ref, *, hg, t, group):
    qi = lax.broadcasted_iota(jnp.int32, (SWA_BLOCK, SWA_BLOCK), 0)
    ki = lax.broadcasted_iota(jnp.int32, (SWA_BLOCK, SWA_BLOCK), 1)
    n_pat = len(SWA_PATTERNS)
    for h in range(hg):
        cols = slice(h * HEAD, (h + 1) * HEAD)
        for p, (window, dil) in enumerate(SWA_PATTERNS):
            assert window // dil == SWA_BLOCK
            l_sub = t // dil
            nb = l_sub // SWA_BLOCK
            assert nb * SWA_BLOCK * dil == t

            def rows_of(r, c, dil=dil):
                return pl.ds(r + c * (SWA_BLOCK * dil), SWA_BLOCK, stride=dil)

            blocks = [(r, c) for r in range(dil) for c in range(nb)]
            for g0 in range(0, len(blocks), group):
                grp = blocks[g0:g0 + group]
                q = [(q_ref[0, rows_of(r, c), cols] * (HEAD ** -0.5)).astype(BF16) for r, c in grp]
                s_cur = [jnp.where(ki <= qi, _dot_nt(x, k_ref[0, rows_of(r, c), cols].astype(BF16)), -jnp.inf)
                         for x, (r, c) in zip(q, grp)]
                s_prev = [jnp.where(ki >= qi, _dot_nt(x, k_ref[0, rows_of(r, c - 1), cols].astype(BF16)), -jnp.inf)
                          if c > 0 else None for x, (r, c) in zip(q, grp)]
                m = [jnp.max(a, axis=-1, keepdims=True) if b is None else
                     jnp.maximum(jnp.max(a, axis=-1, keepdims=True), jnp.max(b, axis=-1, keepdims=True))
                     for a, b in zip(s_cur, s_prev)]
                e_cur = [jnp.exp(a - x) for a, x in zip(s_cur, m)]
                e_prev = [None if b is None else jnp.exp(b - x) for b, x in zip(s_prev, m)]
                den = [jnp.sum(a, axis=-1, keepdims=True) if b is None else
                       jnp.sum(a, axis=-1, keepdims=True) + jnp.sum(b, axis=-1, keepdims=True)
                       for a, b in zip(e_cur, e_prev)]
                o = [_dot(a.astype(BF16), v_ref[0, rows_of(r, c), cols].astype(BF16)) for a, (r, c) in zip(e_cur, grp)]
                o = [x if b is None else x + _dot(b.astype(BF16), v_ref[0, rows_of(r, c - 1), cols].astype(BF16))
                     for x, b, (r, c) in zip(o, e_prev, grp)]
                for x, d, mx, (r, c) in zip(o, den, m, grp):
                    acc_ref[p, rows_of(r, c), :] = x / d
                    lse_ref[p, rows_of(r, c), :] = jnp.broadcast_to(mx + jnp.log(d), (SWA_BLOCK, HEAD))
        lse = [lse_ref[p] for p in range(n_pat)]
        top = functools.reduce(jnp.maximum, lse)
        wgt = [jnp.exp(x - top) for x in lse]
        tot = functools.reduce(lambda a, b: a + b, wgt)
        mix = functools.reduce(lambda a, b: a + b, [(wgt[p] / tot) * acc_ref[p] for p in range(n_pat)])
        o_ref[0, :, cols] = (mix * _silu(z_ref[0, :, cols])).astype(o_ref.dtype)


def dil_prompt(qn, k_stack, v_stack, layer, proj3d, z_off, hg):
    bn, t, wc = qn.shape
    cw = hg * HEAD
    assert z_off % cw == 0 and wc % cw == 0
    blk = (1, t, cw)
    sblk = (None, 1, t, cw)
    return pl.pallas_call(
        functools.partial(_dil_prompt_kernel, hg=hg, t=t, group=4),
        grid=(bn, wc // cw),
        in_specs=[pl.BlockSpec(blk, lambda b, h: (b, 0, h)),
                  pl.BlockSpec(sblk, lambda b, h: (layer, b, 0, h)),
                  pl.BlockSpec(sblk, lambda b, h: (layer, b, 0, h)),
                  pl.BlockSpec(blk, lambda b, h: (b, 0, z_off // cw + h))],
        out_specs=pl.BlockSpec(blk, lambda b, h: (b, 0, h)),
        out_shape=jax.ShapeDtypeStruct((bn, t, wc), BF16),
        scratch_shapes=[pltpu.VMEM((len(SWA_PATTERNS), t, HEAD), F32),
                        pltpu.VMEM((len(SWA_PATTERNS), t, HEAD), F32)],
        compiler_params=_cparams(2),
        name="dil_prompt",
    )(qn, k_stack, v_stack, proj3d)


def _dil_sample_kernel(q_ref, kn_ref, vn_ref, z_ref, kc_ref, vc_ref, o_ref, *, hg, t, wb):
    row_c = lax.broadcasted_iota(jnp.int32, (t, wb), 0)
    col_c = lax.broadcasted_iota(jnp.int32, (t, wb), 1)
    delta_c = wb + row_c - col_c
    row_n = lax.broadcasted_iota(jnp.int32, (t, t), 0)
    col_n = lax.broadcasted_iota(jnp.int32, (t, t), 1)
    delta_n = row_n - col_n
    for h in range(hg):
        cols = slice(h * HEAD, (h + 1) * HEAD)
        q = q_ref[0, :, cols].astype(BF16)
        vc = vc_ref[0, :, cols].astype(BF16)
        vn = vn_ref[0, :, cols].astype(BF16)
        s_c = _dot_nt(q, kc_ref[0, :, cols].astype(BF16)) * (HEAD ** -0.5)
        s_n = _dot_nt(q, kn_ref[0, :, cols].astype(BF16)) * (HEAD ** -0.5)
        outs, lses = [], []
        for window, dil in SWA_PATTERNS:
            ok_c = ((delta_c & (dil - 1)) == 0) & (delta_c <= window)
            ok_n = (delta_n >= 0) & ((delta_n & (dil - 1)) == 0) & (delta_n <= window)
            m_c = jnp.where(ok_c, s_c, -jnp.inf)
            m_n = jnp.where(ok_n, s_n, -jnp.inf)
            m = jnp.maximum(jnp.max(m_c, axis=-1, keepdims=True), jnp.max(m_n, axis=-1, keepdims=True))
            e_c = jnp.exp(m_c - m)
            e_n = jnp.exp(m_n - m)
            den = jnp.sum(e_c, axis=-1, keepdims=True) + jnp.sum(e_n, axis=-1, keepdims=True)
            outs.append(_dot((e_c / den).astype(BF16), vc) + _dot((e_n / den).astype(BF16), vn))
            lses.append(m + jnp.log(den))
        top = functools.reduce(jnp.maximum, lses)
        wgt = [jnp.exp(x - top) for x in lses]
        tot = functools.reduce(lambda a, b: a + b, wgt)
        mix = functools.reduce(lambda a, b: a + b, [(w / tot) * o for w, o in zip(wgt, outs)])
        o_ref[0, :, cols] = (mix * _silu(z_ref[0, :, cols])).astype(o_ref.dtype)


def dil_sample(qn, k_stack, v_stack, layer, proj3d, z_off, cache_k, cache_v, hg):
    bn, t, wc = qn.shape
    wb = cache_k.shape[2]
    cw = hg * HEAD
    assert z_off % cw == 0 and wc % cw == 0
    for window, dil in SWA_PATTERNS:
        assert dil & (dil - 1) == 0
    blk = (1, t, cw)
    sblk = (None, 1, t, cw)
    cblk = (None, 1, wb, cw)
    return pl.pallas_call(
        functools.partial(_dil_sample_kernel, hg=hg, t=t, wb=wb),
        grid=(bn, wc // cw),
        in_specs=[pl.BlockSpec(blk, lambda b, h: (b, 0, h)),
                  pl.BlockSpec(sblk, lambda b, h: (layer, b, 0, h)),
                  pl.BlockSpec(sblk, lambda b, h: (layer, b, 0, h)),
                  pl.BlockSpec(blk, lambda b, h: (b, 0, z_off // cw + h)),
                  pl.BlockSpec(cblk, lambda b, h: (layer, b, 0, h)),
                  pl.BlockSpec(cblk, lambda b, h: (layer, b, 0, h))],
        out_specs=pl.BlockSpec(blk, lambda b, h: (b, 0, h)),
        out_shape=jax.ShapeDtypeStruct((bn, t, wc), BF16),
        compiler_params=_cparams(2),
        name="dil_sample",
    )(qn, k_stack, v_stack, proj3d, cache_k, cache_v)


def _row_tile(m, pref):
    return pref if m % pref == 0 else m


def _in_proj(x2d, g_norm, w_stack, layer, n_main):
    m = x2d.shape[0]
    h = rms_cast(x2d, g_norm, _row_tile(m, 256))
    return h, matmul([h], w_stack, layer, 0, n_main, _row_tile(m, 512), 1024)


def _pad_state(buf):
    return jnp.pad(buf, ((0, 0), (SUBLANES - buf.shape[1], 0), (0, 0)))


def even_layer(x, conv_a_buf, s0, conv_b_buf, mem_k, mem_v, layer, g_norm, w_main, w_small, conv_a_w, a_log,
               dt_bias, gdn_gain, conv_b_w, w_out, q_gain_m):
    bn, t, d = x.shape
    m = bn * t
    n_heads = a_log.shape[0]
    w_a = n_heads * HEAD
    w_b = conv_b_w.shape[1]
    w_m = H_M * HEAD
    off_za = 3 * w_a
    off_b = off_za + w_a
    off_c, off_x, off_zb = off_b + w_b, off_b + 2 * w_b, off_b + 3 * w_b
    off_qm = off_b + 4 * w_b
    off_zm = off_qm + w_m
    n_main = off_zm + w_m
    x2d = x.reshape(m, d)
    h, proj = _in_proj(x2d, g_norm, w_main, layer, n_main)
    small = matmul([h], w_small, layer, 0, HEAD, _row_tile(m, 512), HEAD)
    proj3d = proj.reshape(bn, t, n_main)
    tt = _row_tile(t, 512)
    qkv_act = conv_qkv(proj3d, _pad_state(conv_a_buf), conv_a_w, 3 * w_a, tt, 512)
    cs = min(GDN_CHUNK, t)
    nc = t // cs
    beta_rep, gam_rep, gam_small = gdn_gates(small, a_log, dt_bias, _row_tile(m, 512), cs)
    gam_row4 = (gam_small.reshape(bn, t, HEAD)[:, :, n_heads:2 * n_heads]
                .transpose(0, 2, 1).reshape(bn, n_heads, nc, cs))
    o_a, new_s = gdn(qkv_act, proj3d, off_za, beta_rep.reshape(bn, t, w_a), gam_rep.reshape(bn, t, w_a),
                     gam_row4, s0, gdn_gain, 4, _row_tile(t, 2 * GDN_CHUNK))
    new_conv_a = proj3d[:, t - (CONV_A - 1):, :3 * w_a]
    o_b, nb8 = short_conv(proj3d, off_b, off_c, off_x, off_zb, w_b, _pad_state(conv_b_buf), conv_b_w, tt, 512)
    new_conv_b = nb8[:, SUBLANES - (CONV_B - 1):, :]
    o_m = mem_attn(proj3d, off_qm, off_zm, mem_k, mem_v, q_gain_m, tt)
    y = matmul([o_a.reshape(m, w_a), o_b.reshape(m, w_b), o_m.reshape(m, w_m)], w_out, layer, 0, d,
               _row_tile(m, 512), 1024, res=x2d)
    return y.reshape(bn, t, d), new_conv_a, new_s, new_conv_b


def odd_layer(x, cache_k, cache_v, mem_k, mem_v, layer, n_layers, k_prev, v_prev, g_norm, w_main, q_gain_c,
              k_gain_c, w_out, q_gain_m):
    bn, t, d = x.shape
    m = bn * t
    w_c = d
    w_m = H_M * HEAD
    off_z = 3 * w_c
    off_qm = 4 * w_c
    off_zm = off_qm + w_m
    n_main = off_zm + w_m
    x2d = x.reshape(m, d)
    _, proj = _in_proj(x2d, g_norm, w_main, layer, n_main)
    proj3d = proj.reshape(bn, t, n_main)
    tm = _row_tile(m, 512)
    qn, k_stack, v_stack = odd_qkv(proj, w_c, q_gain_c, k_gain_c, layer, n_layers, k_prev, v_prev, tm, 512)
    qn = qn.reshape(bn, t, w_c)
    k4 = k_stack.reshape(n_layers, bn, t, w_c)
    v4 = v_stack.reshape(n_layers, bn, t, w_c)
    if cache_k is None:
        o_c = dil_prompt(qn, k4, v4, layer, proj3d, off_z, 1)
    else:
        o_c = dil_sample(qn, k4, v4, layer, proj3d, off_z, cache_k, cache_v, 4)
    o_m = mem_attn(proj3d, off_qm, off_zm, mem_k, mem_v, q_gain_m, _row_tile(t, 512))
    y = matmul([o_c.reshape(m, w_c), o_m.reshape(m, w_m)], w_out, layer, 0, d, tm, 1024, res=x2d)
    return y.reshape(bn, t, d), k_stack, v_stack


def mem_project(mem, g, w_kv_stack, layer, k_gain):
    bn, n_mem, d = mem.shape
    m = bn * n_mem
    w_m = H_M * HEAD
    hm = rms_cast(mem.reshape(m, d), g, _row_tile(m, 256))
    kv = matmul([hm], w_kv_stack, layer, 0, 2 * w_m, _row_tile(m, 512), w_m)
    k = headnorm(kv, 0, w_m, k_gain, _row_tile(m, 512), w_m)
    return k.reshape(bn, n_mem, H_M, HEAD), kv[:, w_m:].reshape(bn, n_mem, H_M, HEAD)


def _prep_even_weights(w_in, n_heads):
    w_a = n_heads * HEAD
    cut = 4 * w_a
    w_main = jnp.concatenate([w_in[:, :, :cut], w_in[:, :, cut + 2 * n_heads:]], axis=2).astype(BF16)
    w_small = jnp.pad(w_in[:, :, cut:cut + 2 * n_heads], ((0, 0), (0, 0), (0, HEAD - 2 * n_heads))).astype(BF16)
    return w_main, w_small


def kernel(x_prompt, x_sample, state_gdn, state_gdn_conv, state_sconv, cache_swa_k, cache_swa_v, cache_mem_k, cache_mem_v, mem_prompt, norm_even, w_in_even, conv_a, a_log, dt_bias, gdn_norm, conv_b, w_out_even, norm_odd, w_in_odd, q_norm_c, k_norm_c, w_out_odd, mem_norm, w_mem_kv, q_norm_m, k_norm_m):
    bp, t_p, d = x_prompt.shape
    bs, t_s, _ = x_sample.shape
    depth = mem_norm.shape[0]
    n_odd = norm_odd.shape[0]
    n_heads_a = a_log.shape[1]
    n_heads_c = cache_swa_k.shape[3]
    wb = cache_swa_k.shape[2]
    w_even_main, w_even_small = _prep_even_weights(w_in_even, n_heads_a)
    w_odd_main = w_in_odd.astype(BF16)
    w_out_e = w_out_even.astype(BF16)
    w_out_o = w_out_odd.astype(BF16)
    w_kv = w_mem_kv.astype(BF16)
    cache_k = cache_swa_k.reshape(n_odd, bs, wb, d)
    cache_v = cache_swa_v.reshape(n_odd, bs, wb, d)
    y_p, y_s = x_prompt, x_sample
    gdn_p, gdn_s, gconv_p, gconv_s, sconv_p, sconv_s = [], [], [], [], [], []
    memk_p, memv_p = [], []
    kp = vp = ks = vs = None
    for i in range(depth):
        mk_p, mv_p = mem_project(mem_prompt, mem_norm[i], w_kv, i, k_norm_m[i])
        memk_p.append(mk_p)
        memv_p.append(mv_p)
        if i % 2 == 0:
            e = i // 2
            wts = (e, norm_even[e], w_even_main, w_even_small, conv_a[e], a_log[e], dt_bias[e], gdn_norm[e],
                   conv_b[e], w_out_e, q_norm_m[i])
            y_p, ca, sa, cb = even_layer(
                y_p,
                jnp.zeros((bp,) + state_gdn_conv.shape[2:], state_gdn_conv.dtype),
                jnp.zeros((bp,) + state_gdn.shape[2:], state_gdn.dtype),
                jnp.zeros((bp,) + state_sconv.shape[2:], state_sconv.dtype),
                mk_p, mv_p, *wts)
            gconv_p.append(ca)
            gdn_p.append(sa)
            sconv_p.append(cb)
            y_s, ca, sa, cb = even_layer(y_s, state_gdn_conv[e], state_gdn[e], state_sconv[e],
                                         cache_mem_k[i], cache_mem_v[i], *wts)
            gconv_s.append(ca)
            gdn_s.append(sa)
            sconv_s.append(cb)
        else:
            o = i // 2
            wts = (norm_odd[o], w_odd_main, q_norm_c[o], k_norm_c[o], w_out_o, q_norm_m[i])
            y_p, kp, vp = odd_layer(y_p, None, None, mk_p, mv_p, o, n_odd, kp, vp, *wts)
            y_s, ks, vs = odd_layer(y_s, cache_k, cache_v, cache_mem_k[i], cache_mem_v[i], o, n_odd, ks, vs, *wts)
    n_keep = min(max(w for w, _ in SWA_PATTERNS), t_p)
    swa_k_p = kp.reshape(n_odd, bp, t_p, n_heads_c, HEAD)[:, :, t_p - n_keep:]
    swa_v_p = vp.reshape(n_odd, bp, t_p, n_heads_c, HEAD)[:, :, t_p - n_keep:]
    swa_k_s = jnp.concatenate([cache_k[:, :, t_s:], ks.reshape(n_odd, bs, t_s, d)], axis=2)
    swa_v_s = jnp.concatenate([cache_v[:, :, t_s:], vs.reshape(n_odd, bs, t_s, d)], axis=2)
    swa_k_s = swa_k_s.reshape(n_odd, bs, wb, n_heads_c, HEAD)
    swa_v_s = swa_v_s.reshape(n_odd, bs, wb, n_heads_c, HEAD)
    return (y_p, y_s, jnp.stack(gdn_p), jnp.stack(gdn_s), jnp.stack(gconv_p), jnp.stack(gconv_s),
            jnp.stack(sconv_p), jnp.stack(sconv_s), swa_k_p, swa_k_s, swa_v_p, swa_v_s,
            jnp.stack(memk_p), jnp.stack(memv_p))
```

```python
import functools
import math

import jax
import jax.numpy as jnp
from jax import lax
from jax.experimental import pallas as pl
from jax.experimental.pallas import tpu as pltpu

F32 = jnp.float32
BF16 = jnp.bfloat16
EPS = 1e-6
HEAD = 128
SUBLANES = 8
CONV_A = 4
CONV_B = 3
GDN_CHUNK = 64
SWA_PATTERNS = ((128, 1), (512, 4), (2048, 16))
SWA_BLOCK = 128
H_M = 4
VMEM_LIMIT = 48 * 1024 * 1024
HI = lax.Precision.HIGHEST


def _cparams(n_axes):
    return pltpu.CompilerParams(dimension_semantics=("arbitrary",) * n_axes,
                                vmem_limit_bytes=VMEM_LIMIT)


def _silu(x):
    return x * jax.nn.sigmoid(x)


def _dot(a, b, precision=None):
    return jnp.dot(a, b, preferred_element_type=F32, precision=precision)


def _dot_nt(a, b, precision=None):
    return lax.dot_general(a, b, (((1,), (1,)), ((), ())), preferred_element_type=F32,
                           precision=precision)


def _dot_tn(a, b, precision=None):
    return lax.dot_general(a, b, (((0,), (0,)), ((), ())), preferred_element_type=F32,
                           precision=precision)


def _rms_cast_kernel(x_ref, g_ref, o_ref):
    x = x_ref[...]
    ms = jnp.mean(x * x, axis=-1, keepdims=True)
    o_ref[...] = (x * lax.rsqrt(ms + EPS) * g_ref[...]).astype(o_ref.dtype)


def rms_cast(x2d, g, tm):
    m, d = x2d.shape
    return pl.pallas_call(
        _rms_cast_kernel,
        grid=(m // tm,),
        in_specs=[pl.BlockSpec((tm, d), lambda i: (i, 0)),
                  pl.BlockSpec((1, d), lambda i: (0, 0))],
        out_specs=pl.BlockSpec((tm, d), lambda i: (i, 0)),
        out_shape=jax.ShapeDtypeStruct((m, d), BF16),
        compiler_params=_cparams(1),
        name="rms_cast",
    )(x2d, g.reshape(1, d))


def _mm_kernel(*refs, n_in, has_res):
    o_ref = refs[-1]
    acc = _dot(refs[0][...], refs[n_in][...])
    for i in range(1, n_in):
        acc = acc + _dot(refs[i][...], refs[n_in + i][...])
    if has_res:
        acc = acc + refs[2 * n_in][...]
    o_ref[...] = acc.astype(o_ref.dtype)


def matmul(a_list, w, layer, n_off, n_size, tm, tn, res=None):
    m = a_list[0].shape[0]
    n_in = len(a_list)
    in_specs, w_specs = [], []
    row = 0
    for a in a_list:
        k = a.shape[1]
        assert row % k == 0 and n_off % tn == 0 and n_size % tn == 0 and m % tm == 0
        in_specs.append(pl.BlockSpec((tm, k), lambda n, i: (i, 0)))
        w_specs.append(pl.BlockSpec((None, k, tn), functools.partial(
            lambda n, i, rb, nb: (layer, rb, n + nb), rb=row // k, nb=n_off // tn)))
        row += k
    args = list(a_list) + [w] * n_in
    specs = in_specs + w_specs
    if res is not None:
        specs.append(pl.BlockSpec((tm, tn), lambda n, i: (i, n)))
        args.append(res)
    return pl.pallas_call(
        functools.partial(_mm_kernel, n_in=n_in, has_res=res is not None),
        grid=(n_size // tn, m // tm),
        in_specs=specs,
        out_specs=pl.BlockSpec((tm, tn), lambda n, i: (i, n)),
        out_shape=jax.ShapeDtypeStruct((m, n_size), F32),
        compiler_params=_cparams(2),
        name="matmul",
    )(*args)


def _headnorm_kernel(x_ref, g_ref, o_ref, *, heads):
    g = g_ref[...]
    for h in range(heads):
        cols = slice(h * HEAD, (h + 1) * HEAD)
        x = x_ref[:, cols]
        ms = jnp.mean(x * x, axis=-1, keepdims=True)
        o_ref[:, cols] = x * lax.rsqrt(ms + EPS) * g


def headnorm(x2d, col_off, width, gain, tm, cw):
    m = x2d.shape[0]
    assert col_off % cw == 0 and width % cw == 0 and m % tm == 0
    return pl.pallas_call(
        functools.partial(_headnorm_kernel, heads=cw // HEAD),
        grid=(m // tm, width // cw),
        in_specs=[pl.BlockSpec((tm, cw), lambda i, c: (i, c + col_off // cw)),
                  pl.BlockSpec((1, HEAD), lambda i, c: (0, 0))],
        out_specs=pl.BlockSpec((tm, cw), lambda i, c: (i, c)),
        out_shape=jax.ShapeDtypeStruct((m, width), F32),
        compiler_params=_cparams(2),
        name="headnorm",
    )(x2d, gain.reshape(1, HEAD))


def _odd_qkv_kernel(q_ref, k_ref, v_ref, qg_ref, kg_ref, k_prev_hbm, v_prev_hbm, qo_ref, ko_ref, vo_ref, *, heads):
    del k_prev_hbm, v_prev_hbm
    qg = qg_ref[...]
    kg = kg_ref[...]
    for h in range(heads):
        cols = slice(h * HEAD, (h + 1) * HEAD)
        q = q_ref[:, cols]
        qo_ref[:, cols] = q * lax.rsqrt(jnp.mean(q * q, axis=-1, keepdims=True) + EPS) * qg
        k = k_ref[:, cols]
        ko_ref[:, cols] = k * lax.rsqrt(jnp.mean(k * k, axis=-1, keepdims=True) + EPS) * kg
    vo_ref[...] = v_ref[...]


def odd_qkv(proj, w_c, q_gain, k_gain, layer, n_layers, k_prev, v_prev, tm, cw):
    m = proj.shape[0]
    assert w_c % cw == 0 and m % tm == 0 and k_prev.shape == v_prev.shape == (n_layers, m, w_c)
    nb = w_c // cw
    blk = (tm, cw)
    any_spec = pl.BlockSpec(memory_space=pl.ANY)
    stack = jax.ShapeDtypeStruct((n_layers, m, w_c), F32)
    return pl.pallas_call(
        functools.partial(_odd_qkv_kernel, heads=cw // HEAD),
        grid=(m // tm, nb),
        in_specs=[pl.BlockSpec(blk, lambda i, c: (i, c)),
                  pl.BlockSpec(blk, lambda i, c: (i, c + nb)),
                  pl.BlockSpec(blk, lambda i, c: (i, c + 2 * nb)),
                  pl.BlockSpec((1, HEAD), lambda i, c: (0, 0)),
                  pl.BlockSpec((1, HEAD), lambda i, c: (0, 0)),
                  any_spec, any_spec],
        out_specs=[pl.BlockSpec(blk, lambda i, c: (i, c)),
                   pl.BlockSpec((None,) + blk, lambda i, c: (layer, i, c)),
                   pl.BlockSpec((None,) + blk, lambda i, c: (layer, i, c))],
        out_shape=[jax.ShapeDtypeStruct((m, w_c), F32), stack, stack],
        input_output_aliases={5: 1, 6: 2},
        compiler_params=_cparams(2),
        name="odd_qkv",
    )(proj, proj, proj, q_gain.reshape(1, HEAD), k_gain.reshape(1, HEAD), k_prev, v_prev)


def _gate_kernel(x_ref, alog_ref, dt_ref, beta_ref, gam_ref, gsm_ref, *, n_heads, cs):
    x = x_ref[...]
    tm = x.shape[0]
    beta = jax.nn.sigmoid(x)
    g = -jnp.exp(alog_ref[...]) * jax.nn.softplus(x + dt_ref[...])
    ri = lax.broadcasted_iota(jnp.int32, (tm, tm), 0)
    ci = lax.broadcasted_iota(jnp.int32, (tm, tm), 1)
    tri = jnp.where((ci <= ri) & (ci >= (ri // cs) * cs), 1.0, 0.0).astype(F32)
    gam = _dot(tri, g, HI)
    gsm_ref[...] = gam
    wide = n_heads * HEAD
    src = lax.broadcasted_iota(jnp.int32, (HEAD, wide), 0)
    dst = lax.broadcasted_iota(jnp.int32, (HEAD, wide), 1) // HEAD
    beta_ref[...] = _dot(beta, jnp.where(src == dst, 1.0, 0.0).astype(F32), HI)
    gam_ref[...] = _dot(gam, jnp.where(src == dst + n_heads, 1.0, 0.0).astype(F32), HI)


def gdn_gates(small2d, a_log, dt_bias, tm, cs):
    m = small2d.shape[0]
    n_heads = a_log.shape[0]
    assert tm % cs == 0 and m % tm == 0
    pad = HEAD - 2 * n_heads
    alog_row = jnp.concatenate([jnp.zeros((n_heads,), F32), a_log, jnp.zeros((pad,), F32)]).reshape(1, HEAD)
    dt_row = jnp.concatenate([jnp.zeros((n_heads,), F32), dt_bias, jnp.zeros((pad,), F32)]).reshape(1, HEAD)
    wide = n_heads * HEAD
    return pl.pallas_call(
        functools.partial(_gate_kernel, n_heads=n_heads, cs=cs),
        grid=(m // tm,),
        in_specs=[pl.BlockSpec((tm, HEAD), lambda i: (i, 0)),
                  pl.BlockSpec((1, HEAD), lambda i: (0, 0)),
                  pl.BlockSpec((1, HEAD), lambda i: (0, 0))],
        out_specs=[pl.BlockSpec((tm, wide), lambda i: (i, 0)),
                   pl.BlockSpec((tm, wide), lambda i: (i, 0)),
                   pl.BlockSpec((tm, HEAD), lambda i: (i, 0))],
        out_shape=[jax.ShapeDtypeStruct((m, wide), F32), jax.ShapeDtypeStruct((m, wide), F32),
                   jax.ShapeDtypeStruct((m, HEAD), F32)],
        compiler_params=_cparams(1),
        name="gdn_gates",
    )(small2d, alog_row, dt_row)


def _conv_qkv_kernel(x_ref, halo_ref, st_ref, w_ref, o_ref, buf_ref, *, tt, n_q, n_qk, heads):
    c = pl.program_id(1)
    t = pl.program_id(2)
    buf_ref[0:SUBLANES, :] = jnp.where(t == 0, st_ref[0], halo_ref[0])
    buf_ref[SUBLANES:SUBLANES + tt, :] = x_ref[0]
    w = w_ref[...]
    first = SUBLANES - (CONV_A - 1)
    y = buf_ref[first:first + tt, :] * w[0:1, :]
    for j in range(1, CONV_A):
        y = y + buf_ref[first + j:first + j + tt, :] * w[j:j + 1, :]
    y = _silu(y)

    @pl.when(c < n_qk)
    def _():
        scale = jnp.where(c < n_q, HEAD ** -0.5, 1.0).astype(F32)
        for h in range(heads):
            cols = slice(h * HEAD, (h + 1) * HEAD)
            yh = y[:, cols]
            ss = jnp.sum(yh * yh, axis=-1, keepdims=True)
            o_ref[0, :, cols] = yh * lax.rsqrt(ss + EPS) * scale

    @pl.when(c >= n_qk)
    def _():
        o_ref[0] = y


def conv_qkv(proj3d, state_pad, conv_w, width, tt, cw):
    bn, t, _ = proj3d.shape
    third = width // 3
    assert third % cw == 0 and t % tt == 0 and tt % SUBLANES == 0
    rb = tt // SUBLANES
    return pl.pallas_call(
        functools.partial(_conv_qkv_kernel, tt=tt, n_q=third // cw, n_qk=2 * third // cw, heads=cw // HEAD),
        grid=(bn, width // cw, t // tt),
        in_specs=[pl.BlockSpec((1, tt, cw), lambda b, c, i: (b, i, c)),
                  pl.BlockSpec((1, SUBLANES, cw), lambda b, c, i: (b, jnp.maximum(i * rb - 1, 0), c)),
                  pl.BlockSpec((1, SUBLANES, cw), lambda b, c, i: (b, 0, c)),
                  pl.BlockSpec((CONV_A, cw), lambda b, c, i: (0, c))],
        out_specs=pl.BlockSpec((1, tt, cw), lambda b, c, i: (b, i, c)),
        out_shape=jax.ShapeDtypeStruct((bn, t, width), F32),
        scratch_shapes=[pltpu.VMEM((tt + SUBLANES, cw), F32)],
        compiler_params=_cparams(3),
        name="conv_qkv",
    )(proj3d, proj3d, state_pad, conv_w)


def _sconv_kernel(bg_ref, cg_ref, xb_ref, zb_ref, hc_ref, hx_ref, st_ref, w_ref, o_ref, nb_ref, buf_ref, *, tt):
    t = pl.program_id(2)
    buf_ref[0:SUBLANES, :] = jnp.where(t == 0, st_ref[0], hc_ref[0] * hx_ref[0])
    buf_ref[SUBLANES:SUBLANES + tt, :] = cg_ref[0] * xb_ref[0]
    w = w_ref[...]
    first = SUBLANES - (CONV_B - 1)
    y = buf_ref[first:first + tt, :] * w[0:1, :]
    for j in range(1, CONV_B):
        y = y + buf_ref[first + j:first + j + tt, :] * w[j:j + 1, :]
    o_ref[0] = (bg_ref[0] * y * _silu(zb_ref[0])).astype(o_ref.dtype)
    nb_ref[0] = buf_ref[tt:tt + SUBLANES, :]


def short_conv(proj3d, off_b, off_c, off_x, off_z, width, state_pad, conv_w, tt, cw):
    bn, t, _ = proj3d.shape
    assert t % tt == 0 and width % cw == 0
    rb = tt // SUBLANES

    def col(off):
        assert off % cw == 0
        return functools.partial(lambda b, c, i, o: (b, i, c + o), o=off // cw)

    def halo(off):
        return functools.partial(lambda b, c, i, o: (b, jnp.maximum(i * rb - 1, 0), c + o), o=off // cw)

    blk = (1, tt, cw)
    hblk = (1, SUBLANES, cw)
    return pl.pallas_call(
        functools.partial(_sconv_kernel, tt=tt),
        grid=(bn, width // cw, t // tt),
        in_specs=[pl.BlockSpec(blk, col(off_b)), pl.BlockSpec(blk, col(off_c)),
                  pl.BlockSpec(blk, col(off_x)), pl.BlockSpec(blk, col(off_z)),
                  pl.BlockSpec(hblk, halo(off_c)), pl.BlockSpec(hblk, halo(off_x)),
                  pl.BlockSpec(hblk, lambda b, c, i: (b, 0, c)),
                  pl.BlockSpec((CONV_B, cw), lambda b, c, i: (0, c))],
        out_specs=[pl.BlockSpec(blk, lambda b, c, i: (b, i, c)),
                   pl.BlockSpec(hblk, lambda b, c, i: (b, 0, c))],
        out_shape=[jax.ShapeDtypeStruct((bn, t, width), BF16),
                   jax.ShapeDtypeStruct((bn, SUBLANES, width), F32)],
        scratch_shapes=[pltpu.VMEM((tt + SUBLANES, cw), F32)],
        compiler_params=_cparams(3),
        name="short_conv",
    )(proj3d, proj3d, proj3d, proj3d, proj3d, proj3d, state_pad, conv_w)


def _split_bf16(x):
    hi = x.astype(BF16)
    return hi, (x - hi.astype(F32)).astype(BF16)


def _dot_3pass(a, b):
    a_hi, a_lo = _split_bf16(a)
    b_hi, b_lo = _split_bf16(b)
    return _dot(a_hi, b_hi) + (_dot(a_hi, b_lo) + _dot(a_lo, b_hi))


def _gdn_kernel(q_ref, k_ref, v_ref, z_ref, beta_ref, gam_ref, grow_ref, s0_ref, gain_ref, o_ref, s_ref,
                *, hb, cs, nct):
    ti = pl.program_id(2)

    @pl.when(ti == 0)
    def _():
        s_ref[...] = s0_ref[...]

    ii = lax.broadcasted_iota(jnp.int32, (cs, cs), 0)
    jj = lax.broadcasted_iota(jnp.int32, (cs, cs), 1)
    causal = ii >= jj
    strict = ii > jj
    eye_f = (ii == jj).astype(F32)
    gain = gain_ref[...]
    n_double = int(math.log2(cs)) - 1
    chains = [(j, h) for j in range(nct) for h in range(hb)]

    def tile(ref, c):
        return ref[0, c[0] * cs:(c[0] + 1) * cs, c[1] * HEAD:(c[1] + 1) * HEAD]

    k = [tile(k_ref, c) for c in chains]
    beta = [tile(beta_ref, c) for c in chains]
    gam = [tile(gam_ref, c) for c in chains]
    kb = [a * b for a, b in zip(k, beta)]
    raw = [_dot_nt(jnp.concatenate([b, tile(q_ref, c)], axis=0).astype(BF16), a.astype(BF16))
           for a, b, c in zip(k, kb, chains)]
    decay = []
    for g, (j, h) in zip(gam, chains):
        gam_row = grow_ref[0, h, pl.ds(ti * nct + j, 1), :]
        decay.append(jnp.where(causal, jnp.exp(jnp.where(causal, g[:, :cs] - gam_row, 0.0)), 0.0))
    qk = [jnp.where(causal, r[cs:] * d, 0.0) for r, d in zip(raw, decay)]
    pw = [jnp.where(strict, -(r[:cs] * d), 0.0) for r, d in zip(raw, decay)]
    inv = [eye_f + p for p in pw]
    pw = [_dot_3pass(p, p) for p in pw]
    for _ in range(n_double - 1):
        prod = [_dot_3pass(jnp.concatenate([p, x], axis=0), p) for p, x in zip(pw, inv)]
        pw = [p[:cs] for p in prod]
        inv = [x + p[cs:] for x, p in zip(inv, prod)]
    inv = [x + _dot_3pass(x, p) for x, p in zip(inv, pw)]
    e_gam = [jnp.exp(g) for g in gam]
    uw = [_dot_3pass(x, jnp.concatenate([tile(v_ref, c) * b, a * e], axis=1))
          for x, c, b, a, e in zip(inv, chains, beta, kb, e_gam)]
    wq = [jnp.concatenate([x[:, HEAD:], tile(q_ref, c) * e], axis=0).astype(BF16)
          for x, c, e in zip(uw, chains, e_gam)]
    k_dec_t = [(a * jnp.exp(g[cs - 1:cs, :] - g)).T.astype(BF16) for a, g in zip(k, gam)]
    state = [s_ref[0, h] for h in range(hb)]
    for j in range(nct):
        idx = [j * hb + h for h in range(hb)]
        ws = [_dot(wq[i], state[h].astype(BF16)) for h, i in enumerate(idx)]
        v16 = [(uw[i][:, :HEAD] - x[:cs]).astype(BF16) for x, i in zip(ws, idx)]
        o = [x[cs:] + _dot(qk[i].astype(BF16), y) for x, y, i in zip(ws, v16, idx)]
        state = [s * jnp.exp(gam[i][cs - 1:cs, :]) + _dot(k_dec_t[i], y) for s, y, i in zip(state, v16, idx)]
        for h, i in enumerate(idx):
            ms = jnp.mean(o[h] * o[h], axis=-1, keepdims=True)
            on = o[h] * lax.rsqrt(ms + EPS) * gain
            o_ref[0, j * cs:(j + 1) * cs, h * HEAD:(h + 1) * HEAD] = (
                on * _silu(tile(z_ref, (j, h)))).astype(o_ref.dtype)
    for h in range(hb):
        s_ref[0, h] = state[h]


def gdn(qkv_act, proj3d, z_off, beta_rep, gam_rep, gam_row4, s0, gain, hb, tt):
    bn, t, w3 = qkv_act.shape
    n_heads = w3 // (3 * HEAD)
    nc, cs = gam_row4.shape[2], gam_row4.shape[3]
    cw = hb * HEAD
    ng = n_heads // hb
    assert z_off % cw == 0 and n_heads % hb == 0 and t % tt == 0 and tt % cs == 0
    blk = (1, tt, cw)
    sblk = (1, hb, HEAD, HEAD)
    return pl.pallas_call(
        functools.partial(_gdn_kernel, hb=hb, cs=cs, nct=tt // cs),
        grid=(bn, ng, t // tt),
        in_specs=[pl.BlockSpec(blk, lambda b, h, i: (b, i, h)),
                  pl.BlockSpec(blk, lambda b, h, i: (b, i, ng + h)),
                  pl.BlockSpec(blk, lambda b, h, i: (b, i, 2 * ng + h)),
                  pl.BlockSpec(blk, lambda b, h, i: (b, i, z_off // cw + h)),
                  pl.BlockSpec(blk, lambda b, h, i: (b, i, h)),
                  pl.BlockSpec(blk, lambda b, h, i: (b, i, h)),
                  pl.BlockSpec((1, hb, nc, cs), lambda b, h, i: (b, h, 0, 0)),
                  pl.BlockSpec(sblk, lambda b, h, i: (b, h, 0, 0)),
                  pl.BlockSpec((1, HEAD), lambda b, h, i: (0, 0))],
        out_specs=[pl.BlockSpec(blk, lambda b, h, i: (b, i, h)),
                   pl.BlockSpec(sblk, lambda b, h, i: (b, h, 0, 0))],
        out_shape=[jax.ShapeDtypeStruct((bn, t, n_heads * HEAD), BF16),
                   jax.ShapeDtypeStruct((bn, n_heads, HEAD, HEAD), F32)],
        compiler_params=_cparams(3),
        name="gdn",
    )(qkv_act, qkv_act, qkv_act, proj3d, beta_rep, gam_rep, gam_row4, s0, gain.reshape(1, HEAD))


def _mem_attn_kernel(q_ref, z_ref, mk_ref, mv_ref, gain_ref, o_ref):
    gain = gain_ref[...]
    for h in range(H_M):
        cols = slice(h * HEAD, (h + 1) * HEAD)
        q = q_ref[0, :, cols]
        ms = jnp.mean(q * q, axis=-1, keepdims=True)
        qn = (q * lax.rsqrt(ms + EPS) * gain).astype(BF16)
        sc = _dot_nt(qn, mk_ref[0, :, cols].astype(BF16)) * (HEAD ** -0.5)
        m = jnp.max(sc, axis=-1, keepdims=True)
        p = jnp.exp(sc - m)
        p = p / jnp.sum(p, axis=-1, keepdims=True)
        o = _dot(p.astype(BF16), mv_ref[0, :, cols].astype(BF16))
        o_ref[0, :, cols] = (o * _silu(z_ref[0, :, cols])).astype(o_ref.dtype)


def mem_attn(proj3d, q_off, z_off, mem_k, mem_v, q_gain, tt):
    bn, t, _ = proj3d.shape
    n_mem = mem_k.shape[1]
    wm = H_M * HEAD
    assert q_off % wm == 0 and z_off % wm == 0 and t % tt == 0
    return pl.pallas_call(
        _mem_attn_kernel,
        grid=(bn, t // tt),
        in_specs=[pl.BlockSpec((1, tt, wm), lambda b, i: (b, i, q_off // wm)),
                  pl.BlockSpec((1, tt, wm), lambda b, i: (b, i, z_off // wm)),
                  pl.BlockSpec((1, n_mem, wm), lambda b, i: (b, 0, 0)),
                  pl.BlockSpec((1, n_mem, wm), lambda b, i: (b, 0, 0)),
                  pl.BlockSpec((1, HEAD), lambda b, i: (0, 0))],
        out_specs=pl.BlockSpec((1, tt, wm), lambda b, i: (b, i, 0)),
        out_shape=jax.ShapeDtypeStruct((bn, t, wm), BF16),
        compiler_params=_cparams(2),
        name="mem_attn",
    )(proj3d, proj3d, mem_k.reshape(bn, n_mem, wm), mem_v.reshape(bn, n_mem, wm), q_gain.reshape(1, HEAD))


def _dil_prompt_kernel(q_ref, k_ref, v_ref, z_ref, o_ref, acc_ref, lse_ref, *, hg, t, group):
    qi = lax.broadcasted_iota(jnp.int32, (SWA_BLOCK, SWA_BLOCK), 0)
    ki = lax.broadcasted_iota(jnp.int32, (SWA_BLOCK, SWA_BLOCK), 1)
    n_pat = len(SWA_PATTERNS)
    for h in range(hg):
        cols = slice(h * HEAD, (h + 1) * HEAD)
        for p, (window, dil) in enumerate(SWA_PATTERNS):
            assert window // dil == SWA_BLOCK
            l_sub = t // dil
            nb = l_sub // SWA_BLOCK
            assert nb * SWA_BLOCK * dil == t

            def rows_of(r, c, dil=dil):
                return pl.ds(r + c * (SWA_BLOCK * dil), SWA_BLOCK, stride=dil)

            blocks = [(r, c) for r in range(dil) for c in range(nb)]
            for g0 in range(0, len(blocks), group):
                grp = blocks[g0:g0 + group]
                q = [(q_ref[0, rows_of(r, c), cols] * (HEAD ** -0.5)).astype(BF16) for r, c in grp]
                s_cur = [jnp.where(ki <= qi, _dot_nt(x, k_ref[0, rows_of(r, c), cols].astype(BF16)), -jnp.inf)
                         for x, (r, c) in zip(q, grp)]
                s_prev = [jnp.where(ki >= qi, _dot_nt(x, k_ref[0, rows_of(r, c - 1), cols].astype(BF16)), -jnp.inf)
                          if c > 0 else None for x, (r, c) in zip(q, grp)]
                m = [jnp.max(a, axis=-1, keepdims=True) if b is None else
                     jnp.maximum(jnp.max(a, axis=-1, keepdims=True), jnp.max(b, axis=-1, keepdims=True))
                     for a, b in zip(s_cur, s_prev)]
                e_cur = [jnp.exp(a - x) for a, x in zip(s_cur, m)]
                e_prev = [None if b is None else jnp.exp(b - x) for b, x in zip(s_prev, m)]
                den = [jnp.sum(a, axis=-1, keepdims=True) if b is None else
                       jnp.sum(a, axis=-1, keepdims=True) + jnp.sum(b, axis=-1, keepdims=True)
                       for a, b in zip(e_cur, e_prev)]
                o = [_dot(a.astype(BF16), v_ref[0, rows_of(r, c), cols].astype(BF16)) for a, (r, c) in zip(e_cur, grp)]
                o = [x if b is None else x + _dot(b.astype(BF16), v_ref[0, rows_of(r, c - 1), cols].astype(BF16))
                     for x, b, (r, c) in zip(o, e_prev, grp)]
                for x, d, mx, (r, c) in zip(o, den, m, grp):
                    acc_ref[p, rows_of(r, c), :] = x / d
                    lse_ref[p, rows_of(r, c), :] = jnp.broadcast_to(mx + jnp.log(d), (SWA_BLOCK, HEAD))
        lse = [lse_ref[p] for p in range(n_pat)]
        top = functools.reduce(jnp.maximum, lse)
        wgt = [jnp.exp(x - top) for x in lse]
        tot = functools.reduce(lambda a, b: a + b, wgt)
        mix = functools.reduce(lambda a, b: a + b, [(wgt[p] / tot) * acc_ref[p] for p in range(n_pat)])
        o_ref[0, :, cols] = (mix * _silu(z_ref[0, :, cols])).astype(o_ref.dtype)


def dil_prompt(qn, k_stack, v_stack, layer, proj3d, z_off, hg):
    bn, t, wc = qn.shape
    cw = hg * HEAD
    assert z_off % cw == 0 and wc % cw == 0
    blk = (1, t, cw)
    sblk = (None, 1, t, cw)
    return pl.pallas_call(
        functools.partial(_dil_prompt_kernel, hg=hg, t=t, group=8),
        grid=(bn, wc // cw),
        in_specs=[pl.BlockSpec(blk, lambda b, h: (b, 0, h)),
                  pl.BlockSpec(sblk, lambda b, h: (layer, b, 0, h)),
                  pl.BlockSpec(sblk, lambda b, h: (layer, b, 0, h)),
                  pl.BlockSpec(blk, lambda b, h: (b, 0, z_off // cw + h))],
        out_specs=pl.BlockSpec(blk, lambda b, h: (b, 0, h)),
        out_shape=jax.ShapeDtypeStruct((bn, t, wc), BF16),
        scratch_shapes=[pltpu.VMEM((len(SWA_PATTERNS), t, HEAD), F32),
                        pltpu.VMEM((len(SWA_PATTERNS), t, HEAD), F32)],
        compiler_params=_cparams(2),
        name="dil_prompt",
    )(qn, k_stack, v_stack, proj3d)


MASKED_MAX = -1e30


def _dil_sample_kernel(q_ref, kn_ref, vn_ref, z_ref, kc_ref, vc_ref, o_ref, m_ref, l_ref, acc_ref,
                       *, n_heads, t, wb, tw, group):
    j = pl.program_id(1)
    n_pat = len(SWA_PATTERNS)

    @pl.when(j == 0)
    def _():
        m_ref[...] = jnp.full(m_ref.shape, MASKED_MAX, F32)
        l_ref[...] = jnp.zeros(l_ref.shape, F32)
        acc_ref[...] = jnp.zeros(acc_ref.shape, F32)

    def head_rows(ref, h):
        return ref[0, pl.ds(h, tw, stride=n_heads), :].astype(BF16)

    def q16(h):
        return (q_ref[0, :, h * HEAD:(h + 1) * HEAD] * (HEAD ** -0.5)).astype(BF16)

    def update(hs, p, scores, ok, values):
        idx = [h * n_pat + p for h in hs]
        s = [jnp.where(ok, x, -jnp.inf) for x in scores]
        m_old = [m_ref[i] for i in idx]
        m_new = [jnp.maximum(a, jnp.max(x, axis=-1, keepdims=True)) for a, x in zip(m_old, s)]
        alpha = [jnp.exp(a - b) for a, b in zip(m_old, m_new)]
        e = [jnp.exp(x - b[:, :1]) for x, b in zip(s, m_new)]
        pv = [_dot(x.astype(BF16), v) for x, v in zip(e, values)]
        for i, a, x, y, b in zip(idx, alpha, e, pv, m_new):
            l_ref[i] = l_ref[i] * a + jnp.sum(x, axis=-1, keepdims=True)
            acc_ref[i] = acc_ref[i] * a + y
            m_ref[i] = b

    row = lax.broadcasted_iota(jnp.int32, (t, tw), 0)
    col = lax.broadcasted_iota(jnp.int32, (t, tw), 1)
    delta = wb + row - (j * tw + col)
    nearest = wb - (j + 1) * tw + 1
    for g0 in range(0, n_heads, group):
        hs = list(range(g0, g0 + group))
        scores = [_dot_nt(q16(h), head_rows(kc_ref, h)) for h in hs]
        for p, (window, dil) in enumerate(SWA_PATTERNS):
            def tile_update(p=p, window=window, dil=dil, hs=hs, scores=scores):
                ok = ((delta & (dil - 1)) == 0) & (delta <= window)
                update(hs, p, scores, ok, [head_rows(vc_ref, h) for h in hs])
            pl.when(nearest <= window)(tile_update)

    @pl.when(j == pl.num_programs(1) - 1)
    def _():
        dn = lax.broadcasted_iota(jnp.int32, (t, t), 0) - lax.broadcasted_iota(jnp.int32, (t, t), 1)
        for g0 in range(0, n_heads, group):
            hs = list(range(g0, g0 + group))
            cols = [slice(h * HEAD, (h + 1) * HEAD) for h in hs]
            scores = [_dot_nt(q16(h), kn_ref[0, :, c].astype(BF16)) for h, c in zip(hs, cols)]
            values = [vn_ref[0, :, c].astype(BF16) for c in cols]
            for p, (window, dil) in enumerate(SWA_PATTERNS):
                ok = (dn >= 0) & ((dn & (dil - 1)) == 0) & (dn <= window)
                update(hs, p, scores, ok, values)
            for h, c in zip(hs, cols):
                idx = [h * n_pat + p for p in range(n_pat)]
                lse = [m_ref[i] + jnp.log(l_ref[i]) for i in idx]
                top = functools.reduce(jnp.maximum, lse)
                wgt = [jnp.exp(x - top) for x in lse]
                tot = functools.reduce(lambda a, b: a + b, wgt)
                mix = functools.reduce(lambda a, b: a + b,
                                       [(w / tot) * (acc_ref[i] / l_ref[i]) for w, i in zip(wgt, idx)])
                o_ref[0, :, c] = (mix * _silu(z_ref[0, :, c])).astype(o_ref.dtype)


def dil_sample(qn, k_stack, v_stack, layer, proj3d, z_off, cache_k, cache_v, n_heads, tw):
    bn, t, wc = qn.shape
    wb = cache_k.shape[2] // n_heads
    assert z_off % wc == 0 and wb % tw == 0 and wc == n_heads * HEAD
    for window, dil in SWA_PATTERNS:
        assert dil & (dil - 1) == 0
    blk = (1, t, wc)
    sblk = (None, 1, t, wc)
    cblk = (None, 1, tw * n_heads, HEAD)
    state = pltpu.VMEM((n_heads * len(SWA_PATTERNS), t, HEAD), F32)
    return pl.pallas_call(
        functools.partial(_dil_sample_kernel, n_heads=n_heads, t=t, wb=wb, tw=tw, group=8),
        grid=(bn, wb // tw),
        in_specs=[pl.BlockSpec(blk, lambda b, j: (b, 0, 0)),
                  pl.BlockSpec(sblk, lambda b, j: (layer, b, 0, 0)),
                  pl.BlockSpec(sblk, lambda b, j: (layer, b, 0, 0)),
                  pl.BlockSpec(blk, lambda b, j: (b, 0, z_off // wc)),
                  pl.BlockSpec(cblk, lambda b, j: (layer, b, j, 0)),
                  pl.BlockSpec(cblk, lambda b, j: (layer, b, j, 0))],
        out_specs=pl.BlockSpec(blk, lambda b, j: (b, 0, 0)),
        out_shape=jax.ShapeDtypeStruct((bn, t, wc), BF16),
        scratch_shapes=[state, state, state],
        compiler_params=_cparams(2),
        name="dil_sample",
    )(qn, k_stack, v_stack, proj3d, cache_k, cache_v)


def _cache_roll_kernel(ck_hbm, cv_hbm, kn_ref, vn_ref, ok_hbm, ov_hbm, stage_k, stage_v, sems, *, n_heads, t, wb):
    l = pl.program_id(0)
    b = pl.program_id(1)
    keep = (wb - t) * n_heads
    new = t * n_heads
    copies = [
        pltpu.make_async_copy(ck_hbm.at[l, b, pl.ds(new, keep)], ok_hbm.at[l, b, pl.ds(0, keep)], sems.at[0]),
        pltpu.make_async_copy(cv_hbm.at[l, b, pl.ds(new, keep)], ov_hbm.at[l, b, pl.ds(0, keep)], sems.at[1]),
        pltpu.make_async_copy(stage_k, ok_hbm.at[l, b, pl.ds(keep, new)], sems.at[2]),
        pltpu.make_async_copy(stage_v, ov_hbm.at[l, b, pl.ds(keep, new)], sems.at[3]),
    ]
    copies[0].start()
    copies[1].start()
    for h in range(n_heads):
        stage_k[pl.ds(h, t, stride=n_heads), :] = kn_ref[:, h * HEAD:(h + 1) * HEAD]
        stage_v[pl.ds(h, t, stride=n_heads), :] = vn_ref[:, h * HEAD:(h + 1) * HEAD]
    copies[2].start()
    copies[3].start()
    for c in copies:
        c.wait()


def cache_roll(cache_k, cache_v, k_new, v_new, n_heads):
    n_l, bn, rows, _ = cache_k.shape
    t = k_new.shape[2]
    wb = rows // n_heads
    nblk = (None, None, t, n_heads * HEAD)
    any_spec = pl.BlockSpec(memory_space=pl.ANY)
    out = jax.ShapeDtypeStruct(cache_k.shape, cache_k.dtype)
    return pl.pallas_call(
        functools.partial(_cache_roll_kernel, n_heads=n_heads, t=t, wb=wb),
        grid=(n_l, bn),
        in_specs=[any_spec, any_spec,
                  pl.BlockSpec(nblk, lambda l, b: (l, b, 0, 0)),
                  pl.BlockSpec(nblk, lambda l, b: (l, b, 0, 0))],
        out_specs=[any_spec, any_spec],
        out_shape=[out, out],
        scratch_shapes=[pltpu.VMEM((t * n_heads, HEAD), F32), pltpu.VMEM((t * n_heads, HEAD), F32),
                        pltpu.SemaphoreType.DMA((4,))],
        compiler_params=_cparams(2),
        name="cache_roll",
    )(cache_k, cache_v, k_new, v_new)


def _row_tile(m, pref):
    return pref if m % pref == 0 else m


def _in_proj(x2d, g_norm, w_stack, layer, n_main):
    m = x2d.shape[0]
    h = rms_cast(x2d, g_norm, _row_tile(m, 256))
    return h, matmul([h], w_stack, layer, 0, n_main, _row_tile(m, 512), 1024)


def _pad_state(buf):
    return jnp.pad(buf, ((0, 0), (SUBLANES - buf.shape[1], 0), (0, 0)))


def even_layer(x, conv_a_buf, s0, conv_b_buf, mem_k, mem_v, layer, g_norm, w_main, w_small, conv_a_w, a_log,
               dt_bias, gdn_gain, conv_b_w, w_out, q_gain_m):
    bn, t, d = x.shape
    m = bn * t
    n_heads = a_log.shape[0]
    w_a = n_heads * HEAD
    w_b = conv_b_w.shape[1]
    w_m = H_M * HEAD
    off_za = 3 * w_a
    off_b = off_za + w_a
    off_c, off_x, off_zb = off_b + w_b, off_b + 2 * w_b, off_b + 3 * w_b
    off_qm = off_b + 4 * w_b
    off_zm = off_qm + w_m
    n_main = off_zm + w_m
    x2d = x.reshape(m, d)
    h, proj = _in_proj(x2d, g_norm, w_main, layer, n_main)
    small = matmul([h], w_small, layer, 0, HEAD, _row_tile(m, 512), HEAD)
    proj3d = proj.reshape(bn, t, n_main)
    tt = _row_tile(t, 512)
    qkv_act = conv_qkv(proj3d, _pad_state(conv_a_buf), conv_a_w, 3 * w_a, tt, 512)
    cs = min(GDN_CHUNK, t)
    nc = t // cs
    beta_rep, gam_rep, gam_small = gdn_gates(small, a_log, dt_bias, _row_tile(m, 512), cs)
    gam_row4 = (gam_small.reshape(bn, t, HEAD)[:, :, n_heads:2 * n_heads]
                .transpose(0, 2, 1).reshape(bn, n_heads, nc, cs))
    o_a, new_s = gdn(qkv_act, proj3d, off_za, beta_rep.reshape(bn, t, w_a), gam_rep.reshape(bn, t, w_a),
                     gam_row4, s0, gdn_gain, 4, _row_tile(t, 2 * GDN_CHUNK))
    new_conv_a = proj3d[:, t - (CONV_A - 1):, :3 * w_a]
    o_b, nb8 = short_conv(proj3d, off_b, off_c, off_x, off_zb, w_b, _pad_state(conv_b_buf), conv_b_w, tt, 512)
    new_conv_b = nb8[:, SUBLANES - (CONV_B - 1):, :]
    o_m = mem_attn(proj3d, off_qm, off_zm, mem_k, mem_v, q_gain_m, tt)
    y = matmul([o_a.reshape(m, w_a), o_b.reshape(m, w_b), o_m.reshape(m, w_m)], w_out, layer, 0, d,
               _row_tile(m, 512), 1024, res=x2d)
    return y.reshape(bn, t, d), new_conv_a, new_s, new_conv_b


def odd_layer(x, cache_k, cache_v, mem_k, mem_v, layer, n_layers, k_prev, v_prev, g_norm, w_main, q_gain_c,
              k_gain_c, w_out, q_gain_m):
    bn, t, d = x.shape
    m = bn * t
    w_c = d
    w_m = H_M * HEAD
    off_z = 3 * w_c
    off_qm = 4 * w_c
    off_zm = off_qm + w_m
    n_main = off_zm + w_m
    x2d = x.reshape(m, d)
    _, proj = _in_proj(x2d, g_norm, w_main, layer, n_main)
    proj3d = proj.reshape(bn, t, n_main)
    tm = _row_tile(m, 512)
    qn, k_stack, v_stack = odd_qkv(proj, w_c, q_gain_c, k_gain_c, layer, n_layers, k_prev, v_prev, tm, 512)
    qn = qn.reshape(bn, t, w_c)
    k4 = k_stack.reshape(n_layers, bn, t, w_c)
    v4 = v_stack.reshape(n_layers, bn, t, w_c)
    if cache_k is None:
        o_c = dil_prompt(qn, k4, v4, layer, proj3d, off_z, 1)
    else:
        o_c = dil_sample(qn, k4, v4, layer, proj3d, off_z, cache_k, cache_v, w_c // HEAD, 256)
    o_m = mem_attn(proj3d, off_qm, off_zm, mem_k, mem_v, q_gain_m, _row_tile(t, 512))
    y = matmul([o_c.reshape(m, w_c), o_m.reshape(m, w_m)], w_out, layer, 0, d, tm, 1024, res=x2d)
    return y.reshape(bn, t, d), k_stack, v_stack


def mem_project(mem, g, w_kv_stack, layer, k_gain):
    bn, n_mem, d = mem.shape
    m = bn * n_mem
    w_m = H_M * HEAD
    hm = rms_cast(mem.reshape(m, d), g, _row_tile(m, 256))
    kv = matmul([hm], w_kv_stack, layer, 0, 2 * w_m, _row_tile(m, 512), w_m)
    k = headnorm(kv, 0, w_m, k_gain, _row_tile(m, 512), w_m)
    return k.reshape(bn, n_mem, H_M, HEAD), kv[:, w_m:].reshape(bn, n_mem, H_M, HEAD)


def _prep_even_weights(w_in, n_heads):
    w_a = n_heads * HEAD
    cut = 4 * w_a
    w_main = jnp.concatenate([w_in[:, :, :cut], w_in[:, :, cut + 2 * n_heads:]], axis=2).astype(BF16)
    w_small = jnp.pad(w_in[:, :, cut:cut + 2 * n_heads], ((0, 0), (0, 0), (0, HEAD - 2 * n_heads))).astype(BF16)
    return w_main, w_small


def kernel(x_prompt, x_sample, state_gdn, state_gdn_conv, state_sconv, cache_swa_k, cache_swa_v, cache_mem_k, cache_mem_v, mem_prompt, norm_even, w_in_even, conv_a, a_log, dt_bias, gdn_norm, conv_b, w_out_even, norm_odd, w_in_odd, q_norm_c, k_norm_c, w_out_odd, mem_norm, w_mem_kv, q_norm_m, k_norm_m):
    bp, t_p, d = x_prompt.shape
    bs, t_s, _ = x_sample.shape
    depth = mem_norm.shape[0]
    n_odd = norm_odd.shape[0]
    n_heads_a = a_log.shape[1]
    n_heads_c = cache_swa_k.shape[3]
    wb = cache_swa_k.shape[2]
    w_even_main, w_even_small = _prep_even_weights(w_in_even, n_heads_a)
    w_odd_main = w_in_odd.astype(BF16)
    w_out_e = w_out_even.astype(BF16)
    w_out_o = w_out_odd.astype(BF16)
    w_kv = w_mem_kv.astype(BF16)
    cache_k = cache_swa_k.reshape(n_odd, bs, wb * n_heads_c, HEAD)
    cache_v = cache_swa_v.reshape(n_odd, bs, wb * n_heads_c, HEAD)
    y_p, y_s = x_prompt, x_sample
    gdn_p, gdn_s, gconv_p, gconv_s, sconv_p, sconv_s = [], [], [], [], [], []
    memk_p, memv_p = [], []
    kp, vp = (jnp.zeros((n_odd, bp * t_p, d), F32) for _ in range(2))
    ks, vs = (jnp.zeros((n_odd, bs * t_s, d), F32) for _ in range(2))
    for i in range(depth):
        mk_p, mv_p = mem_project(mem_prompt, mem_norm[i], w_kv, i, k_norm_m[i])
        memk_p.append(mk_p)
        memv_p.append(mv_p)
        if i % 2 == 0:
            e = i // 2
            wts = (e, norm_even[e], w_even_main, w_even_small, conv_a[e], a_log[e], dt_bias[e], gdn_norm[e],
                   conv_b[e], w_out_e, q_norm_m[i])
            y_p, ca, sa, cb = even_layer(
                y_p,
                jnp.zeros((bp,) + state_gdn_conv.shape[2:], state_gdn_conv.dtype),
                jnp.zeros((bp,) + state_gdn.shape[2:], state_gdn.dtype),
                jnp.zeros((bp,) + state_sconv.shape[2:], state_sconv.dtype),
                mk_p, mv_p, *wts)
            gconv_p.append(ca)
            gdn_p.append(sa)
            sconv_p.append(cb)
            y_s, ca, sa, cb = even_layer(y_s, state_gdn_conv[e], state_gdn[e], state_sconv[e],
                                         cache_mem_k[i], cache_mem_v[i], *wts)
            gconv_s.append(ca)
            gdn_s.append(sa)
            sconv_s.append(cb)
        else:
            o = i // 2
            wts = (norm_odd[o], w_odd_main, q_norm_c[o], k_norm_c[o], w_out_o, q_norm_m[i])
            y_p, kp, vp = odd_layer(y_p, None, None, mk_p, mv_p, o, n_odd, kp, vp, *wts)
            y_s, ks, vs = odd_layer(y_s, cache_k, cache_v, cache_mem_k[i], cache_mem_v[i], o, n_odd, ks, vs, *wts)
    n_keep = min(max(w for w, _ in SWA_PATTERNS), t_p)
    swa_k_p = kp.reshape(n_odd, bp, t_p, n_heads_c, HEAD)[:, :, t_p - n_keep:]
    swa_v_p = vp.reshape(n_odd, bp, t_p, n_heads_c, HEAD)[:, :, t_p - n_keep:]
    swa_k_s, swa_v_s = cache_roll(cache_k, cache_v, ks.reshape(n_odd, bs, t_s, d), vs.reshape(n_odd, bs, t_s, d),
                                  n_heads_c)
    swa_k_s = swa_k_s.reshape(n_odd, bs, wb, n_heads_c, HEAD)
    swa_v_s = swa_v_s.reshape(n_odd, bs, wb, n_heads_c, HEAD)
    return (y_p, y_s, jnp.stack(gdn_p), jnp.stack(gdn_s), jnp.stack(gconv_p), jnp.stack(gconv_s),
            jnp.stack(sconv_p), jnp.stack(sconv_s), swa_k_p, swa_k_s, swa_v_p, swa_v_s,
            jnp.stack(memk_p), jnp.stack(memv_p))
```

```python
import functools
import math

import jax
import jax.numpy as jnp
from jax import lax
from jax.experimental import pallas as pl
from jax.experimental.pallas import tpu as pltpu

F32 = jnp.float32
BF16 = jnp.bfloat16
EPS = 1e-6
HEAD = 128
SUBLANES = 8
CONV_A = 4
CONV_B = 3
GDN_CHUNK = 64
SWA_PATTERNS = ((128, 1), (512, 4), (2048, 16))
SWA_BLOCK = 128
H_M = 4
VMEM_LIMIT = 48 * 1024 * 1024
HI = lax.Precision.HIGHEST


def _cparams(n_axes):
    return pltpu.CompilerParams(dimension_semantics=("arbitrary",) * n_axes,
                                vmem_limit_bytes=VMEM_LIMIT)


def _silu(x):
    return x * jax.nn.sigmoid(x)


def _dot(a, b, precision=None):
    return jnp.dot(a, b, preferred_element_type=F32, precision=precision)


def _dot_nt(a, b, precision=None):
    return lax.dot_general(a, b, (((1,), (1,)), ((), ())), preferred_element_type=F32,
                           precision=precision)


def _dot_tn(a, b, precision=None):
    return lax.dot_general(a, b, (((0,), (0,)), ((), ())), preferred_element_type=F32,
                           precision=precision)


def _rms_cast_kernel(x_ref, g_ref, o_ref):
    x = x_ref[...]
    ms = jnp.mean(x * x, axis=-1, keepdims=True)
    o_ref[...] = (x * lax.rsqrt(ms + EPS) * g_ref[...]).astype(o_ref.dtype)


def rms_cast(x2d, g, tm):
    m, d = x2d.shape
    return pl.pallas_call(
        _rms_cast_kernel,
        grid=(m // tm,),
        in_specs=[pl.BlockSpec((tm, d), lambda i: (i, 0)),
                  pl.BlockSpec((1, d), lambda i: (0, 0))],
        out_specs=pl.BlockSpec((tm, d), lambda i: (i, 0)),
        out_shape=jax.ShapeDtypeStruct((m, d), BF16),
        compiler_params=_cparams(1),
        name="rms_cast",
    )(x2d, g.reshape(1, d))


def _mm_kernel(*refs, n_in, has_res):
    o_ref = refs[-1]
    acc = _dot(refs[0][...], refs[n_in][...])
    for i in range(1, n_in):
        acc = acc + _dot(refs[i][...], refs[n_in + i][...])
    if has_res:
        acc = acc + refs[2 * n_in][...]
    o_ref[...] = acc.astype(o_ref.dtype)


def matmul(a_list, w, layer, n_off, n_size, tm, tn, res=None):
    m = a_list[0].shape[0]
    n_in = len(a_list)
    in_specs, w_specs = [], []
    row = 0
    for a in a_list:
        k = a.shape[1]
        assert row % k == 0 and n_off % tn == 0 and n_size % tn == 0 and m % tm == 0
        in_specs.append(pl.BlockSpec((tm, k), lambda n, i: (i, 0)))
        w_specs.append(pl.BlockSpec((None, k, tn), functools.partial(
            lambda n, i, rb, nb: (layer, rb, n + nb), rb=row // k, nb=n_off // tn)))
        row += k
    args = list(a_list) + [w] * n_in
    specs = in_specs + w_specs
    if res is not None:
        specs.append(pl.BlockSpec((tm, tn), lambda n, i: (i, n)))
        args.append(res)
    return pl.pallas_call(
        functools.partial(_mm_kernel, n_in=n_in, has_res=res is not None),
        grid=(n_size // tn, m // tm),
        in_specs=specs,
        out_specs=pl.BlockSpec((tm, tn), lambda n, i: (i, n)),
        out_shape=jax.ShapeDtypeStruct((m, n_size), F32),
        compiler_params=_cparams(2),
        name="matmul",
    )(*args)


def _headnorm_kernel(x_ref, g_ref, o_ref, *, heads):
    g = g_ref[...]
    for h in range(heads):
        cols = slice(h * HEAD, (h + 1) * HEAD)
        x = x_ref[:, cols]
        ms = jnp.mean(x * x, axis=-1, keepdims=True)
        o_ref[:, cols] = x * lax.rsqrt(ms + EPS) * g


def headnorm(x2d, col_off, width, gain, tm, cw):
    m = x2d.shape[0]
    assert col_off % cw == 0 and width % cw == 0 and m % tm == 0
    return pl.pallas_call(
        functools.partial(_headnorm_kernel, heads=cw // HEAD),
        grid=(m // tm, width // cw),
        in_specs=[pl.BlockSpec((tm, cw), lambda i, c: (i, c + col_off // cw)),
                  pl.BlockSpec((1, HEAD), lambda i, c: (0, 0))],
        out_specs=pl.BlockSpec((tm, cw), lambda i, c: (i, c)),
        out_shape=jax.ShapeDtypeStruct((m, width), F32),
        compiler_params=_cparams(2),
        name="headnorm",
    )(x2d, gain.reshape(1, HEAD))


def _odd_qkv_kernel(q_ref, k_ref, v_ref, qg_ref, kg_ref, k_prev_hbm, v_prev_hbm, qo_ref, ko_ref, vo_ref, *, heads):
    del k_prev_hbm, v_prev_hbm
    qg = qg_ref[...]
    kg = kg_ref[...]
    for h in range(heads):
        cols = slice(h * HEAD, (h + 1) * HEAD)
        q = q_ref[:, cols]
        qo_ref[:, cols] = q * lax.rsqrt(jnp.mean(q * q, axis=-1, keepdims=True) + EPS) * qg
        k = k_ref[:, cols]
        ko_ref[:, cols] = k * lax.rsqrt(jnp.mean(k * k, axis=-1, keepdims=True) + EPS) * kg
    vo_ref[...] = v_ref[...]


def odd_qkv(proj, w_c, q_gain, k_gain, layer, n_layers, k_prev, v_prev, tm, cw):
    m = proj.shape[0]
    assert w_c % cw == 0 and m % tm == 0 and k_prev.shape == v_prev.shape == (n_layers, m, w_c)
    nb = w_c // cw
    blk = (tm, cw)
    any_spec = pl.BlockSpec(memory_space=pl.ANY)
    stack = jax.ShapeDtypeStruct((n_layers, m, w_c), F32)
    return pl.pallas_call(
        functools.partial(_odd_qkv_kernel, heads=cw // HEAD),
        grid=(m // tm, nb),
        in_specs=[pl.BlockSpec(blk, lambda i, c: (i, c)),
                  pl.BlockSpec(blk, lambda i, c: (i, c + nb)),
                  pl.BlockSpec(blk, lambda i, c: (i, c + 2 * nb)),
                  pl.BlockSpec((1, HEAD), lambda i, c: (0, 0)),
                  pl.BlockSpec((1, HEAD), lambda i, c: (0, 0)),
                  any_spec, any_spec],
        out_specs=[pl.BlockSpec(blk, lambda i, c: (i, c)),
                   pl.BlockSpec((None,) + blk, lambda i, c: (layer, i, c)),
                   pl.BlockSpec((None,) + blk, lambda i, c: (layer, i, c))],
        out_shape=[jax.ShapeDtypeStruct((m, w_c), F32), stack, stack],
        input_output_aliases={5: 1, 6: 2},
        compiler_params=_cparams(2),
        name="odd_qkv",
    )(proj, proj, proj, q_gain.reshape(1, HEAD), k_gain.reshape(1, HEAD), k_prev, v_prev)


def _gate_kernel(x_ref, alog_ref, dt_ref, beta_ref, gam_ref, gsm_ref, *, n_heads, cs):
    x = x_ref[...]
    tm = x.shape[0]
    beta = jax.nn.sigmoid(x)
    g = -jnp.exp(alog_ref[...]) * jax.nn.softplus(x + dt_ref[...])
    ri = lax.broadcasted_iota(jnp.int32, (tm, tm), 0)
    ci = lax.broadcasted_iota(jnp.int32, (tm, tm), 1)
    tri = jnp.where((ci <= ri) & (ci >= (ri // cs) * cs), 1.0, 0.0).astype(F32)
    gam = _dot(tri, g, HI)
    gsm_ref[...] = gam
    wide = n_heads * HEAD
    src = lax.broadcasted_iota(jnp.int32, (HEAD, wide), 0)
    dst = lax.broadcasted_iota(jnp.int32, (HEAD, wide), 1) // HEAD
    beta_ref[...] = _dot(beta, jnp.where(src == dst, 1.0, 0.0).astype(F32), HI)
    gam_ref[...] = _dot(gam, jnp.where(src == dst + n_heads, 1.0, 0.0).astype(F32), HI)


def gdn_gates(small2d, a_log, dt_bias, tm, cs):
    m = small2d.shape[0]
    n_heads = a_log.shape[0]
    assert tm % cs == 0 and m % tm == 0
    pad = HEAD - 2 * n_heads
    alog_row = jnp.concatenate([jnp.zeros((n_heads,), F32), a_log, jnp.zeros((pad,), F32)]).reshape(1, HEAD)
    dt_row = jnp.concatenate([jnp.zeros((n_heads,), F32), dt_bias, jnp.zeros((pad,), F32)]).reshape(1, HEAD)
    wide = n_heads * HEAD
    return pl.pallas_call(
        functools.partial(_gate_kernel, n_heads=n_heads, cs=cs),
        grid=(m // tm,),
        in_specs=[pl.BlockSpec((tm, HEAD), lambda i: (i, 0)),
                  pl.BlockSpec((1, HEAD), lambda i: (0, 0)),
                  pl.BlockSpec((1, HEAD), lambda i: (0, 0))],
        out_specs=[pl.BlockSpec((tm, wide), lambda i: (i, 0)),
                   pl.BlockSpec((tm, wide), lambda i: (i, 0)),
                   pl.BlockSpec((tm, HEAD), lambda i: (i, 0))],
        out_shape=[jax.ShapeDtypeStruct((m, wide), F32), jax.ShapeDtypeStruct((m, wide), F32),
                   jax.ShapeDtypeStruct((m, HEAD), F32)],
        compiler_params=_cparams(1),
        name="gdn_gates",
    )(small2d, alog_row, dt_row)


def _sconv_kernel(bg_ref, cg_ref, xb_ref, zb_ref, hc_ref, hx_ref, st_ref, w_ref, o_ref, nb_ref, buf_ref, *, tt):
    t = pl.program_id(2)
    buf_ref[0:SUBLANES, :] = jnp.where(t == 0, st_ref[0], hc_ref[0] * hx_ref[0])
    buf_ref[SUBLANES:SUBLANES + tt, :] = cg_ref[0] * xb_ref[0]
    w = w_ref[...]
    first = SUBLANES - (CONV_B - 1)
    y = buf_ref[first:first + tt, :] * w[0:1, :]
    for j in range(1, CONV_B):
        y = y + buf_ref[first + j:first + j + tt, :] * w[j:j + 1, :]
    o_ref[0] = (bg_ref[0] * y * _silu(zb_ref[0])).astype(o_ref.dtype)
    nb_ref[0] = buf_ref[tt:tt + SUBLANES, :]


def short_conv(proj3d, off_b, off_c, off_x, off_z, width, state_pad, conv_w, tt, cw):
    bn, t, _ = proj3d.shape
    assert t % tt == 0 and width % cw == 0
    rb = tt // SUBLANES

    def col(off):
        assert off % cw == 0
        return functools.partial(lambda b, c, i, o: (b, i, c + o), o=off // cw)

    def halo(off):
        return functools.partial(lambda b, c, i, o: (b, jnp.maximum(i * rb - 1, 0), c + o), o=off // cw)

    blk = (1, tt, cw)
    hblk = (1, SUBLANES, cw)
    return pl.pallas_call(
        functools.partial(_sconv_kernel, tt=tt),
        grid=(bn, width // cw, t // tt),
        in_specs=[pl.BlockSpec(blk, col(off_b)), pl.BlockSpec(blk, col(off_c)),
                  pl.BlockSpec(blk, col(off_x)), pl.BlockSpec(blk, col(off_z)),
                  pl.BlockSpec(hblk, halo(off_c)), pl.BlockSpec(hblk, halo(off_x)),
                  pl.BlockSpec(hblk, lambda b, c, i: (b, 0, c)),
                  pl.BlockSpec((CONV_B, cw), lambda b, c, i: (0, c))],
        out_specs=[pl.BlockSpec(blk, lambda b, c, i: (b, i, c)),
                   pl.BlockSpec(hblk, lambda b, c, i: (b, 0, c))],
        out_shape=[jax.ShapeDtypeStruct((bn, t, width), BF16),
                   jax.ShapeDtypeStruct((bn, SUBLANES, width), F32)],
        scratch_shapes=[pltpu.VMEM((tt + SUBLANES, cw), F32)],
        compiler_params=_cparams(3),
        name="short_conv",
    )(proj3d, proj3d, proj3d, proj3d, proj3d, proj3d, state_pad, conv_w)


def _split_bf16(x):
    hi = x.astype(BF16)
    return hi, (x - hi.astype(F32)).astype(BF16)


def _dot_3pass(a, b):
    a_hi, a_lo = _split_bf16(a)
    b_hi, b_lo = _split_bf16(b)
    return _dot(a_hi, b_hi) + (_dot(a_hi, b_lo) + _dot(a_lo, b_hi))


def _gdn_kernel(xq_ref, xk_ref, xv_ref, wq_ref, wk_ref, wv_ref, cq_ref, ck_ref, cv_ref, z_ref, beta_ref, gam_ref,
                grow_ref, s0_ref, gain_ref, o_ref, s_ref, xbuf, act, carry, *, hb, cs, nct):
    ti = pl.program_id(2)
    tt = nct * cs

    @pl.when(ti == 0)
    def _():
        s_ref[...] = s0_ref[...]
        for a, c_ref in enumerate((cq_ref, ck_ref, cv_ref)):
            carry[a] = c_ref[0]

    first = SUBLANES - (CONV_A - 1)
    for a, (x_ref, w_ref) in enumerate(((xq_ref, wq_ref), (xk_ref, wk_ref), (xv_ref, wv_ref))):
        xbuf[a, 0:SUBLANES, :] = carry[a]
        xbuf[a, SUBLANES:SUBLANES + tt, :] = x_ref[0]
        carry[a] = x_ref[0, tt - SUBLANES:tt, :]
        w = w_ref[...]
        y = xbuf[a, first:first + tt, :] * w[0:1, :]
        for j in range(1, CONV_A):
            y = y + xbuf[a, first + j:first + j + tt, :] * w[j:j + 1, :]
        y = _silu(y)
        if a == 2:
            act[a] = y
        else:
            scale = HEAD ** -0.5 if a == 0 else 1.0
            for h in range(hb):
                yh = y[:, h * HEAD:(h + 1) * HEAD]
                ss = jnp.sum(yh * yh, axis=-1, keepdims=True)
                act[a, :, h * HEAD:(h + 1) * HEAD] = yh * lax.rsqrt(ss + EPS) * scale

    ii = lax.broadcasted_iota(jnp.int32, (cs, cs), 0)
    jj = lax.broadcasted_iota(jnp.int32, (cs, cs), 1)
    causal = ii >= jj
    strict = ii > jj
    eye_f = (ii == jj).astype(F32)
    gain = gain_ref[...]
    n_double = int(math.log2(cs)) - 1
    chains = [(j, h) for j in range(nct) for h in range(hb)]

    def tile(ref, c):
        return ref[0, c[0] * cs:(c[0] + 1) * cs, c[1] * HEAD:(c[1] + 1) * HEAD]

    def act_tile(a, c):
        return act[a, c[0] * cs:(c[0] + 1) * cs, c[1] * HEAD:(c[1] + 1) * HEAD]

    k = [act_tile(1, c) for c in chains]
    beta = [tile(beta_ref, c) for c in chains]
    gam = [tile(gam_ref, c) for c in chains]
    kb = [a * b for a, b in zip(k, beta)]
    raw = [_dot_nt(jnp.concatenate([b, act_tile(0, c)], axis=0).astype(BF16), a.astype(BF16))
           for a, b, c in zip(k, kb, chains)]
    decay = []
    for g, (j, h) in zip(gam, chains):
        gam_row = grow_ref[0, h, pl.ds(ti * nct + j, 1), :]
        decay.append(jnp.where(causal, jnp.exp(jnp.where(causal, g[:, :cs] - gam_row, 0.0)), 0.0))
    qk = [jnp.where(causal, r[cs:] * d, 0.0) for r, d in zip(raw, decay)]
    pw = [jnp.where(strict, -(r[:cs] * d), 0.0) for r, d in zip(raw, decay)]
    inv = [eye_f + p for p in pw]
    pw = [_dot_3pass(p, p) for p in pw]
    for _ in range(n_double - 1):
        prod = [_dot_3pass(jnp.concatenate([p, x], axis=0), p) for p, x in zip(pw, inv)]
        pw = [p[:cs] for p in prod]
        inv = [x + p[cs:] for x, p in zip(inv, prod)]
    inv = [x + _dot_3pass(x, p) for x, p in zip(inv, pw)]
    e_gam = [jnp.exp(g) for g in gam]
    uw = [_dot_3pass(x, jnp.concatenate([act_tile(2, c) * b, a * e], axis=1))
          for x, c, b, a, e in zip(inv, chains, beta, kb, e_gam)]
    wq = [jnp.concatenate([x[:, HEAD:], act_tile(0, c) * e], axis=0).astype(BF16)
          for x, c, e in zip(uw, chains, e_gam)]
    k_dec_t = [(a * jnp.exp(g[cs - 1:cs, :] - g)).T.astype(BF16) for a, g in zip(k, gam)]
    state = [s_ref[0, h] for h in range(hb)]
    for j in range(nct):
        idx = [j * hb + h for h in range(hb)]
        ws = [_dot(wq[i], state[h].astype(BF16)) for h, i in enumerate(idx)]
        v16 = [(uw[i][:, :HEAD] - x[:cs]).astype(BF16) for x, i in zip(ws, idx)]
        o = [x[cs:] + _dot(qk[i].astype(BF16), y) for x, y, i in zip(ws, v16, idx)]
        state = [s * jnp.exp(gam[i][cs - 1:cs, :]) + _dot(k_dec_t[i], y) for s, y, i in zip(state, v16, idx)]
        for h, i in enumerate(idx):
            ms = jnp.mean(o[h] * o[h], axis=-1, keepdims=True)
            on = o[h] * lax.rsqrt(ms + EPS) * gain
            o_ref[0, j * cs:(j + 1) * cs, h * HEAD:(h + 1) * HEAD] = (
                on * _silu(tile(z_ref, (j, h)))).astype(o_ref.dtype)
    for h in range(hb):
        s_ref[0, h] = state[h]


def gdn(proj3d, z_off, conv_w, conv_state, beta_rep, gam_rep, gam_row4, s0, gain, hb, tt):
    bn, t, _ = proj3d.shape
    n_heads = s0.shape[1]
    nc, cs = gam_row4.shape[2], gam_row4.shape[3]
    cw = hb * HEAD
    ng = n_heads // hb
    assert z_off % cw == 0 and n_heads % hb == 0 and t % tt == 0 and tt % cs == 0 and tt % SUBLANES == 0
    blk = (1, tt, cw)
    sblk = (1, hb, HEAD, HEAD)
    qkv_specs = [pl.BlockSpec(blk, functools.partial(lambda b, h, i, a: (b, i, a * ng + h), a=a)) for a in range(3)]
    w_specs = [pl.BlockSpec((CONV_A, cw), functools.partial(lambda b, h, i, a: (0, a * ng + h), a=a)) for a in range(3)]
    c_specs = [pl.BlockSpec((1, SUBLANES, cw), functools.partial(lambda b, h, i, a: (b, 0, a * ng + h), a=a))
               for a in range(3)]
    return pl.pallas_call(
        functools.partial(_gdn_kernel, hb=hb, cs=cs, nct=tt // cs),
        grid=(bn, ng, t // tt),
        in_specs=qkv_specs + w_specs + c_specs + [
                  pl.BlockSpec(blk, lambda b, h, i: (b, i, z_off // cw + h)),
                  pl.BlockSpec(blk, lambda b, h, i: (b, i, h)),
                  pl.BlockSpec(blk, lambda b, h, i: (b, i, h)),
                  pl.BlockSpec((1, hb, nc, cs), lambda b, h, i: (b, h, 0, 0)),
                  pl.BlockSpec(sblk, lambda b, h, i: (b, h, 0, 0)),
                  pl.BlockSpec((1, HEAD), lambda b, h, i: (0, 0))],
        out_specs=[pl.BlockSpec(blk, lambda b, h, i: (b, i, h)),
                   pl.BlockSpec(sblk, lambda b, h, i: (b, h, 0, 0))],
        out_shape=[jax.ShapeDtypeStruct((bn, t, n_heads * HEAD), BF16),
                   jax.ShapeDtypeStruct((bn, n_heads, HEAD, HEAD), F32)],
        scratch_shapes=[pltpu.VMEM((3, tt + SUBLANES, cw), F32), pltpu.VMEM((3, tt, cw), F32),
                        pltpu.VMEM((3, SUBLANES, cw), F32)],
        compiler_params=_cparams(3),
        name="gdn",
    )(proj3d, proj3d, proj3d, conv_w, conv_w, conv_w, conv_state, conv_state, conv_state, proj3d, beta_rep,
      gam_rep, gam_row4, s0, gain.reshape(1, HEAD))


def _mem_attn_kernel(q_ref, z_ref, mk_ref, mv_ref, gain_ref, o_ref):
    gain = gain_ref[...]
    for h in range(H_M):
        cols = slice(h * HEAD, (h + 1) * HEAD)
        q = q_ref[0, :, cols]
        ms = jnp.mean(q * q, axis=-1, keepdims=True)
        qn = (q * lax.rsqrt(ms + EPS) * gain).astype(BF16)
        sc = _dot_nt(qn, mk_ref[0, :, cols].astype(BF16)) * (HEAD ** -0.5)
        m = jnp.max(sc, axis=-1, keepdims=True)
        p = jnp.exp(sc - m)
        p = p / jnp.sum(p, axis=-1, keepdims=True)
        o = _dot(p.astype(BF16), mv_ref[0, :, cols].astype(BF16))
        o_ref[0, :, cols] = (o * _silu(z_ref[0, :, cols])).astype(o_ref.dtype)


def mem_attn(proj3d, q_off, z_off, mem_k, mem_v, q_gain, tt):
    bn, t, _ = proj3d.shape
    n_mem = mem_k.shape[1]
    wm = H_M * HEAD
    assert q_off % wm == 0 and z_off % wm == 0 and t % tt == 0
    return pl.pallas_call(
        _mem_attn_kernel,
        grid=(bn, t // tt),
        in_specs=[pl.BlockSpec((1, tt, wm), lambda b, i: (b, i, q_off // wm)),
                  pl.BlockSpec((1, tt, wm), lambda b, i: (b, i, z_off // wm)),
                  pl.BlockSpec((1, n_mem, wm), lambda b, i: (b, 0, 0)),
                  pl.BlockSpec((1, n_mem, wm), lambda b, i: (b, 0, 0)),
                  pl.BlockSpec((1, HEAD), lambda b, i: (0, 0))],
        out_specs=pl.BlockSpec((1, tt, wm), lambda b, i: (b, i, 0)),
        out_shape=jax.ShapeDtypeStruct((bn, t, wm), BF16),
        compiler_params=_cparams(2),
        name="mem_attn",
    )(proj3d, proj3d, mem_k.reshape(bn, n_mem, wm), mem_v.reshape(bn, n_mem, wm), q_gain.reshape(1, HEAD))


def _dil_prompt_kernel(q_ref, k_ref, v_ref, z_ref, o_ref, acc_ref, lse_ref, *, hg, t, group):
    qi = lax.broadcasted_iota(jnp.int32, (SWA_BLOCK, SWA_BLOCK), 0)
    ki = lax.broadcasted_iota(jnp.int32, (SWA_BLOCK, SWA_BLOCK), 1)
    n_pat = len(SWA_PATTERNS)
    for h in range(hg):
        cols = slice(h * HEAD, (h + 1) * HEAD)
        for p, (window, dil) in enumerate(SWA_PATTERNS):
            assert window // dil == SWA_BLOCK
            l_sub = t // dil
            nb = l_sub // SWA_BLOCK
            assert nb * SWA_BLOCK * dil == t

            def rows_of(r, c, dil=dil):
                return pl.ds(r + c * (SWA_BLOCK * dil), SWA_BLOCK, stride=dil)

            blocks = [(r, c) for r in range(dil) for c in range(nb)]
            for g0 in range(0, len(blocks), group):
                grp = blocks[g0:g0 + group]
                q = [(q_ref[0, rows_of(r, c), cols] * (HEAD ** -0.5)).astype(BF16) for r, c in grp]
                s_cur = [jnp.where(ki <= qi, _dot_nt(x, k_ref[0, rows_of(r, c), cols].astype(BF16)), -jnp.inf)
                         for x, (r, c) in zip(q, grp)]
                s_prev = [jnp.where(ki >= qi, _dot_nt(x, k_ref[0, rows_of(r, c - 1), cols].astype(BF16)), -jnp.inf)
                          if c > 0 else None for x, (r, c) in zip(q, grp)]
                m = [jnp.max(a, axis=-1, keepdims=True) if b is None else
                     jnp.maximum(jnp.max(a, axis=-1, keepdims=True), jnp.max(b, axis=-1, keepdims=True))
                     for a, b in zip(s_cur, s_prev)]
                e_cur = [jnp.exp(a - x) for a, x in zip(s_cur, m)]
                e_prev = [None if b is None else jnp.exp(b - x) for b, x in zip(s_prev, m)]
                den = [jnp.sum(a, axis=-1, keepdims=True) if b is None else
                       jnp.sum(a, axis=-1, keepdims=True) + jnp.sum(b, axis=-1, keepdims=True)
                       for a, b in zip(e_cur, e_prev)]
                o = [_dot(a.astype(BF16), v_ref[0, rows_of(r, c), cols].astype(BF16)) for a, (r, c) in zip(e_cur, grp)]
                o = [x if b is None else x + _dot(b.astype(BF16), v_ref[0, rows_of(r, c - 1), cols].astype(BF16))
                     for x, b, (r, c) in zip(o, e_prev, grp)]
                for x, d, mx, (r, c) in zip(o, den, m, grp):
                    acc_ref[p, rows_of(r, c), :] = x / d
                    lse_ref[p, rows_of(r, c), :] = jnp.broadcast_to(mx + jnp.log(d), (SWA_BLOCK, HEAD))
        lse = [lse_ref[p] for p in range(n_pat)]
        top = functools.reduce(jnp.maximum, lse)
        wgt = [jnp.exp(x - top) for x in lse]
        tot = functools.reduce(lambda a, b: a + b, wgt)
        mix = functools.reduce(lambda a, b: a + b, [(wgt[p] / tot) * acc_ref[p] for p in range(n_pat)])
        o_ref[0, :, cols] = (mix * _silu(z_ref[0, :, cols])).astype(o_ref.dtype)


def dil_prompt(qn, k_stack, v_stack, layer, proj3d, z_off, hg):
    bn, t, wc = qn.shape
    cw = hg * HEAD
    assert z_off % cw == 0 and wc % cw == 0
    blk = (1, t, cw)
    sblk = (None, 1, t, cw)
    return pl.pallas_call(
        functools.partial(_dil_prompt_kernel, hg=hg, t=t, group=8),
        grid=(bn, wc // cw),
        in_specs=[pl.BlockSpec(blk, lambda b, h: (b, 0, h)),
                  pl.BlockSpec(sblk, lambda b, h: (layer, b, 0, h)),
                  pl.BlockSpec(sblk, lambda b, h: (layer, b, 0, h)),
                  pl.BlockSpec(blk, lambda b, h: (b, 0, z_off // cw + h))],
        out_specs=pl.BlockSpec(blk, lambda b, h: (b, 0, h)),
        out_shape=jax.ShapeDtypeStruct((bn, t, wc), BF16),
        scratch_shapes=[pltpu.VMEM((len(SWA_PATTERNS), t, HEAD), F32),
                        pltpu.VMEM((len(SWA_PATTERNS), t, HEAD), F32)],
        compiler_params=_cparams(2),
        name="dil_prompt",
    )(qn, k_stack, v_stack, proj3d)


MASKED_MAX = -1e30


def _dil_sample_kernel(q_ref, kn_ref, vn_ref, z_ref, kc_ref, vc_ref, o_ref, m_ref, l_ref, acc_ref,
                       *, n_heads, t, wb, tw, group):
    j = pl.program_id(1)
    n_pat = len(SWA_PATTERNS)

    @pl.when(j == 0)
    def _():
        m_ref[...] = jnp.full(m_ref.shape, MASKED_MAX, F32)
        l_ref[...] = jnp.zeros(l_ref.shape, F32)
        acc_ref[...] = jnp.zeros(acc_ref.shape, F32)

    def head_rows(ref, h):
        return ref[0, pl.ds(h, tw, stride=n_heads), :].astype(BF16)

    def q16(h):
        return (q_ref[0, :, h * HEAD:(h + 1) * HEAD] * (HEAD ** -0.5)).astype(BF16)

    def update(hs, p, scores, ok, values):
        idx = [h * n_pat + p for h in hs]
        s = [jnp.where(ok, x, -jnp.inf) for x in scores]
        m_old = [m_ref[i] for i in idx]
        m_new = [jnp.maximum(a, jnp.max(x, axis=-1, keepdims=True)) for a, x in zip(m_old, s)]
        alpha = [jnp.exp(a - b) for a, b in zip(m_old, m_new)]
        e = [jnp.exp(x - b[:, :1]) for x, b in zip(s, m_new)]
        pv = [_dot(x.astype(BF16), v) for x, v in zip(e, values)]
        for i, a, x, y, b in zip(idx, alpha, e, pv, m_new):
            l_ref[i] = l_ref[i] * a + jnp.sum(x, axis=-1, keepdims=True)
            acc_ref[i] = acc_ref[i] * a + y
            m_ref[i] = b

    row = lax.broadcasted_iota(jnp.int32, (t, tw), 0)
    col = lax.broadcasted_iota(jnp.int32, (t, tw), 1)
    delta = wb + row - (j * tw + col)
    nearest = wb - (j + 1) * tw + 1
    for p, (window, dil) in enumerate(SWA_PATTERNS):
        def tile_update(p=p, window=window, dil=dil):
            ok = ((delta & (dil - 1)) == 0) & (delta <= window)
            for g0 in range(0, n_heads, group):
                hs = list(range(g0, g0 + group))
                scores = [_dot_nt(q16(h), head_rows(kc_ref, h)) for h in hs]
                update(hs, p, scores, ok, [head_rows(vc_ref, h) for h in hs])
        if window >= wb - tw + 1:
            tile_update()
        else:
            pl.when(nearest <= window)(tile_update)

    @pl.when(j == pl.num_programs(1) - 1)
    def _():
        dn = lax.broadcasted_iota(jnp.int32, (t, t), 0) - lax.broadcasted_iota(jnp.int32, (t, t), 1)
        for g0 in range(0, n_heads, group):
            hs = list(range(g0, g0 + group))
            cols = [slice(h * HEAD, (h + 1) * HEAD) for h in hs]
            scores = [_dot_nt(q16(h), kn_ref[0, :, c].astype(BF16)) for h, c in zip(hs, cols)]
            values = [vn_ref[0, :, c].astype(BF16) for c in cols]
            for p, (window, dil) in enumerate(SWA_PATTERNS):
                ok = (dn >= 0) & ((dn & (dil - 1)) == 0) & (dn <= window)
                update(hs, p, scores, ok, values)
            for h, c in zip(hs, cols):
                idx = [h * n_pat + p for p in range(n_pat)]
                lse = [m_ref[i] + jnp.log(l_ref[i]) for i in idx]
                top = functools.reduce(jnp.maximum, lse)
                wgt = [jnp.exp(x - top) for x in lse]
                tot = functools.reduce(lambda a, b: a + b, wgt)
                mix = functools.reduce(lambda a, b: a + b,
                                       [(w / tot) * (acc_ref[i] / l_ref[i]) for w, i in zip(wgt, idx)])
                o_ref[0, :, c] = (mix * _silu(z_ref[0, :, c])).astype(o_ref.dtype)


def dil_sample(qn, k_stack, v_stack, layer, proj3d, z_off, cache_k, cache_v, n_heads, tw):
    bn, t, wc = qn.shape
    wb = cache_k.shape[2] // n_heads
    assert z_off % wc == 0 and wb % tw == 0 and wc == n_heads * HEAD
    for window, dil in SWA_PATTERNS:
        assert dil & (dil - 1) == 0
    blk = (1, t, wc)
    sblk = (None, 1, t, wc)
    cblk = (None, 1, tw * n_heads, HEAD)
    state = pltpu.VMEM((n_heads * len(SWA_PATTERNS), t, HEAD), F32)
    return pl.pallas_call(
        functools.partial(_dil_sample_kernel, n_heads=n_heads, t=t, wb=wb, tw=tw, group=8),
        grid=(bn, wb // tw),
        in_specs=[pl.BlockSpec(blk, lambda b, j: (b, 0, 0)),
                  pl.BlockSpec(sblk, lambda b, j: (layer, b, 0, 0)),
                  pl.BlockSpec(sblk, lambda b, j: (layer, b, 0, 0)),
                  pl.BlockSpec(blk, lambda b, j: (b, 0, z_off // wc)),
                  pl.BlockSpec(cblk, lambda b, j: (layer, b, j, 0)),
                  pl.BlockSpec(cblk, lambda b, j: (layer, b, j, 0))],
        out_specs=pl.BlockSpec(blk, lambda b, j: (b, 0, 0)),
        out_shape=jax.ShapeDtypeStruct((bn, t, wc), BF16),
        scratch_shapes=[state, state, state],
        compiler_params=_cparams(2),
        name="dil_sample",
    )(qn, k_stack, v_stack, proj3d, cache_k, cache_v)


CACHE_ROLL_CHUNK_BYTES = 4 * 1024 * 1024


def _cache_roll_kernel(ck_hbm, cv_hbm, kn_ref, vn_ref, ok_hbm, ov_hbm, buf, stage, in_sems, out_sems, new_sems,
                       *, n_heads, t, chunk, n_chunks):
    l = pl.program_id(0)
    b = pl.program_id(1)
    new = t * n_heads
    srcs = (ck_hbm, cv_hbm)
    dsts = (ok_hbm, ov_hbm)

    def read(a, c):
        return pltpu.make_async_copy(srcs[a].at[l, b, pl.ds(new + c * chunk, chunk)], buf.at[a, c % 2],
                                     in_sems.at[a, c % 2])

    def write(a, c):
        return pltpu.make_async_copy(buf.at[a, c % 2], dsts[a].at[l, b, pl.ds(c * chunk, chunk)],
                                     out_sems.at[a, c % 2])

    def write_new(a):
        return pltpu.make_async_copy(stage.at[a], dsts[a].at[l, b, pl.ds(n_chunks * chunk, new)], new_sems.at[a])

    for a in range(2):
        read(a, 0).start()
    for h in range(n_heads):
        stage[0, pl.ds(h, t, stride=n_heads), :] = kn_ref[:, h * HEAD:(h + 1) * HEAD]
        stage[1, pl.ds(h, t, stride=n_heads), :] = vn_ref[:, h * HEAD:(h + 1) * HEAD]
    for a in range(2):
        write_new(a).start()
    for c in range(n_chunks):
        for a in range(2):
            read(a, c).wait()
            write(a, c).start()
            if c + 1 < n_chunks:
                if c >= 1:
                    write(a, c - 1).wait()
                read(a, c + 1).start()
    for a in range(2):
        if n_chunks >= 2:
            write(a, n_chunks - 2).wait()
        write(a, n_chunks - 1).wait()
        write_new(a).wait()


def cache_roll(cache_k, cache_v, k_new, v_new, n_heads):
    n_l, bn, rows, _ = cache_k.shape
    t = k_new.shape[2]
    wb = rows // n_heads
    keep_tokens = wb - t
    token_bytes = n_heads * HEAD * cache_k.dtype.itemsize
    chunk_tokens = max(d for d in range(1, keep_tokens + 1)
                       if keep_tokens % d == 0 and d * token_bytes <= CACHE_ROLL_CHUNK_BYTES)
    chunk = chunk_tokens * n_heads
    nblk = (None, None, t, n_heads * HEAD)
    any_spec = pl.BlockSpec(memory_space=pl.ANY)
    out = jax.ShapeDtypeStruct(cache_k.shape, cache_k.dtype)
    return pl.pallas_call(
        functools.partial(_cache_roll_kernel, n_heads=n_heads, t=t, chunk=chunk, n_chunks=keep_tokens // chunk_tokens),
        grid=(n_l, bn),
        in_specs=[any_spec, any_spec,
                  pl.BlockSpec(nblk, lambda l, b: (l, b, 0, 0)),
                  pl.BlockSpec(nblk, lambda l, b: (l, b, 0, 0))],
        out_specs=[any_spec, any_spec],
        out_shape=[out, out],
        scratch_shapes=[pltpu.VMEM((2, 2, chunk, HEAD), cache_k.dtype),
                        pltpu.VMEM((2, t * n_heads, HEAD), cache_k.dtype),
                        pltpu.SemaphoreType.DMA((2, 2)), pltpu.SemaphoreType.DMA((2, 2)),
                        pltpu.SemaphoreType.DMA((2,))],
        compiler_params=_cparams(2),
        name="cache_roll",
    )(cache_k, cache_v, k_new, v_new)


def _row_tile(m, pref):
    return pref if m % pref == 0 else m


def _in_proj(x2d, g_norm, w_stack, layer, n_main):
    m = x2d.shape[0]
    h = rms_cast(x2d, g_norm, _row_tile(m, 256))
    return h, matmul([h], w_stack, layer, 0, n_main, _row_tile(m, 512), 1024)


def _pad_state(buf):
    return jnp.pad(buf, ((0, 0), (SUBLANES - buf.shape[1], 0), (0, 0)))


def even_layer(x, conv_a_buf, s0, conv_b_buf, mem_k, mem_v, layer, g_norm, w_main, w_small, conv_a_w, a_log,
               dt_bias, gdn_gain, conv_b_w, w_out, q_gain_m):
    bn, t, d = x.shape
    m = bn * t
    n_heads = a_log.shape[0]
    w_a = n_heads * HEAD
    w_b = conv_b_w.shape[1]
    w_m = H_M * HEAD
    off_za = 3 * w_a
    off_b = off_za + w_a
    off_c, off_x, off_zb = off_b + w_b, off_b + 2 * w_b, off_b + 3 * w_b
    off_qm = off_b + 4 * w_b
    off_zm = off_qm + w_m
    n_main = off_zm + w_m
    x2d = x.reshape(m, d)
    h, proj = _in_proj(x2d, g_norm, w_main, layer, n_main)
    small = matmul([h], w_small, layer, 0, HEAD, _row_tile(m, 512), HEAD)
    proj3d = proj.reshape(bn, t, n_main)
    tt = _row_tile(t, 512)
    cs = min(GDN_CHUNK, t)
    nc = t // cs
    beta_rep, gam_rep, gam_small = gdn_gates(small, a_log, dt_bias, _row_tile(m, 512), cs)
    gam_row4 = (gam_small.reshape(bn, t, HEAD)[:, :, n_heads:2 * n_heads]
                .transpose(0, 2, 1).reshape(bn, n_heads, nc, cs))
    o_a, new_s = gdn(proj3d, off_za, conv_a_w, _pad_state(conv_a_buf), beta_rep.reshape(bn, t, w_a),
                     gam_rep.reshape(bn, t, w_a), gam_row4, s0, gdn_gain, 4, _row_tile(t, 2 * GDN_CHUNK))
    new_conv_a = proj3d[:, t - (CONV_A - 1):, :3 * w_a]
    o_b, nb8 = short_conv(proj3d, off_b, off_c, off_x, off_zb, w_b, _pad_state(conv_b_buf), conv_b_w, tt, 512)
    new_conv_b = nb8[:, SUBLANES - (CONV_B - 1):, :]
    o_m = mem_attn(proj3d, off_qm, off_zm, mem_k, mem_v, q_gain_m, tt)
    y = matmul([o_a.reshape(m, w_a), o_b.reshape(m, w_b), o_m.reshape(m, w_m)], w_out, layer, 0, d,
               _row_tile(m, 512), 1024, res=x2d)
    return y.reshape(bn, t, d), new_conv_a, new_s, new_conv_b


def odd_layer(x, cache_k, cache_v, mem_k, mem_v, layer, n_layers, k_prev, v_prev, g_norm, w_main, q_gain_c,
              k_gain_c, w_out, q_gain_m):
    bn, t, d = x.shape
    m = bn * t
    w_c = d
    w_m = H_M * HEAD
    off_z = 3 * w_c
    off_qm = 4 * w_c
    off_zm = off_qm + w_m
    n_main = off_zm + w_m
    x2d = x.reshape(m, d)
    _, proj = _in_proj(x2d, g_norm, w_main, layer, n_main)
    proj3d = proj.reshape(bn, t, n_main)
    tm = _row_tile(m, 512)
    qn, k_stack, v_stack = odd_qkv(proj, w_c, q_gain_c, k_gain_c, layer, n_layers, k_prev, v_prev, tm, 512)
    qn = qn.reshape(bn, t, w_c)
    k4 = k_stack.reshape(n_layers, bn, t, w_c)
    v4 = v_stack.reshape(n_layers, bn, t, w_c)
    if cache_k is None:
        o_c = dil_prompt(qn, k4, v4, layer, proj3d, off_z, 1)
    else:
        o_c = dil_sample(qn, k4, v4, layer, proj3d, off_z, cache_k, cache_v, w_c // HEAD, 256)
    o_m = mem_attn(proj3d, off_qm, off_zm, mem_k, mem_v, q_gain_m, _row_tile(t, 512))
    y = matmul([o_c.reshape(m, w_c), o_m.reshape(m, w_m)], w_out, layer, 0, d, tm, 1024, res=x2d)
    return y.reshape(bn, t, d), k_stack, v_stack


def mem_project(mem, g, w_kv_stack, layer, k_gain):
    bn, n_mem, d = mem.shape
    m = bn * n_mem
    w_m = H_M * HEAD
    hm = rms_cast(mem.reshape(m, d), g, _row_tile(m, 256))
    kv = matmul([hm], w_kv_stack, layer, 0, 2 * w_m, _row_tile(m, 512), w_m)
    k = headnorm(kv, 0, w_m, k_gain, _row_tile(m, 512), w_m)
    return k.reshape(bn, n_mem, H_M, HEAD), kv[:, w_m:].reshape(bn, n_mem, H_M, HEAD)


def _prep_even_weights(w_in, n_heads):
    w_a = n_heads * HEAD
    cut = 4 * w_a
    w_main = jnp.concatenate([w_in[:, :, :cut], w_in[:, :, cut + 2 * n_heads:]], axis=2).astype(BF16)
    w_small = jnp.pad(w_in[:, :, cut:cut + 2 * n_heads], ((0, 0), (0, 0), (0, HEAD - 2 * n_heads))).astype(BF16)
    return w_main, w_small


def kernel(x_prompt, x_sample, state_gdn, state_gdn_conv, state_sconv, cache_swa_k, cache_swa_v, cache_mem_k, cache_mem_v, mem_prompt, norm_even, w_in_even, conv_a, a_log, dt_bias, gdn_norm, conv_b, w_out_even, norm_odd, w_in_odd, q_norm_c, k_norm_c, w_out_odd, mem_norm, w_mem_kv, q_norm_m, k_norm_m):
    bp, t_p, d = x_prompt.shape
    bs, t_s, _ = x_sample.shape
    depth = mem_norm.shape[0]
    n_odd = norm_odd.shape[0]
    n_heads_a = a_log.shape[1]
    n_heads_c = cache_swa_k.shape[3]
    wb = cache_swa_k.shape[2]
    w_even_main, w_even_small = _prep_even_weights(w_in_even, n_heads_a)
    w_odd_main = w_in_odd.astype(BF16)
    w_out_e = w_out_even.astype(BF16)
    w_out_o = w_out_odd.astype(BF16)
    w_kv = w_mem_kv.astype(BF16)
    cache_k = cache_swa_k.reshape(n_odd, bs, wb * n_heads_c, HEAD)
    cache_v = cache_swa_v.reshape(n_odd, bs, wb * n_heads_c, HEAD)
    y_p, y_s = x_prompt, x_sample
    gdn_p, gdn_s, gconv_p, gconv_s, sconv_p, sconv_s = [], [], [], [], [], []
    memk_p, memv_p = [], []
    kp, vp = (jnp.zeros((n_odd, bp * t_p, d), F32) for _ in range(2))
    ks, vs = (jnp.zeros((n_odd, bs * t_s, d), F32) for _ in range(2))
    for i in range(depth):
        mk_p, mv_p = mem_project(mem_prompt, mem_norm[i], w_kv, i, k_norm_m[i])
        memk_p.append(mk_p)
        memv_p.append(mv_p)
        if i % 2 == 0:
            e = i // 2
            wts = (e, norm_even[e], w_even_main, w_even_small, conv_a[e], a_log[e], dt_bias[e], gdn_norm[e],
                   conv_b[e], w_out_e, q_norm_m[i])
            y_p, ca, sa, cb = even_layer(
                y_p,
                jnp.zeros((bp,) + state_gdn_conv.shape[2:], state_gdn_conv.dtype),
                jnp.zeros((bp,) + state_gdn.shape[2:], state_gdn.dtype),
                jnp.zeros((bp,) + state_sconv.shape[2:], state_sconv.dtype),
                mk_p, mv_p, *wts)
            gconv_p.append(ca)
            gdn_p.append(sa)
            sconv_p.append(cb)
            y_s, ca, sa, cb = even_layer(y_s, state_gdn_conv[e], state_gdn[e], state_sconv[e],
                                         cache_mem_k[i], cache_mem_v[i], *wts)
            gconv_s.append(ca)
            gdn_s.append(sa)
            sconv_s.append(cb)
        else:
            o = i // 2
            wts = (norm_odd[o], w_odd_main, q_norm_c[o], k_norm_c[o], w_out_o, q_norm_m[i])
            y_p, kp, vp = odd_layer(y_p, None, None, mk_p, mv_p, o, n_odd, kp, vp, *wts)
            y_s, ks, vs = odd_layer(y_s, cache_k, cache_v, cache_mem_k[i], cache_mem_v[i], o, n_odd, ks, vs, *wts)
    n_keep = min(max(w for w, _ in SWA_PATTERNS), t_p)
    swa_k_p = kp.reshape(n_odd, bp, t_p, n_heads_c, HEAD)[:, :, t_p - n_keep:]
    swa_v_p = vp.reshape(n_odd, bp, t_p, n_heads_c, HEAD)[:, :, t_p - n_keep:]
    swa_k_s, swa_v_s = cache_roll(cache_k, cache_v, ks.reshape(n_odd, bs, t_s, d), vs.reshape(n_odd, bs, t_s, d),
                                  n_heads_c)
    swa_k_s = swa_k_s.reshape(n_odd, bs, wb, n_heads_c, HEAD)
    swa_v_s = swa_v_s.reshape(n_odd, bs, wb, n_heads_c, HEAD)
    return (y_p, y_s, jnp.stack(gdn_p), jnp.stack(gdn_s), jnp.stack(gconv_p), jnp.stack(gconv_s),
            jnp.stack(sconv_p), jnp.stack(sconv_s), swa_k_p, swa_k_s, swa_v_p, swa_v_s,
            jnp.stack(memk_p), jnp.stack(memv_p))
```

```python
import functools
import math

import jax
import jax.numpy as jnp
from jax import lax
from jax.experimental import pallas as pl
from jax.experimental.pallas import tpu as pltpu

F32 = jnp.float32
BF16 = jnp.bfloat16
EPS = 1e-6
HEAD = 128
SUBLANES = 8
CONV_A = 4
CONV_B = 3
GDN_CHUNK = 64
SWA_PATTERNS = ((128, 1), (512, 4), (2048, 16))
SWA_BLOCK = 128
H_M = 4
VMEM_LIMIT = 48 * 1024 * 1024
HI = lax.Precision.HIGHEST


def _cparams(n_axes):
    return pltpu.CompilerParams(dimension_semantics=("arbitrary",) * n_axes,
                                vmem_limit_bytes=VMEM_LIMIT)


def _silu(x):
    return x * jax.nn.sigmoid(x)


def _dot(a, b, precision=None):
    return jnp.dot(a, b, preferred_element_type=F32, precision=precision)


def _dot_nt(a, b, precision=None):
    return lax.dot_general(a, b, (((1,), (1,)), ((), ())), preferred_element_type=F32,
                           precision=precision)


def _dot_tn(a, b, precision=None):
    return lax.dot_general(a, b, (((0,), (0,)), ((), ())), preferred_element_type=F32,
                           precision=precision)


def _rms_cast_kernel(x_ref, g_ref, o_ref):
    x = x_ref[...]
    ms = jnp.mean(x * x, axis=-1, keepdims=True)
    o_ref[...] = (x * lax.rsqrt(ms + EPS) * g_ref[...]).astype(o_ref.dtype)


def rms_cast(x2d, g, tm):
    m, d = x2d.shape
    return pl.pallas_call(
        _rms_cast_kernel,
        grid=(m // tm,),
        in_specs=[pl.BlockSpec((tm, d), lambda i: (i, 0)),
                  pl.BlockSpec((1, d), lambda i: (0, 0))],
        out_specs=pl.BlockSpec((tm, d), lambda i: (i, 0)),
        out_shape=jax.ShapeDtypeStruct((m, d), BF16),
        compiler_params=_cparams(1),
        name="rms_cast",
    )(x2d, g.reshape(1, d))


def _mm_kernel(*refs, n_in, has_res, has_side):
    n_out = 2 if has_side else 1
    ins, outs = refs[:-n_out], refs[-n_out:]
    w_refs = ins[n_in:2 * n_in]
    per_group = n_in + (1 if has_res else 0)

    def run(a_refs, res_ref, o_ref):
        acc = _dot(a_refs[0][...], w_refs[0][...])
        for a_ref, w_ref in zip(a_refs[1:], w_refs[1:]):
            acc = acc + _dot(a_ref[...], w_ref[...])
        if res_ref is not None:
            acc = acc + res_ref[...]
        o_ref[...] = acc.astype(o_ref.dtype)

    run(ins[:n_in], ins[2 * n_in] if has_res else None, outs[0])
    if has_side:
        side = ins[n_in + per_group:]

        @pl.when(pl.program_id(1) == 0)
        def _():
            run(side[:n_in], side[n_in] if has_res else None, outs[1])


def matmul(a_list, w, layer, n_off, n_size, tm, tn, res=None, side_a=None, side_res=None):
    m = a_list[0].shape[0]
    n_in = len(a_list)
    has_side = side_a is not None
    assert (side_res is not None) == (has_side and res is not None)
    in_specs, w_specs = [], []
    row = 0
    for a in a_list:
        k = a.shape[1]
        assert row % k == 0 and n_off % tn == 0 and n_size % tn == 0 and m % tm == 0
        in_specs.append(pl.BlockSpec((tm, k), lambda n, i: (i, 0)))
        w_specs.append(pl.BlockSpec((None, k, tn), functools.partial(
            lambda n, i, rb, nb: (layer, rb, n + nb), rb=row // k, nb=n_off // tn)))
        row += k
    args = list(a_list) + [w] * n_in
    specs = in_specs + w_specs
    out_specs = [pl.BlockSpec((tm, tn), lambda n, i: (i, n))]
    out_shape = [jax.ShapeDtypeStruct((m, n_size), F32)]
    if res is not None:
        specs.append(pl.BlockSpec((tm, tn), lambda n, i: (i, n)))
        args.append(res)
    if has_side:
        ms = side_a[0].shape[0]
        for a in side_a:
            specs.append(pl.BlockSpec((ms, a.shape[1]), lambda n, i: (0, 0)))
            args.append(a)
        if side_res is not None:
            specs.append(pl.BlockSpec((ms, tn), lambda n, i: (0, n)))
            args.append(side_res)
        out_specs.append(pl.BlockSpec((ms, tn), lambda n, i: (0, n)))
        out_shape.append(jax.ShapeDtypeStruct((ms, n_size), F32))
    out = pl.pallas_call(
        functools.partial(_mm_kernel, n_in=n_in, has_res=res is not None, has_side=has_side),
        grid=(n_size // tn, m // tm),
        in_specs=specs,
        out_specs=out_specs,
        out_shape=out_shape,
        compiler_params=_cparams(2),
        name="matmul",
    )(*args)
    return tuple(out) if has_side else out[0]


def _headnorm_kernel(x_ref, g_ref, o_ref, *, heads):
    g = g_ref[...]
    for h in range(heads):
        cols = slice(h * HEAD, (h + 1) * HEAD)
        x = x_ref[:, cols]
        ms = jnp.mean(x * x, axis=-1, keepdims=True)
        o_ref[:, cols] = x * lax.rsqrt(ms + EPS) * g


def headnorm(x2d, col_off, width, gain, tm, cw):
    m = x2d.shape[0]
    assert col_off % cw == 0 and width % cw == 0 and m % tm == 0
    return pl.pallas_call(
        functools.partial(_headnorm_kernel, heads=cw // HEAD),
        grid=(m // tm, width // cw),
        in_specs=[pl.BlockSpec((tm, cw), lambda i, c: (i, c + col_off // cw)),
                  pl.BlockSpec((1, HEAD), lambda i, c: (0, 0))],
        out_specs=pl.BlockSpec((tm, cw), lambda i, c: (i, c)),
        out_shape=jax.ShapeDtypeStruct((m, width), F32),
        compiler_params=_cparams(2),
        name="headnorm",
    )(x2d, gain.reshape(1, HEAD))


def _odd_qkv_kernel(q_ref, k_ref, v_ref, qg_ref, kg_ref, k_prev_hbm, v_prev_hbm, qo_ref, ko_ref, vo_ref, *, heads):
    del k_prev_hbm, v_prev_hbm
    qg = qg_ref[...]
    kg = kg_ref[...]
    for h in range(heads):
        cols = slice(h * HEAD, (h + 1) * HEAD)
        q = q_ref[:, cols]
        qo_ref[:, cols] = q * lax.rsqrt(jnp.mean(q * q, axis=-1, keepdims=True) + EPS) * qg
        k = k_ref[:, cols]
        ko_ref[:, cols] = k * lax.rsqrt(jnp.mean(k * k, axis=-1, keepdims=True) + EPS) * kg
    vo_ref[...] = v_ref[...]


def odd_qkv(proj, w_c, q_gain, k_gain, layer, n_layers, k_prev, v_prev, tm, cw):
    m = proj.shape[0]
    assert w_c % cw == 0 and m % tm == 0 and k_prev.shape == v_prev.shape == (n_layers, m, w_c)
    nb = w_c // cw
    blk = (tm, cw)
    any_spec = pl.BlockSpec(memory_space=pl.ANY)
    stack = jax.ShapeDtypeStruct((n_layers, m, w_c), F32)
    return pl.pallas_call(
        functools.partial(_odd_qkv_kernel, heads=cw // HEAD),
        grid=(m // tm, nb),
        in_specs=[pl.BlockSpec(blk, lambda i, c: (i, c)),
                  pl.BlockSpec(blk, lambda i, c: (i, c + nb)),
                  pl.BlockSpec(blk, lambda i, c: (i, c + 2 * nb)),
                  pl.BlockSpec((1, HEAD), lambda i, c: (0, 0)),
                  pl.BlockSpec((1, HEAD), lambda i, c: (0, 0)),
                  any_spec, any_spec],
        out_specs=[pl.BlockSpec(blk, lambda i, c: (i, c)),
                   pl.BlockSpec((None,) + blk, lambda i, c: (layer, i, c)),
                   pl.BlockSpec((None,) + blk, lambda i, c: (layer, i, c))],
        out_shape=[jax.ShapeDtypeStruct((m, w_c), F32), stack, stack],
        input_output_aliases={5: 1, 6: 2},
        compiler_params=_cparams(2),
        name="odd_qkv",
    )(proj, proj, proj, q_gain.reshape(1, HEAD), k_gain.reshape(1, HEAD), k_prev, v_prev)


def _gate_kernel(x_ref, alog_ref, dt_ref, beta_ref, gam_ref, gsm_ref, tri_ref, eb_ref, eg_ref, *, n_heads, cs):
    tm = x_ref.shape[0]
    wide = n_heads * HEAD

    @pl.when(pl.program_id(0) == 0)
    def _():
        ri = lax.broadcasted_iota(jnp.int32, (tm, tm), 0)
        ci = lax.broadcasted_iota(jnp.int32, (tm, tm), 1)
        tri_ref[...] = jnp.where((ci <= ri) & (ci >= (ri // cs) * cs), 1.0, 0.0).astype(F32)
        src = lax.broadcasted_iota(jnp.int32, (HEAD, wide), 0)
        dst = lax.broadcasted_iota(jnp.int32, (HEAD, wide), 1) // HEAD
        eb_ref[...] = jnp.where(src == dst, 1.0, 0.0).astype(F32)
        eg_ref[...] = jnp.where(src == dst + n_heads, 1.0, 0.0).astype(F32)

    x = x_ref[...]
    beta = jax.nn.sigmoid(x)
    g = -jnp.exp(alog_ref[...]) * jax.nn.softplus(x + dt_ref[...])
    gam = _dot(tri_ref[...], g, HI)
    gsm_ref[...] = gam
    beta_ref[...] = _dot(beta, eb_ref[...], HI)
    gam_ref[...] = _dot(gam, eg_ref[...], HI)


def gdn_gates(small2d, a_log, dt_bias, tm, cs):
    m = small2d.shape[0]
    n_heads = a_log.shape[0]
    assert tm % cs == 0 and m % tm == 0
    pad = HEAD - 2 * n_heads
    alog_row = jnp.concatenate([jnp.zeros((n_heads,), F32), a_log, jnp.zeros((pad,), F32)]).reshape(1, HEAD)
    dt_row = jnp.concatenate([jnp.zeros((n_heads,), F32), dt_bias, jnp.zeros((pad,), F32)]).reshape(1, HEAD)
    wide = n_heads * HEAD
    return pl.pallas_call(
        functools.partial(_gate_kernel, n_heads=n_heads, cs=cs),
        grid=(m // tm,),
        in_specs=[pl.BlockSpec((tm, HEAD), lambda i: (i, 0)),
                  pl.BlockSpec((1, HEAD), lambda i: (0, 0)),
                  pl.BlockSpec((1, HEAD), lambda i: (0, 0))],
        out_specs=[pl.BlockSpec((tm, wide), lambda i: (i, 0)),
                   pl.BlockSpec((tm, wide), lambda i: (i, 0)),
                   pl.BlockSpec((tm, HEAD), lambda i: (i, 0))],
        out_shape=[jax.ShapeDtypeStruct((m, wide), F32), jax.ShapeDtypeStruct((m, wide), F32),
                   jax.ShapeDtypeStruct((m, HEAD), F32)],
        scratch_shapes=[pltpu.VMEM((tm, tm), F32), pltpu.VMEM((HEAD, wide), F32), pltpu.VMEM((HEAD, wide), F32)],
        compiler_params=_cparams(1),
        name="gdn_gates",
    )(small2d, alog_row, dt_row)


def _sconv_kernel(bg_ref, cg_ref, xb_ref, zb_ref, hc_ref, hx_ref, st_ref, w_ref, o_ref, nb_ref, buf_ref, *, tt):
    t = pl.program_id(2)
    buf_ref[0:SUBLANES, :] = jnp.where(t == 0, st_ref[0], hc_ref[0] * hx_ref[0])
    buf_ref[SUBLANES:SUBLANES + tt, :] = cg_ref[0] * xb_ref[0]
    w = w_ref[...]
    first = SUBLANES - (CONV_B - 1)
    y = buf_ref[first:first + tt, :] * w[0:1, :]
    for j in range(1, CONV_B):
        y = y + buf_ref[first + j:first + j + tt, :] * w[j:j + 1, :]
    o_ref[0] = (bg_ref[0] * y * _silu(zb_ref[0])).astype(o_ref.dtype)
    nb_ref[0] = buf_ref[tt:tt + SUBLANES, :]


def short_conv(proj3d, off_b, off_c, off_x, off_z, width, state_pad, conv_w, tt, cw):
    bn, t, _ = proj3d.shape
    assert t % tt == 0 and width % cw == 0
    rb = tt // SUBLANES

    def col(off):
        assert off % cw == 0
        return functools.partial(lambda b, c, i, o: (b, i, c + o), o=off // cw)

    def halo(off):
        return functools.partial(lambda b, c, i, o: (b, jnp.maximum(i * rb - 1, 0), c + o), o=off // cw)

    blk = (1, tt, cw)
    hblk = (1, SUBLANES, cw)
    return pl.pallas_call(
        functools.partial(_sconv_kernel, tt=tt),
        grid=(bn, width // cw, t // tt),
        in_specs=[pl.BlockSpec(blk, col(off_b)), pl.BlockSpec(blk, col(off_c)),
                  pl.BlockSpec(blk, col(off_x)), pl.BlockSpec(blk, col(off_z)),
                  pl.BlockSpec(hblk, halo(off_c)), pl.BlockSpec(hblk, halo(off_x)),
                  pl.BlockSpec(hblk, lambda b, c, i: (b, 0, c)),
                  pl.BlockSpec((CONV_B, cw), lambda b, c, i: (0, c))],
        out_specs=[pl.BlockSpec(blk, lambda b, c, i: (b, i, c)),
                   pl.BlockSpec(hblk, lambda b, c, i: (b, 0, c))],
        out_shape=[jax.ShapeDtypeStruct((bn, t, width), BF16),
                   jax.ShapeDtypeStruct((bn, SUBLANES, width), F32)],
        scratch_shapes=[pltpu.VMEM((tt + SUBLANES, cw), F32)],
        compiler_params=_cparams(3),
        name="short_conv",
    )(proj3d, proj3d, proj3d, proj3d, proj3d, proj3d, state_pad, conv_w)


def _split_bf16(x):
    hi = x.astype(BF16)
    return hi, (x - hi.astype(F32)).astype(BF16)


def _dot_3pass(a, b):
    a_hi, a_lo = _split_bf16(a)
    b_hi, b_lo = _split_bf16(b)
    return _dot(a_hi, b_hi) + (_dot(a_hi, b_lo) + _dot(a_lo, b_hi))


def _gdn_kernel(xq_ref, xk_ref, xv_ref, wq_ref, wk_ref, wv_ref, cq_ref, ck_ref, cv_ref, z_ref, beta_ref, gam_ref,
                grow_ref, s0_ref, gain_ref, o_ref, s_ref, xbuf, act, carry, *, hb, cs, nct):
    ti = pl.program_id(2)
    tt = nct * cs

    @pl.when(ti == 0)
    def _():
        s_ref[...] = s0_ref[...]
        for a, c_ref in enumerate((cq_ref, ck_ref, cv_ref)):
            carry[a] = c_ref[0]

    first = SUBLANES - (CONV_A - 1)
    for a, (x_ref, w_ref) in enumerate(((xq_ref, wq_ref), (xk_ref, wk_ref), (xv_ref, wv_ref))):
        xbuf[a, 0:SUBLANES, :] = carry[a]
        xbuf[a, SUBLANES:SUBLANES + tt, :] = x_ref[0]
        carry[a] = x_ref[0, tt - SUBLANES:tt, :]
        w = w_ref[...]
        y = xbuf[a, first:first + tt, :] * w[0:1, :]
        for j in range(1, CONV_A):
            y = y + xbuf[a, first + j:first + j + tt, :] * w[j:j + 1, :]
        y = _silu(y)
        if a == 2:
            act[a] = y
        else:
            scale = HEAD ** -0.5 if a == 0 else 1.0
            for h in range(hb):
                yh = y[:, h * HEAD:(h + 1) * HEAD]
                ss = jnp.sum(yh * yh, axis=-1, keepdims=True)
                act[a, :, h * HEAD:(h + 1) * HEAD] = yh * lax.rsqrt(ss + EPS) * scale

    ii = lax.broadcasted_iota(jnp.int32, (cs, cs), 0)
    jj = lax.broadcasted_iota(jnp.int32, (cs, cs), 1)
    causal = ii >= jj
    strict = ii > jj
    eye_f = (ii == jj).astype(F32)
    gain = gain_ref[...]
    n_double = int(math.log2(cs)) - 1
    chains = [(j, h) for j in range(nct) for h in range(hb)]

    def tile(ref, c):
        return ref[0, c[0] * cs:(c[0] + 1) * cs, c[1] * HEAD:(c[1] + 1) * HEAD]

    def act_tile(a, c):
        return act[a, c[0] * cs:(c[0] + 1) * cs, c[1] * HEAD:(c[1] + 1) * HEAD]

    k = [act_tile(1, c) for c in chains]
    beta = [tile(beta_ref, c) for c in chains]
    gam = [tile(gam_ref, c) for c in chains]
    kb = [a * b for a, b in zip(k, beta)]
    raw = [_dot_nt(jnp.concatenate([b, act_tile(0, c)], axis=0).astype(BF16), a.astype(BF16))
           for a, b, c in zip(k, kb, chains)]
    decay = []
    for g, (j, h) in zip(gam, chains):
        gam_row = grow_ref[0, h, pl.ds(ti * nct + j, 1), :]
        decay.append(jnp.where(causal, jnp.exp(jnp.where(causal, g[:, :cs] - gam_row, 0.0)), 0.0))
    qk = [jnp.where(causal, r[cs:] * d, 0.0) for r, d in zip(raw, decay)]
    pw = [jnp.where(strict, -(r[:cs] * d), 0.0) for r, d in zip(raw, decay)]
    inv = [eye_f + p for p in pw]
    pw = [_dot(p.astype(BF16), p.astype(BF16)) for p in pw]
    for _ in range(n_double - 1):
        prod = [_dot(jnp.concatenate([p, x], axis=0).astype(BF16), p.astype(BF16)) for p, x in zip(pw, inv)]
        pw = [p[:cs] for p in prod]
        inv = [x + p[cs:] for x, p in zip(inv, prod)]
    inv = [x + _dot(x.astype(BF16), p.astype(BF16)) for x, p in zip(inv, pw)]
    e_gam = [jnp.exp(g) for g in gam]
    uw = [_dot_3pass(x, jnp.concatenate([act_tile(2, c) * b, a * e], axis=1))
          for x, c, b, a, e in zip(inv, chains, beta, kb, e_gam)]
    wq = [jnp.concatenate([x[:, HEAD:], act_tile(0, c) * e], axis=0).astype(BF16)
          for x, c, e in zip(uw, chains, e_gam)]
    k_dec_t = [(a * jnp.exp(g[cs - 1:cs, :] - g)).T.astype(BF16) for a, g in zip(k, gam)]
    state = [s_ref[0, h] for h in range(hb)]
    for j in range(nct):
        idx = [j * hb + h for h in range(hb)]
        ws = [_dot(wq[i], state[h].astype(BF16)) for h, i in enumerate(idx)]
        v16 = [(uw[i][:, :HEAD] - x[:cs]).astype(BF16) for x, i in zip(ws, idx)]
        o = [x[cs:] + _dot(qk[i].astype(BF16), y) for x, y, i in zip(ws, v16, idx)]
        state = [s * jnp.exp(gam[i][cs - 1:cs, :]) + _dot(k_dec_t[i], y) for s, y, i in zip(state, v16, idx)]
        for h, i in enumerate(idx):
            ms = jnp.mean(o[h] * o[h], axis=-1, keepdims=True)
            on = o[h] * lax.rsqrt(ms + EPS) * gain
            o_ref[0, j * cs:(j + 1) * cs, h * HEAD:(h + 1) * HEAD] = (
                on * _silu(tile(z_ref, (j, h)))).astype(o_ref.dtype)
    for h in range(hb):
        s_ref[0, h] = state[h]


def gdn(proj3d, z_off, conv_w, conv_state, beta_rep, gam_rep, gam_row4, s0, gain, hb, tt):
    bn, t, _ = proj3d.shape
    n_heads = s0.shape[1]
    nc, cs = gam_row4.shape[2], gam_row4.shape[3]
    cw = hb * HEAD
    ng = n_heads // hb
    assert z_off % cw == 0 and n_heads % hb == 0 and t % tt == 0 and tt % cs == 0 and tt % SUBLANES == 0
    blk = (1, tt, cw)
    sblk = (1, hb, HEAD, HEAD)
    qkv_specs = [pl.BlockSpec(blk, functools.partial(lambda b, h, i, a: (b, i, a * ng + h), a=a)) for a in range(3)]
    w_specs = [pl.BlockSpec((CONV_A, cw), functools.partial(lambda b, h, i, a: (0, a * ng + h), a=a)) for a in range(3)]
    c_specs = [pl.BlockSpec((1, SUBLANES, cw), functools.partial(lambda b, h, i, a: (b, 0, a * ng + h), a=a))
               for a in range(3)]
    return pl.pallas_call(
        functools.partial(_gdn_kernel, hb=hb, cs=cs, nct=tt // cs),
        grid=(bn, ng, t // tt),
        in_specs=qkv_specs + w_specs + c_specs + [
                  pl.BlockSpec(blk, lambda b, h, i: (b, i, z_off // cw + h)),
                  pl.BlockSpec(blk, lambda b, h, i: (b, i, h)),
                  pl.BlockSpec(blk, lambda b, h, i: (b, i, h)),
                  pl.BlockSpec((1, hb, nc, cs), lambda b, h, i: (b, h, 0, 0)),
                  pl.BlockSpec(sblk, lambda b, h, i: (b, h, 0, 0)),
                  pl.BlockSpec((1, HEAD), lambda b, h, i: (0, 0))],
        out_specs=[pl.BlockSpec(blk, lambda b, h, i: (b, i, h)),
                   pl.BlockSpec(sblk, lambda b, h, i: (b, h, 0, 0))],
        out_shape=[jax.ShapeDtypeStruct((bn, t, n_heads * HEAD), BF16),
                   jax.ShapeDtypeStruct((bn, n_heads, HEAD, HEAD), F32)],
        scratch_shapes=[pltpu.VMEM((3, tt + SUBLANES, cw), F32), pltpu.VMEM((3, tt, cw), F32),
                        pltpu.VMEM((3, SUBLANES, cw), F32)],
        compiler_params=_cparams(3),
        name="gdn",
    )(proj3d, proj3d, proj3d, conv_w, conv_w, conv_w, conv_state, conv_state, conv_state, proj3d, beta_rep,
      gam_rep, gam_row4, s0, gain.reshape(1, HEAD))


def _mem_attn_kernel(q_ref, z_ref, mk_ref, mv_ref, gain_ref, o_ref):
    gain = gain_ref[...]
    for h in range(H_M):
        cols = slice(h * HEAD, (h + 1) * HEAD)
        q = q_ref[0, :, cols]
        ms = jnp.mean(q * q, axis=-1, keepdims=True)
        qn = (q * lax.rsqrt(ms + EPS) * gain).astype(BF16)
        sc = _dot_nt(qn, mk_ref[0, :, cols].astype(BF16)) * (HEAD ** -0.5)
        m = jnp.max(sc, axis=-1, keepdims=True)
        p = jnp.exp(sc - m)
        p = p / jnp.sum(p, axis=-1, keepdims=True)
        o = _dot(p.astype(BF16), mv_ref[0, :, cols].astype(BF16))
        o_ref[0, :, cols] = (o * _silu(z_ref[0, :, cols])).astype(o_ref.dtype)


def mem_attn(proj3d, q_off, z_off, mem_k, mem_v, q_gain, tt):
    bn, t, _ = proj3d.shape
    n_mem = mem_k.shape[1]
    wm = H_M * HEAD
    assert q_off % wm == 0 and z_off % wm == 0 and t % tt == 0
    return pl.pallas_call(
        _mem_attn_kernel,
        grid=(bn, t // tt),
        in_specs=[pl.BlockSpec((1, tt, wm), lambda b, i: (b, i, q_off // wm)),
                  pl.BlockSpec((1, tt, wm), lambda b, i: (b, i, z_off // wm)),
                  pl.BlockSpec((1, n_mem, wm), lambda b, i: (b, 0, 0)),
                  pl.BlockSpec((1, n_mem, wm), lambda b, i: (b, 0, 0)),
                  pl.BlockSpec((1, HEAD), lambda b, i: (0, 0))],
        out_specs=pl.BlockSpec((1, tt, wm), lambda b, i: (b, i, 0)),
        out_shape=jax.ShapeDtypeStruct((bn, t, wm), BF16),
        compiler_params=_cparams(2),
        name="mem_attn",
    )(proj3d, proj3d, mem_k.reshape(bn, n_mem, wm), mem_v.reshape(bn, n_mem, wm), q_gain.reshape(1, HEAD))


def _dil_prompt_kernel(q_ref, k_ref, v_ref, z_ref, o_ref, acc_ref, lse_ref, *, hg, t, group):
    qi = lax.broadcasted_iota(jnp.int32, (SWA_BLOCK, SWA_BLOCK), 0)
    ki = lax.broadcasted_iota(jnp.int32, (SWA_BLOCK, SWA_BLOCK), 1)
    n_pat = len(SWA_PATTERNS)
    for h in range(hg):
        cols = slice(h * HEAD, (h + 1) * HEAD)
        for p, (window, dil) in enumerate(SWA_PATTERNS):
            assert window // dil == SWA_BLOCK
            l_sub = t // dil
            nb = l_sub // SWA_BLOCK
            assert nb * SWA_BLOCK * dil == t

            def rows_of(r, c, dil=dil):
                return pl.ds(r + c * (SWA_BLOCK * dil), SWA_BLOCK, stride=dil)

            blocks = [(r, c) for r in range(dil) for c in range(nb)]
            for g0 in range(0, len(blocks), group):
                grp = blocks[g0:g0 + group]
                q = [(q_ref[0, rows_of(r, c), cols] * (HEAD ** -0.5)).astype(BF16) for r, c in grp]
                s_cur = [jnp.where(ki <= qi, _dot_nt(x, k_ref[0, rows_of(r, c), cols].astype(BF16)), -jnp.inf)
                         for x, (r, c) in zip(q, grp)]
                s_prev = [jnp.where(ki >= qi, _dot_nt(x, k_ref[0, rows_of(r, c - 1), cols].astype(BF16)), -jnp.inf)
                          if c > 0 else None for x, (r, c) in zip(q, grp)]
                m = [jnp.max(a, axis=-1, keepdims=True) if b is None else
                     jnp.maximum(jnp.max(a, axis=-1, keepdims=True), jnp.max(b, axis=-1, keepdims=True))
                     for a, b in zip(s_cur, s_prev)]
                e_cur = [jnp.exp(a - x) for a, x in zip(s_cur, m)]
                e_prev = [None if b is None else jnp.exp(b - x) for b, x in zip(s_prev, m)]
                den = [jnp.sum(a, axis=-1, keepdims=True) if b is None else
                       jnp.sum(a, axis=-1, keepdims=True) + jnp.sum(b, axis=-1, keepdims=True)
                       for a, b in zip(e_cur, e_prev)]
                o = [_dot(a.astype(BF16), v_ref[0, rows_of(r, c), cols].astype(BF16)) for a, (r, c) in zip(e_cur, grp)]
                o = [x if b is None else x + _dot(b.astype(BF16), v_ref[0, rows_of(r, c - 1), cols].astype(BF16))
                     for x, b, (r, c) in zip(o, e_prev, grp)]
                for x, d, mx, (r, c) in zip(o, den, m, grp):
                    acc_ref[p, rows_of(r, c), :] = x / d
                    lse_ref[p, rows_of(r, c), :] = jnp.broadcast_to(mx + jnp.log(d), (SWA_BLOCK, HEAD))
        lse = [lse_ref[p] for p in range(n_pat)]
        top = functools.reduce(jnp.maximum, lse)
        wgt = [jnp.exp(x - top) for x in lse]
        tot = functools.reduce(lambda a, b: a + b, wgt)
        mix = functools.reduce(lambda a, b: a + b, [(wgt[p] / tot) * acc_ref[p] for p in range(n_pat)])
        o_ref[0, :, cols] = (mix * _silu(z_ref[0, :, cols])).astype(o_ref.dtype)


def dil_prompt(qn, k_stack, v_stack, layer, proj3d, z_off, hg):
    bn, t, wc = qn.shape
    cw = hg * HEAD
    assert z_off % cw == 0 and wc % cw == 0
    blk = (1, t, cw)
    sblk = (None, 1, t, cw)
    return pl.pallas_call(
        functools.partial(_dil_prompt_kernel, hg=hg, t=t, group=8),
        grid=(bn, wc // cw),
        in_specs=[pl.BlockSpec(blk, lambda b, h: (b, 0, h)),
                  pl.BlockSpec(sblk, lambda b, h: (layer, b, 0, h)),
                  pl.BlockSpec(sblk, lambda b, h: (layer, b, 0, h)),
                  pl.BlockSpec(blk, lambda b, h: (b, 0, z_off // cw + h))],
        out_specs=pl.BlockSpec(blk, lambda b, h: (b, 0, h)),
        out_shape=jax.ShapeDtypeStruct((bn, t, wc), BF16),
        scratch_shapes=[pltpu.VMEM((len(SWA_PATTERNS), t, HEAD), F32),
                        pltpu.VMEM((len(SWA_PATTERNS), t, HEAD), F32)],
        compiler_params=_cparams(2),
        name="dil_prompt",
    )(qn, k_stack, v_stack, proj3d)


MASKED_MAX = -1e30


def _dil_sample_kernel(q_ref, kn_ref, vn_ref, z_ref, kc_ref, vc_ref, o_ref, m_ref, l_ref, acc_ref,
                       *, n_heads, t, wb, tw, group):
    j = pl.program_id(1)
    n_pat = len(SWA_PATTERNS)

    @pl.when(j == 0)
    def _():
        m_ref[...] = jnp.full(m_ref.shape, MASKED_MAX, F32)
        l_ref[...] = jnp.zeros(l_ref.shape, F32)
        acc_ref[...] = jnp.zeros(acc_ref.shape, F32)

    def head_rows(ref, h):
        return ref[0, pl.ds(h, tw, stride=n_heads), :].astype(BF16)

    def q16(h):
        return (q_ref[0, :, h * HEAD:(h + 1) * HEAD] * (HEAD ** -0.5)).astype(BF16)

    def update(hs, p, scores, ok, values):
        idx = [h * n_pat + p for h in hs]
        s = [jnp.where(ok, x, -jnp.inf) for x in scores]
        m_old = [m_ref[i] for i in idx]
        m_new = [jnp.maximum(a, jnp.max(x, axis=-1, keepdims=True)) for a, x in zip(m_old, s)]
        alpha = [jnp.exp(a - b) for a, b in zip(m_old, m_new)]
        e = [jnp.exp(x - b[:, :1]) for x, b in zip(s, m_new)]
        pv = [_dot(x.astype(BF16), v) for x, v in zip(e, values)]
        for i, a, x, y, b in zip(idx, alpha, e, pv, m_new):
            l_ref[i] = l_ref[i] * a + jnp.sum(x, axis=-1, keepdims=True)
            acc_ref[i] = acc_ref[i] * a + y
            m_ref[i] = b

    row = lax.broadcasted_iota(jnp.int32, (t, tw), 0)
    col = lax.broadcasted_iota(jnp.int32, (t, tw), 1)
    delta = wb + row - (j * tw + col)
    nearest = wb - (j + 1) * tw + 1
    for p, (window, dil) in enumerate(SWA_PATTERNS):
        def tile_update(p=p, window=window, dil=dil):
            ok = ((delta & (dil - 1)) == 0) & (delta <= window)
            for g0 in range(0, n_heads, group):
                hs = list(range(g0, g0 + group))
                scores = [_dot_nt(q16(h), head_rows(kc_ref, h)) for h in hs]
                update(hs, p, scores, ok, [head_rows(vc_ref, h) for h in hs])
        if window >= wb - tw + 1:
            tile_update()
        else:
            pl.when(nearest <= window)(tile_update)

    @pl.when(j == pl.num_programs(1) - 1)
    def _():
        dn = lax.broadcasted_iota(jnp.int32, (t, t), 0) - lax.broadcasted_iota(jnp.int32, (t, t), 1)
        for g0 in range(0, n_heads, group):
            hs = list(range(g0, g0 + group))
            cols = [slice(h * HEAD, (h + 1) * HEAD) for h in hs]
            scores = [_dot_nt(q16(h), kn_ref[0, :, c].astype(BF16)) for h, c in zip(hs, cols)]
            values = [vn_ref[0, :, c].astype(BF16) for c in cols]
            for p, (window, dil) in enumerate(SWA_PATTERNS):
                ok = (dn >= 0) & ((dn & (dil - 1)) == 0) & (dn <= window)
                update(hs, p, scores, ok, values)
            for h, c in zip(hs, cols):
                idx = [h * n_pat + p for p in range(n_pat)]
                lse = [m_ref[i] + jnp.log(l_ref[i]) for i in idx]
                top = functools.reduce(jnp.maximum, lse)
                wgt = [jnp.exp(x - top) for x in lse]
                tot = functools.reduce(lambda a, b: a + b, wgt)
                mix = functools.reduce(lambda a, b: a + b,
                                       [(w / tot) * (acc_ref[i] / l_ref[i]) for w, i in zip(wgt, idx)])
                o_ref[0, :, c] = (mix * _silu(z_ref[0, :, c])).astype(o_ref.dtype)


def dil_sample(qn, k_stack, v_stack, layer, proj3d, z_off, cache_k, cache_v, n_heads, tw):
    bn, t, wc = qn.shape
    wb = cache_k.shape[2] // n_heads
    assert z_off % wc == 0 and wb % tw == 0 and wc == n_heads * HEAD
    for window, dil in SWA_PATTERNS:
        assert dil & (dil - 1) == 0
    blk = (1, t, wc)
    sblk = (None, 1, t, wc)
    cblk = (None, 1, tw * n_heads, HEAD)
    state = pltpu.VMEM((n_heads * len(SWA_PATTERNS), t, HEAD), F32)
    return pl.pallas_call(
        functools.partial(_dil_sample_kernel, n_heads=n_heads, t=t, wb=wb, tw=tw, group=8),
        grid=(bn, wb // tw),
        in_specs=[pl.BlockSpec(blk, lambda b, j: (b, 0, 0)),
                  pl.BlockSpec(sblk, lambda b, j: (layer, b, 0, 0)),
                  pl.BlockSpec(sblk, lambda b, j: (layer, b, 0, 0)),
                  pl.BlockSpec(blk, lambda b, j: (b, 0, z_off // wc)),
                  pl.BlockSpec(cblk, lambda b, j: (layer, b, j, 0)),
                  pl.BlockSpec(cblk, lambda b, j: (layer, b, j, 0))],
        out_specs=pl.BlockSpec(blk, lambda b, j: (b, 0, 0)),
        out_shape=jax.ShapeDtypeStruct((bn, t, wc), BF16),
        scratch_shapes=[state, state, state],
        compiler_params=_cparams(2),
        name="dil_sample",
    )(qn, k_stack, v_stack, proj3d, cache_k, cache_v)


CACHE_ROLL_CHUNK_BYTES = 4 * 1024 * 1024


def _cache_roll_kernel(ck_hbm, cv_hbm, kn_ref, vn_ref, ok_hbm, ov_hbm, buf, stage, in_sems, out_sems, new_sems,
                       *, n_heads, t, chunk, n_chunks):
    l = pl.program_id(0)
    b = pl.program_id(1)
    new = t * n_heads
    srcs = (ck_hbm, cv_hbm)
    dsts = (ok_hbm, ov_hbm)

    def read(a, c):
        return pltpu.make_async_copy(srcs[a].at[l, b, pl.ds(new + c * chunk, chunk)], buf.at[a, c % 2],
                                     in_sems.at[a, c % 2])

    def write(a, c):
        return pltpu.make_async_copy(buf.at[a, c % 2], dsts[a].at[l, b, pl.ds(c * chunk, chunk)],
                                     out_sems.at[a, c % 2])

    def write_new(a):
        return pltpu.make_async_copy(stage.at[a], dsts[a].at[l, b, pl.ds(n_chunks * chunk, new)], new_sems.at[a])

    for a in range(2):
        read(a, 0).start()
    for h in range(n_heads):
        stage[0, pl.ds(h, t, stride=n_heads), :] = kn_ref[:, h * HEAD:(h + 1) * HEAD]
        stage[1, pl.ds(h, t, stride=n_heads), :] = vn_ref[:, h * HEAD:(h + 1) * HEAD]
    for a in range(2):
        write_new(a).start()
    for c in range(n_chunks):
        for a in range(2):
            read(a, c).wait()
            write(a, c).start()
            if c + 1 < n_chunks:
                if c >= 1:
                    write(a, c - 1).wait()
                read(a, c + 1).start()
    for a in range(2):
        if n_chunks >= 2:
            write(a, n_chunks - 2).wait()
        write(a, n_chunks - 1).wait()
        write_new(a).wait()


def cache_roll(cache_k, cache_v, k_new, v_new, n_heads):
    n_l, bn, rows, _ = cache_k.shape
    t = k_new.shape[2]
    wb = rows // n_heads
    keep_tokens = wb - t
    token_bytes = n_heads * HEAD * cache_k.dtype.itemsize
    chunk_tokens = max(d for d in range(1, keep_tokens + 1)
                       if keep_tokens % d == 0 and d * token_bytes <= CACHE_ROLL_CHUNK_BYTES)
    chunk = chunk_tokens * n_heads
    nblk = (None, None, t, n_heads * HEAD)
    any_spec = pl.BlockSpec(memory_space=pl.ANY)
    out = jax.ShapeDtypeStruct(cache_k.shape, cache_k.dtype)
    return pl.pallas_call(
        functools.partial(_cache_roll_kernel, n_heads=n_heads, t=t, chunk=chunk, n_chunks=keep_tokens // chunk_tokens),
        grid=(n_l, bn),
        in_specs=[any_spec, any_spec,
                  pl.BlockSpec(nblk, lambda l, b: (l, b, 0, 0)),
                  pl.BlockSpec(nblk, lambda l, b: (l, b, 0, 0))],
        out_specs=[any_spec, any_spec],
        out_shape=[out, out],
        scratch_shapes=[pltpu.VMEM((2, 2, chunk, HEAD), cache_k.dtype),
                        pltpu.VMEM((2, t * n_heads, HEAD), cache_k.dtype),
                        pltpu.SemaphoreType.DMA((2, 2)), pltpu.SemaphoreType.DMA((2, 2)),
                        pltpu.SemaphoreType.DMA((2,))],
        compiler_params=_cparams(2),
        name="cache_roll",
    )(cache_k, cache_v, k_new, v_new)


def _row_tile(m, pref):
    return pref if m % pref == 0 else m


def _flat(x):
    return x.reshape(x.shape[0] * x.shape[1], x.shape[2])


def _in_proj(x_big, x_small, g_norm, w_stacks, layer):
    xb, xs = _flat(x_big), _flat(x_small)
    hb = rms_cast(xb, g_norm, _row_tile(xb.shape[0], 256))
    hs = rms_cast(xs, g_norm, _row_tile(xs.shape[0], 256))
    tm = _row_tile(xb.shape[0], 512)
    return [matmul([hb], w, layer, 0, w.shape[2], tm, min(1024, w.shape[2]), side_a=[hs]) for w in w_stacks]


def _out_proj(x_big, x_small, parts_big, parts_small, w_out, layer):
    xb, xs = _flat(x_big), _flat(x_small)
    yb, ys = matmul([_flat(p) for p in parts_big], w_out, layer, 0, xb.shape[1], _row_tile(xb.shape[0], 512), 1024,
                    res=xb, side_a=[_flat(p) for p in parts_small], side_res=xs)
    return yb.reshape(x_big.shape), ys.reshape(x_small.shape)


def _pad_state(buf):
    return jnp.pad(buf, ((0, 0), (SUBLANES - buf.shape[1], 0), (0, 0)))


def _even_branches(shape, proj_a, proj_b, small, conv_a_buf, s0, conv_b_buf, mem_k, mem_v, conv_a_w, a_log, dt_bias,
                   gdn_gain, conv_b_w, q_gain_m):
    bn, t, _ = shape
    m = bn * t
    n_heads = a_log.shape[0]
    w_a = n_heads * HEAD
    w_b = conv_b_w.shape[1]
    w_m = H_M * HEAD
    pa = proj_a.reshape(bn, t, proj_a.shape[1])
    pb = proj_b.reshape(bn, t, proj_b.shape[1])
    tt = _row_tile(t, 512)
    cs = min(GDN_CHUNK, t)
    nc = t // cs
    beta_rep, gam_rep, gam_small = gdn_gates(small, a_log, dt_bias, _row_tile(m, 512), cs)
    gam_row4 = (gam_small.reshape(bn, t, HEAD)[:, :, n_heads:2 * n_heads]
                .transpose(0, 2, 1).reshape(bn, n_heads, nc, cs))
    o_a, new_s = gdn(pa, 3 * w_a, conv_a_w, _pad_state(conv_a_buf), beta_rep.reshape(bn, t, w_a),
                     gam_rep.reshape(bn, t, w_a), gam_row4, s0, gdn_gain, 4, _row_tile(t, 2 * GDN_CHUNK))
    new_conv_a = pa[:, t - (CONV_A - 1):, :3 * w_a]
    o_b, nb8 = short_conv(pb, 0, w_b, 2 * w_b, 3 * w_b, w_b, _pad_state(conv_b_buf), conv_b_w, tt, 512)
    new_conv_b = nb8[:, SUBLANES - (CONV_B - 1):, :]
    o_m = mem_attn(pb, 4 * w_b, 4 * w_b + w_m, mem_k, mem_v, q_gain_m, tt)
    return (o_a, o_b, o_m), (new_conv_a, new_s, new_conv_b)


def even_layer(x_big, x_small, states_big, states_small, layer, g_norm, w_a, w_b, w_small, conv_a_w, a_log, dt_bias,
               gdn_gain, conv_b_w, w_out, q_gain_m):
    (pa_b, pa_s), (pb_b, pb_s), (sm_b, sm_s) = _in_proj(x_big, x_small, g_norm, (w_a, w_b, w_small), layer)
    wts = (conv_a_w, a_log, dt_bias, gdn_gain, conv_b_w, q_gain_m)
    parts_b, new_b = _even_branches(x_big.shape, pa_b, pb_b, sm_b, *states_big, *wts)
    parts_s, new_s = _even_branches(x_small.shape, pa_s, pb_s, sm_s, *states_small, *wts)
    y_b, y_s = _out_proj(x_big, x_small, parts_b, parts_s, w_out, layer)
    return y_b, y_s, new_b, new_s


def _odd_branches(shape, proj, cache_k, cache_v, mem_k, mem_v, layer, n_layers, k_prev, v_prev, q_gain_c, k_gain_c,
                  q_gain_m):
    bn, t, w_c = shape
    m = bn * t
    w_m = H_M * HEAD
    off_z = 3 * w_c
    off_qm = 4 * w_c
    proj3d = proj.reshape(bn, t, proj.shape[1])
    qn, k_stack, v_stack = odd_qkv(proj, w_c, q_gain_c, k_gain_c, layer, n_layers, k_prev, v_prev,
                                   _row_tile(m, 512), 512)
    qn = qn.reshape(bn, t, w_c)
    k4 = k_stack.reshape(n_layers, bn, t, w_c)
    v4 = v_stack.reshape(n_layers, bn, t, w_c)
    if cache_k is None:
        o_c = dil_prompt(qn, k4, v4, layer, proj3d, off_z, 1)
    else:
        o_c = dil_sample(qn, k4, v4, layer, proj3d, off_z, cache_k, cache_v, w_c // HEAD, 256)
    o_m = mem_attn(proj3d, off_qm, off_qm + w_m, mem_k, mem_v, q_gain_m, _row_tile(t, 512))
    return (o_c, o_m), k_stack, v_stack


def odd_layer(x_big, x_small, cache_k, cache_v, mem_big, mem_small, layer, n_layers, kv_big, kv_small, g_norm, w_main,
              q_gain_c, k_gain_c, w_out, q_gain_m):
    ((p_b, p_s),) = _in_proj(x_big, x_small, g_norm, (w_main,), layer)
    gains = (q_gain_c, k_gain_c, q_gain_m)
    parts_b, kb, vb = _odd_branches(x_big.shape, p_b, None, None, *mem_big, layer, n_layers, *kv_big, *gains)
    parts_s, ks, vs = _odd_branches(x_small.shape, p_s, cache_k, cache_v, *mem_small, layer, n_layers, *kv_small,
                                    *gains)
    y_b, y_s = _out_proj(x_big, x_small, parts_b, parts_s, w_out, layer)
    return y_b, y_s, (kb, vb), (ks, vs)


def mem_project(mem, g, w_kv_stack, layer, k_gain):
    bn, n_mem, d = mem.shape
    m = bn * n_mem
    w_m = H_M * HEAD
    hm = rms_cast(mem.reshape(m, d), g, _row_tile(m, 256))
    kv = matmul([hm], w_kv_stack, layer, 0, 2 * w_m, _row_tile(m, 512), w_m)
    k = headnorm(kv, 0, w_m, k_gain, _row_tile(m, 512), w_m)
    return k.reshape(bn, n_mem, H_M, HEAD), kv[:, w_m:].reshape(bn, n_mem, H_M, HEAD)


def _cast_kernel(x_ref, o_ref):
    o_ref[...] = x_ref[...].astype(o_ref.dtype)


def cast_bf16(w, n_size, tk, tn):
    n_l, k, _ = w.shape
    assert k % tk == 0 and n_size % tn == 0
    return pl.pallas_call(
        _cast_kernel,
        grid=(n_l, k // tk, n_size // tn),
        in_specs=[pl.BlockSpec((None, tk, tn), lambda l, i, j: (l, i, j))],
        out_specs=pl.BlockSpec((None, tk, tn), lambda l, i, j: (l, i, j)),
        out_shape=jax.ShapeDtypeStruct((n_l, k, n_size), BF16),
        compiler_params=_cparams(3),
        name="cast_bf16",
    )(w)


def _prep_even_weights(w_in, n_heads):
    cut = 4 * n_heads * HEAD
    w_a = cast_bf16(w_in, cut, 1024, 1024)
    w_b = w_in[:, :, cut + 2 * n_heads:].astype(BF16)
    w_small = jnp.pad(w_in[:, :, cut:cut + 2 * n_heads], ((0, 0), (0, 0), (0, HEAD - 2 * n_heads))).astype(BF16)
    return w_a, w_b, w_small


def kernel(x_prompt, x_sample, state_gdn, state_gdn_conv, state_sconv, cache_swa_k, cache_swa_v, cache_mem_k, cache_mem_v, mem_prompt, norm_even, w_in_even, conv_a, a_log, dt_bias, gdn_norm, conv_b, w_out_even, norm_odd, w_in_odd, q_norm_c, k_norm_c, w_out_odd, mem_norm, w_mem_kv, q_norm_m, k_norm_m):
    bp, t_p, d = x_prompt.shape
    bs, t_s, _ = x_sample.shape
    depth = mem_norm.shape[0]
    n_odd = norm_odd.shape[0]
    n_heads_a = a_log.shape[1]
    n_heads_c = cache_swa_k.shape[3]
    wb = cache_swa_k.shape[2]
    w_even_a, w_even_b, w_even_small = _prep_even_weights(w_in_even, n_heads_a)
    w_odd_main = cast_bf16(w_in_odd, w_in_odd.shape[2], 1024, 1024)
    w_out_e = cast_bf16(w_out_even, d, 512, 2048)
    w_out_o = cast_bf16(w_out_odd, d, 512, 2048)
    w_kv = cast_bf16(w_mem_kv, w_mem_kv.shape[2], 1024, 1024)
    cache_k = cache_swa_k.reshape(n_odd, bs, wb * n_heads_c, HEAD)
    cache_v = cache_swa_v.reshape(n_odd, bs, wb * n_heads_c, HEAD)
    y_p, y_s = x_prompt, x_sample
    gdn_p, gdn_s, gconv_p, gconv_s, sconv_p, sconv_s = [], [], [], [], [], []
    memk_p, memv_p = [], []
    kv_p = tuple(jnp.zeros((n_odd, bp * t_p, d), F32) for _ in range(2))
    kv_s = tuple(jnp.zeros((n_odd, bs * t_s, d), F32) for _ in range(2))
    for i in range(depth):
        mk_p, mv_p = mem_project(mem_prompt, mem_norm[i], w_kv, i, k_norm_m[i])
        memk_p.append(mk_p)
        memv_p.append(mv_p)
        if i % 2 == 0:
            e = i // 2
            states_p = (jnp.zeros((bp,) + state_gdn_conv.shape[2:], state_gdn_conv.dtype),
                        jnp.zeros((bp,) + state_gdn.shape[2:], state_gdn.dtype),
                        jnp.zeros((bp,) + state_sconv.shape[2:], state_sconv.dtype), mk_p, mv_p)
            states_s = (state_gdn_conv[e], state_gdn[e], state_sconv[e], cache_mem_k[i], cache_mem_v[i])
            y_p, y_s, (ca_p, sa_p, cb_p), (ca_s, sa_s, cb_s) = even_layer(
                y_p, y_s, states_p, states_s, e, norm_even[e], w_even_a, w_even_b, w_even_small, conv_a[e], a_log[e],
                dt_bias[e], gdn_norm[e], conv_b[e], w_out_e, q_norm_m[i])
            gconv_p.append(ca_p)
            gdn_p.append(sa_p)
            sconv_p.append(cb_p)
            gconv_s.append(ca_s)
            gdn_s.append(sa_s)
            sconv_s.append(cb_s)
        else:
            o = i // 2
            y_p, y_s, kv_p, kv_s = odd_layer(
                y_p, y_s, cache_k, cache_v, (mk_p, mv_p), (cache_mem_k[i], cache_mem_v[i]), o, n_odd, kv_p, kv_s,
                norm_odd[o], w_odd_main, q_norm_c[o], k_norm_c[o], w_out_o, q_norm_m[i])
    (kp, vp), (ks, vs) = kv_p, kv_s
    n_keep = min(max(w for w, _ in SWA_PATTERNS), t_p)
    swa_k_p = kp.reshape(n_odd, bp, t_p, n_heads_c, HEAD)[:, :, t_p - n_keep:]
    swa_v_p = vp.reshape(n_odd, bp, t_p, n_heads_c, HEAD)[:, :, t_p - n_keep:]
    swa_k_s, swa_v_s = cache_roll(cache_k, cache_v, ks.reshape(n_odd, bs, t_s, d), vs.reshape(n_odd, bs, t_s, d),
                                  n_heads_c)
    swa_k_s = swa_k_s.reshape(n_odd, bs, wb, n_heads_c, HEAD)
    swa_v_s = swa_v_s.reshape(n_odd, bs, wb, n_heads_c, HEAD)
    return (y_p, y_s, jnp.stack(gdn_p), jnp.stack(gdn_s), jnp.stack(gconv_p), jnp.stack(gconv_s),
            jnp.stack(sconv_p), jnp.stack(sconv_s), swa_k_p, swa_k_s, swa_v_p, swa_v_s,
            jnp.stack(memk_p), jnp.stack(memv_p))
```

```python
import functools
import math

import jax
import jax.numpy as jnp
from jax import lax
from jax.experimental import pallas as pl
from jax.experimental.pallas import tpu as pltpu

F32 = jnp.float32
BF16 = jnp.bfloat16
EPS = 1e-6
HEAD = 128
SUBLANES = 8
CONV_A = 4
CONV_B = 3
GDN_CHUNK = 64
SWA_PATTERNS = ((128, 1), (512, 4), (2048, 16))
SWA_BLOCK = 128
H_M = 4
VMEM_LIMIT = 48 * 1024 * 1024


def _cparams(n_axes):
    return pltpu.CompilerParams(dimension_semantics=("arbitrary",) * n_axes,
                                vmem_limit_bytes=VMEM_LIMIT)


def _silu(x):
    return x * jax.nn.sigmoid(x)


def _dot(a, b, precision=None):
    return jnp.dot(a, b, preferred_element_type=F32, precision=precision)


def _dot_nt(a, b, precision=None):
    return lax.dot_general(a, b, (((1,), (1,)), ((), ())), preferred_element_type=F32,
                           precision=precision)


def _dot_tn(a, b, precision=None):
    return lax.dot_general(a, b, (((0,), (0,)), ((), ())), preferred_element_type=F32,
                           precision=precision)


def _rms_cast_kernel(x_ref, g_ref, o_ref):
    x = x_ref[...]
    ms = jnp.mean(x * x, axis=-1, keepdims=True)
    o_ref[...] = (x * lax.rsqrt(ms + EPS) * g_ref[...]).astype(o_ref.dtype)


def rms_cast(x2d, g, tm):
    m, d = x2d.shape
    return pl.pallas_call(
        _rms_cast_kernel,
        grid=(m // tm,),
        in_specs=[pl.BlockSpec((tm, d), lambda i: (i, 0)),
                  pl.BlockSpec((1, d), lambda i: (0, 0))],
        out_specs=pl.BlockSpec((tm, d), lambda i: (i, 0)),
        out_shape=jax.ShapeDtypeStruct((m, d), BF16),
        compiler_params=_cparams(1),
        name="rms_cast",
    )(x2d, g.reshape(1, d))


def _mm_kernel(*refs, n_in, has_res, has_side):
    n_out = 2 if has_side else 1
    ins, outs = refs[:-n_out], refs[-n_out:]
    w_refs = ins[n_in:2 * n_in]
    per_group = n_in + (1 if has_res else 0)

    def run(a_refs, res_ref, o_ref):
        acc = _dot(a_refs[0][...], w_refs[0][...])
        for a_ref, w_ref in zip(a_refs[1:], w_refs[1:]):
            acc = acc + _dot(a_ref[...], w_ref[...])
        if res_ref is not None:
            acc = acc + res_ref[...]
        o_ref[...] = acc.astype(o_ref.dtype)

    run(ins[:n_in], ins[2 * n_in] if has_res else None, outs[0])
    if has_side:
        side = ins[n_in + per_group:]

        @pl.when(pl.program_id(1) == 0)
        def _():
            run(side[:n_in], side[n_in] if has_res else None, outs[1])


def matmul(a_list, w, layer, n_off, n_size, tm, tn, res=None, side_a=None, side_res=None):
    m = a_list[0].shape[0]
    n_in = len(a_list)
    has_side = side_a is not None
    assert (side_res is not None) == (has_side and res is not None)
    in_specs, w_specs = [], []
    row = 0
    for a in a_list:
        k = a.shape[1]
        assert row % k == 0 and n_off % tn == 0 and n_size % tn == 0 and m % tm == 0
        in_specs.append(pl.BlockSpec((tm, k), lambda n, i: (i, 0)))
        w_specs.append(pl.BlockSpec((None, k, tn), functools.partial(
            lambda n, i, rb, nb: (layer, rb, n + nb), rb=row // k, nb=n_off // tn)))
        row += k
    args = list(a_list) + [w] * n_in
    specs = in_specs + w_specs
    out_specs = [pl.BlockSpec((tm, tn), lambda n, i: (i, n))]
    out_shape = [jax.ShapeDtypeStruct((m, n_size), F32)]
    if res is not None:
        specs.append(pl.BlockSpec((tm, tn), lambda n, i: (i, n)))
        args.append(res)
    if has_side:
        ms = side_a[0].shape[0]
        for a in side_a:
            specs.append(pl.BlockSpec((ms, a.shape[1]), lambda n, i: (0, 0)))
            args.append(a)
        if side_res is not None:
            specs.append(pl.BlockSpec((ms, tn), lambda n, i: (0, n)))
            args.append(side_res)
        out_specs.append(pl.BlockSpec((ms, tn), lambda n, i: (0, n)))
        out_shape.append(jax.ShapeDtypeStruct((ms, n_size), F32))
    out = pl.pallas_call(
        functools.partial(_mm_kernel, n_in=n_in, has_res=res is not None, has_side=has_side),
        grid=(n_size // tn, m // tm),
        in_specs=specs,
        out_specs=out_specs,
        out_shape=out_shape,
        compiler_params=_cparams(2),
        name="matmul",
    )(*args)
    return tuple(out) if has_side else out[0]


def _headnorm_kernel(x_ref, g_ref, o_ref, *, heads):
    g = g_ref[...]
    for h in range(heads):
        cols = slice(h * HEAD, (h + 1) * HEAD)
        x = x_ref[:, cols]
        ms = jnp.mean(x * x, axis=-1, keepdims=True)
        o_ref[:, cols] = x * lax.rsqrt(ms + EPS) * g


def headnorm(x2d, col_off, width, gain, tm, cw):
    m = x2d.shape[0]
    assert col_off % cw == 0 and width % cw == 0 and m % tm == 0
    return pl.pallas_call(
        functools.partial(_headnorm_kernel, heads=cw // HEAD),
        grid=(m // tm, width // cw),
        in_specs=[pl.BlockSpec((tm, cw), lambda i, c: (i, c + col_off // cw)),
                  pl.BlockSpec((1, HEAD), lambda i, c: (0, 0))],
        out_specs=pl.BlockSpec((tm, cw), lambda i, c: (i, c)),
        out_shape=jax.ShapeDtypeStruct((m, width), F32),
        compiler_params=_cparams(2),
        name="headnorm",
    )(x2d, gain.reshape(1, HEAD))


def _odd_qkv_kernel(q_ref, k_ref, v_ref, qg_ref, kg_ref, k_prev_hbm, v_prev_hbm, qo_ref, ko_ref, vo_ref, *, heads):
    del k_prev_hbm, v_prev_hbm
    qg = qg_ref[...]
    kg = kg_ref[...]
    for h in range(heads):
        cols = slice(h * HEAD, (h + 1) * HEAD)
        q = q_ref[:, cols]
        qo_ref[:, cols] = q * lax.rsqrt(jnp.mean(q * q, axis=-1, keepdims=True) + EPS) * qg
        k = k_ref[:, cols]
        ko_ref[:, cols] = k * lax.rsqrt(jnp.mean(k * k, axis=-1, keepdims=True) + EPS) * kg
    vo_ref[...] = v_ref[...]


def odd_qkv(proj, w_c, q_gain, k_gain, layer, n_layers, k_prev, v_prev, tm, cw):
    m = proj.shape[0]
    assert w_c % cw == 0 and m % tm == 0 and k_prev.shape == v_prev.shape == (n_layers, m, w_c)
    nb = w_c // cw
    blk = (tm, cw)
    any_spec = pl.BlockSpec(memory_space=pl.ANY)
    stack = jax.ShapeDtypeStruct((n_layers, m, w_c), F32)
    return pl.pallas_call(
        functools.partial(_odd_qkv_kernel, heads=cw // HEAD),
        grid=(m // tm, nb),
        in_specs=[pl.BlockSpec(blk, lambda i, c: (i, c)),
                  pl.BlockSpec(blk, lambda i, c: (i, c + nb)),
                  pl.BlockSpec(blk, lambda i, c: (i, c + 2 * nb)),
                  pl.BlockSpec((1, HEAD), lambda i, c: (0, 0)),
                  pl.BlockSpec((1, HEAD), lambda i, c: (0, 0)),
                  any_spec, any_spec],
        out_specs=[pl.BlockSpec(blk, lambda i, c: (i, c)),
                   pl.BlockSpec((None,) + blk, lambda i, c: (layer, i, c)),
                   pl.BlockSpec((None,) + blk, lambda i, c: (layer, i, c))],
        out_shape=[jax.ShapeDtypeStruct((m, w_c), F32), stack, stack],
        input_output_aliases={5: 1, 6: 2},
        compiler_params=_cparams(2),
        name="odd_qkv",
    )(proj, proj, proj, q_gain.reshape(1, HEAD), k_gain.reshape(1, HEAD), k_prev, v_prev)


def _gate_kernel(x_ref, alog_ref, dt_ref, beta_ref, gam_ref, gsm_ref, tri_ref, eb_ref, eg_ref, *, n_heads, cs):
    tm = x_ref.shape[0]
    wide = n_heads * HEAD

    @pl.when(pl.program_id(0) == 0)
    def _():
        ri = lax.broadcasted_iota(jnp.int32, (tm, tm), 0)
        ci = lax.broadcasted_iota(jnp.int32, (tm, tm), 1)
        tri_ref[...] = jnp.where((ci <= ri) & (ci >= (ri // cs) * cs), 1.0, 0.0).astype(BF16)
        src = lax.broadcasted_iota(jnp.int32, (HEAD, wide), 0)
        dst = lax.broadcasted_iota(jnp.int32, (HEAD, wide), 1) // HEAD
        eb_ref[...] = jnp.where(src == dst, 1.0, 0.0).astype(BF16)
        eg_ref[...] = jnp.where(src == dst + n_heads, 1.0, 0.0).astype(BF16)

    def parts(v):
        p0 = v.astype(BF16)
        r1 = v - p0.astype(F32)
        p1 = r1.astype(BF16)
        return p0, p1, (r1 - p1.astype(F32)).astype(BF16)

    x = x_ref[...]
    beta = jax.nn.sigmoid(x)
    g = -jnp.exp(alog_ref[...]) * jax.nn.softplus(x + dt_ref[...])
    tri = tri_ref[...]
    gam = functools.reduce(lambda a, b: a + b, [_dot(tri, p) for p in reversed(parts(g))])
    gsm_ref[...] = gam
    beta_ref[...] = functools.reduce(lambda a, b: a + b, [_dot(p, eb_ref[...]) for p in reversed(parts(beta))])
    gam_ref[...] = functools.reduce(lambda a, b: a + b, [_dot(p, eg_ref[...]) for p in reversed(parts(gam))])


def gdn_gates(small2d, a_log, dt_bias, tm, cs):
    m = small2d.shape[0]
    n_heads = a_log.shape[0]
    assert tm % cs == 0 and m % tm == 0
    pad = HEAD - 2 * n_heads
    alog_row = jnp.concatenate([jnp.zeros((n_heads,), F32), a_log, jnp.zeros((pad,), F32)]).reshape(1, HEAD)
    dt_row = jnp.concatenate([jnp.zeros((n_heads,), F32), dt_bias, jnp.zeros((pad,), F32)]).reshape(1, HEAD)
    wide = n_heads * HEAD
    return pl.pallas_call(
        functools.partial(_gate_kernel, n_heads=n_heads, cs=cs),
        grid=(m // tm,),
        in_specs=[pl.BlockSpec((tm, HEAD), lambda i: (i, 0)),
                  pl.BlockSpec((1, HEAD), lambda i: (0, 0)),
                  pl.BlockSpec((1, HEAD), lambda i: (0, 0))],
        out_specs=[pl.BlockSpec((tm, wide), lambda i: (i, 0)),
                   pl.BlockSpec((tm, wide), lambda i: (i, 0)),
                   pl.BlockSpec((tm, HEAD), lambda i: (i, 0))],
        out_shape=[jax.ShapeDtypeStruct((m, wide), F32), jax.ShapeDtypeStruct((m, wide), F32),
                   jax.ShapeDtypeStruct((m, HEAD), F32)],
        scratch_shapes=[pltpu.VMEM((tm, tm), BF16), pltpu.VMEM((HEAD, wide), BF16), pltpu.VMEM((HEAD, wide), BF16)],
        compiler_params=_cparams(1),
        name="gdn_gates",
    )(small2d, alog_row, dt_row)


def _sconv_kernel(bg_ref, cg_ref, xb_ref, zb_ref, hc_ref, hx_ref, st_ref, w_ref, o_ref, nb_ref, buf_ref, *, tt):
    t = pl.program_id(2)
    buf_ref[0:SUBLANES, :] = jnp.where(t == 0, st_ref[0], hc_ref[0] * hx_ref[0])
    buf_ref[SUBLANES:SUBLANES + tt, :] = cg_ref[0] * xb_ref[0]
    w = w_ref[...]
    first = SUBLANES - (CONV_B - 1)
    y = buf_ref[first:first + tt, :] * w[0:1, :]
    for j in range(1, CONV_B):
        y = y + buf_ref[first + j:first + j + tt, :] * w[j:j + 1, :]
    o_ref[0] = (bg_ref[0] * y * _silu(zb_ref[0])).astype(o_ref.dtype)
    nb_ref[0] = buf_ref[tt:tt + SUBLANES, :]


def short_conv(proj3d, off_b, off_c, off_x, off_z, width, state_pad, conv_w, tt, cw):
    bn, t, _ = proj3d.shape
    assert t % tt == 0 and width % cw == 0
    rb = tt // SUBLANES

    def col(off):
        assert off % cw == 0
        return functools.partial(lambda b, c, i, o: (b, i, c + o), o=off // cw)

    def halo(off):
        return functools.partial(lambda b, c, i, o: (b, jnp.maximum(i * rb - 1, 0), c + o), o=off // cw)

    blk = (1, tt, cw)
    hblk = (1, SUBLANES, cw)
    return pl.pallas_call(
        functools.partial(_sconv_kernel, tt=tt),
        grid=(bn, width // cw, t // tt),
        in_specs=[pl.BlockSpec(blk, col(off_b)), pl.BlockSpec(blk, col(off_c)),
                  pl.BlockSpec(blk, col(off_x)), pl.BlockSpec(blk, col(off_z)),
                  pl.BlockSpec(hblk, halo(off_c)), pl.BlockSpec(hblk, halo(off_x)),
                  pl.BlockSpec(hblk, lambda b, c, i: (b, 0, c)),
                  pl.BlockSpec((CONV_B, cw), lambda b, c, i: (0, c))],
        out_specs=[pl.BlockSpec(blk, lambda b, c, i: (b, i, c)),
                   pl.BlockSpec(hblk, lambda b, c, i: (b, 0, c))],
        out_shape=[jax.ShapeDtypeStruct((bn, t, width), BF16),
                   jax.ShapeDtypeStruct((bn, SUBLANES, width), F32)],
        scratch_shapes=[pltpu.VMEM((tt + SUBLANES, cw), F32)],
        compiler_params=_cparams(3),
        name="short_conv",
    )(proj3d, proj3d, proj3d, proj3d, proj3d, proj3d, state_pad, conv_w)


def _split_bf16(x):
    hi = x.astype(BF16)
    return hi, (x - hi.astype(F32)).astype(BF16)


def _dot_3pass(a, b):
    a_hi, a_lo = _split_bf16(a)
    b_hi, b_lo = _split_bf16(b)
    return _dot(a_hi, b_hi) + (_dot(a_hi, b_lo) + _dot(a_lo, b_hi))


def _gdn_kernel(xq_ref, xk_ref, xv_ref, wq_ref, wk_ref, wv_ref, cq_ref, ck_ref, cv_ref, z_ref, beta_ref, gam_ref,
                grow_ref, s0_ref, gain_ref, o_ref, s_ref, xbuf, act, carry, *, hb, cs, nct):
    ti = pl.program_id(2)
    tt = nct * cs

    @pl.when(ti == 0)
    def _():
        s_ref[...] = s0_ref[...]
        for a, c_ref in enumerate((cq_ref, ck_ref, cv_ref)):
            carry[a] = c_ref[0]

    first = SUBLANES - (CONV_A - 1)
    for a, (x_ref, w_ref) in enumerate(((xq_ref, wq_ref), (xk_ref, wk_ref), (xv_ref, wv_ref))):
        xbuf[a, 0:SUBLANES, :] = carry[a]
        xbuf[a, SUBLANES:SUBLANES + tt, :] = x_ref[0]
        carry[a] = x_ref[0, tt - SUBLANES:tt, :]
        w = w_ref[...]
        y = xbuf[a, first:first + tt, :] * w[0:1, :]
        for j in range(1, CONV_A):
            y = y + xbuf[a, first + j:first + j + tt, :] * w[j:j + 1, :]
        y = _silu(y)
        if a == 2:
            act[a] = y
        else:
            scale = HEAD ** -0.5 if a == 0 else 1.0
            for h in range(hb):
                yh = y[:, h * HEAD:(h + 1) * HEAD]
                ss = jnp.sum(yh * yh, axis=-1, keepdims=True)
                act[a, :, h * HEAD:(h + 1) * HEAD] = yh * lax.rsqrt(ss + EPS) * scale

    ii = lax.broadcasted_iota(jnp.int32, (cs, cs), 0)
    jj = lax.broadcasted_iota(jnp.int32, (cs, cs), 1)
    causal = ii >= jj
    strict = ii > jj
    eye_f = (ii == jj).astype(F32)
    gain = gain_ref[...]
    n_double = int(math.log2(cs)) - 1
    chains = [(j, h) for j in range(nct) for h in range(hb)]

    def tile(ref, c):
        return ref[0, c[0] * cs:(c[0] + 1) * cs, c[1] * HEAD:(c[1] + 1) * HEAD]

    def act_tile(a, c):
        return act[a, c[0] * cs:(c[0] + 1) * cs, c[1] * HEAD:(c[1] + 1) * HEAD]

    k = [act_tile(1, c) for c in chains]
    beta = [tile(beta_ref, c) for c in chains]
    gam = [tile(gam_ref, c) for c in chains]
    kb = [a * b for a, b in zip(k, beta)]
    raw = [_dot_nt(jnp.concatenate([b, act_tile(0, c)], axis=0).astype(BF16), a.astype(BF16))
           for a, b, c in zip(k, kb, chains)]
    decay = []
    for g, (j, h) in zip(gam, chains):
        gam_row = grow_ref[0, h, pl.ds(ti * nct + j, 1), :]
        decay.append(jnp.where(causal, jnp.exp(jnp.where(causal, g[:, :cs] - gam_row, 0.0)), 0.0))
    qk = [jnp.where(causal, r[cs:] * d, 0.0) for r, d in zip(raw, decay)]
    pw = [jnp.where(strict, -(r[:cs] * d), 0.0) for r, d in zip(raw, decay)]
    inv = [eye_f + p for p in pw]
    pw = [_dot(p.astype(BF16), p.astype(BF16)) for p in pw]
    for _ in range(n_double - 1):
        prod = [_dot(jnp.concatenate([p, x], axis=0).astype(BF16), p.astype(BF16)) for p, x in zip(pw, inv)]
        pw = [p[:cs] for p in prod]
        inv = [x + p[cs:] for x, p in zip(inv, prod)]
    inv = [x + _dot(x.astype(BF16), p.astype(BF16)) for x, p in zip(inv, pw)]
    e_gam = [jnp.exp(g) for g in gam]
    uw = [_dot_3pass(x, jnp.concatenate([act_tile(2, c) * b, a * e], axis=1))
          for x, c, b, a, e in zip(inv, chains, beta, kb, e_gam)]
    wq = [jnp.concatenate([x[:, HEAD:], act_tile(0, c) * e], axis=0).astype(BF16)
          for x, c, e in zip(uw, chains, e_gam)]
    k_dec_t = [(a * jnp.exp(g[cs - 1:cs, :] - g)).T.astype(BF16) for a, g in zip(k, gam)]
    state = [s_ref[0, h] for h in range(hb)]
    for j in range(nct):
        idx = [j * hb + h for h in range(hb)]
        ws = [_dot(wq[i], state[h].astype(BF16)) for h, i in enumerate(idx)]
        v16 = [(uw[i][:, :HEAD] - x[:cs]).astype(BF16) for x, i in zip(ws, idx)]
        o = [x[cs:] + _dot(qk[i].astype(BF16), y) for x, y, i in zip(ws, v16, idx)]
        state = [s * jnp.exp(gam[i][cs - 1:cs, :]) + _dot(k_dec_t[i], y) for s, y, i in zip(state, v16, idx)]
        for h, i in enumerate(idx):
            ms = jnp.mean(o[h] * o[h], axis=-1, keepdims=True)
            on = o[h] * lax.rsqrt(ms + EPS) * gain
            o_ref[0, j * cs:(j + 1) * cs, h * HEAD:(h + 1) * HEAD] = (
                on * _silu(tile(z_ref, (j, h)))).astype(o_ref.dtype)
    for h in range(hb):
        s_ref[0, h] = state[h]


def gdn(proj3d, z_off, conv_w, conv_state, beta_rep, gam_rep, gam_row4, s0, gain, hb, tt):
    bn, t, _ = proj3d.shape
    n_heads = s0.shape[1]
    nc, cs = gam_row4.shape[2], gam_row4.shape[3]
    cw = hb * HEAD
    ng = n_heads // hb
    assert z_off % cw == 0 and n_heads % hb == 0 and t % tt == 0 and tt % cs == 0 and tt % SUBLANES == 0
    blk = (1, tt, cw)
    sblk = (1, hb, HEAD, HEAD)
    qkv_specs = [pl.BlockSpec(blk, functools.partial(lambda b, h, i, a: (b, i, a * ng + h), a=a)) for a in range(3)]
    w_specs = [pl.BlockSpec((CONV_A, cw), functools.partial(lambda b, h, i, a: (0, a * ng + h), a=a)) for a in range(3)]
    c_specs = [pl.BlockSpec((1, SUBLANES, cw), functools.partial(lambda b, h, i, a: (b, 0, a * ng + h), a=a))
               for a in range(3)]
    return pl.pallas_call(
        functools.partial(_gdn_kernel, hb=hb, cs=cs, nct=tt // cs),
        grid=(bn, ng, t // tt),
        in_specs=qkv_specs + w_specs + c_specs + [
                  pl.BlockSpec(blk, lambda b, h, i: (b, i, z_off // cw + h)),
                  pl.BlockSpec(blk, lambda b, h, i: (b, i, h)),
                  pl.BlockSpec(blk, lambda b, h, i: (b, i, h)),
                  pl.BlockSpec((1, hb, nc, cs), lambda b, h, i: (b, h, 0, 0)),
                  pl.BlockSpec(sblk, lambda b, h, i: (b, h, 0, 0)),
                  pl.BlockSpec((1, HEAD), lambda b, h, i: (0, 0))],
        out_specs=[pl.BlockSpec(blk, lambda b, h, i: (b, i, h)),
                   pl.BlockSpec(sblk, lambda b, h, i: (b, h, 0, 0))],
        out_shape=[jax.ShapeDtypeStruct((bn, t, n_heads * HEAD), BF16),
                   jax.ShapeDtypeStruct((bn, n_heads, HEAD, HEAD), F32)],
        scratch_shapes=[pltpu.VMEM((3, tt + SUBLANES, cw), F32), pltpu.VMEM((3, tt, cw), F32),
                        pltpu.VMEM((3, SUBLANES, cw), F32)],
        compiler_params=_cparams(3),
        name="gdn",
    )(proj3d, proj3d, proj3d, conv_w, conv_w, conv_w, conv_state, conv_state, conv_state, proj3d, beta_rep,
      gam_rep, gam_row4, s0, gain.reshape(1, HEAD))


def _mem_attn_kernel(q_ref, z_ref, mk_ref, mv_ref, gain_ref, o_ref):
    gain = gain_ref[...]
    for h in range(H_M):
        cols = slice(h * HEAD, (h + 1) * HEAD)
        q = q_ref[0, :, cols]
        ms = jnp.mean(q * q, axis=-1, keepdims=True)
        qn = (q * lax.rsqrt(ms + EPS) * gain).astype(BF16)
        sc = _dot_nt(qn, mk_ref[0, :, cols].astype(BF16)) * (HEAD ** -0.5)
        m = jnp.max(sc, axis=-1, keepdims=True)
        p = jnp.exp(sc - m)
        p = p / jnp.sum(p, axis=-1, keepdims=True)
        o = _dot(p.astype(BF16), mv_ref[0, :, cols].astype(BF16))
        o_ref[0, :, cols] = (o * _silu(z_ref[0, :, cols])).astype(o_ref.dtype)


def mem_attn(proj3d, q_off, z_off, mem_k, mem_v, q_gain, tt):
    bn, t, _ = proj3d.shape
    n_mem = mem_k.shape[1]
    wm = H_M * HEAD
    assert q_off % wm == 0 and z_off % wm == 0 and t % tt == 0
    return pl.pallas_call(
        _mem_attn_kernel,
        grid=(bn, t // tt),
        in_specs=[pl.BlockSpec((1, tt, wm), lambda b, i: (b, i, q_off // wm)),
                  pl.BlockSpec((1, tt, wm), lambda b, i: (b, i, z_off // wm)),
                  pl.BlockSpec((1, n_mem, wm), lambda b, i: (b, 0, 0)),
                  pl.BlockSpec((1, n_mem, wm), lambda b, i: (b, 0, 0)),
                  pl.BlockSpec((1, HEAD), lambda b, i: (0, 0))],
        out_specs=pl.BlockSpec((1, tt, wm), lambda b, i: (b, i, 0)),
        out_shape=jax.ShapeDtypeStruct((bn, t, wm), BF16),
        compiler_params=_cparams(2),
        name="mem_attn",
    )(proj3d, proj3d, mem_k.reshape(bn, n_mem, wm), mem_v.reshape(bn, n_mem, wm), q_gain.reshape(1, HEAD))


def _dil_prompt_kernel(q_ref, k_ref, v_ref, z_ref, o_ref, acc_ref, lse_ref, *, hg, t, group):
    qi = lax.broadcasted_iota(jnp.int32, (SWA_BLOCK, SWA_BLOCK), 0)
    ki = lax.broadcasted_iota(jnp.int32, (SWA_BLOCK, SWA_BLOCK), 1)
    n_pat = len(SWA_PATTERNS)
    for h in range(hg):
        cols = slice(h * HEAD, (h + 1) * HEAD)
        for p, (window, dil) in enumerate(SWA_PATTERNS):
            assert window // dil == SWA_BLOCK
            l_sub = t // dil
            nb = l_sub // SWA_BLOCK
            assert nb * SWA_BLOCK * dil == t

            def rows_of(r, c, dil=dil):
                return pl.ds(r + c * (SWA_BLOCK * dil), SWA_BLOCK, stride=dil)

            blocks = [(r, c) for r in range(dil) for c in range(nb)]
            for g0 in range(0, len(blocks), group):
                grp = blocks[g0:g0 + group]
                q = [(q_ref[0, rows_of(r, c), cols] * (HEAD ** -0.5)).astype(BF16) for r, c in grp]
                s_cur = [jnp.where(ki <= qi, _dot_nt(x, k_ref[0, rows_of(r, c), cols].astype(BF16)), -jnp.inf)
                         for x, (r, c) in zip(q, grp)]
                s_prev = [jnp.where(ki >= qi, _dot_nt(x, k_ref[0, rows_of(r, c - 1), cols].astype(BF16)), -jnp.inf)
                          if c > 0 else None for x, (r, c) in zip(q, grp)]
                m = [jnp.max(a, axis=-1, keepdims=True) if b is None else
                     jnp.maximum(jnp.max(a, axis=-1, keepdims=True), jnp.max(b, axis=-1, keepdims=True))
                     for a, b in zip(s_cur, s_prev)]
                e_cur = [jnp.exp(a - x) for a, x in zip(s_cur, m)]
                e_prev = [None if b is None else jnp.exp(b - x) for b, x in zip(s_prev, m)]
                den = [jnp.sum(a, axis=-1, keepdims=True) if b is None else
                       jnp.sum(a, axis=-1, keepdims=True) + jnp.sum(b, axis=-1, keepdims=True)
                       for a, b in zip(e_cur, e_prev)]
                o = [_dot(a.astype(BF16), v_ref[0, rows_of(r, c), cols].astype(BF16)) for a, (r, c) in zip(e_cur, grp)]
                o = [x if b is None else x + _dot(b.astype(BF16), v_ref[0, rows_of(r, c - 1), cols].astype(BF16))
                     for x, b, (r, c) in zip(o, e_prev, grp)]
                for x, d, mx, (r, c) in zip(o, den, m, grp):
                    acc_ref[p, rows_of(r, c), :] = x / d
                    lse_ref[p, rows_of(r, c), :] = jnp.broadcast_to(mx + jnp.log(d), (SWA_BLOCK, HEAD))
        lse = [lse_ref[p] for p in range(n_pat)]
        top = functools.reduce(jnp.maximum, lse)
        wgt = [jnp.exp(x - top) for x in lse]
        tot = functools.reduce(lambda a, b: a + b, wgt)
        mix = functools.reduce(lambda a, b: a + b, [(wgt[p] / tot) * acc_ref[p] for p in range(n_pat)])
        o_ref[0, :, cols] = (mix * _silu(z_ref[0, :, cols])).astype(o_ref.dtype)


def dil_prompt(qn, k_stack, v_stack, layer, proj3d, z_off, hg):
    bn, t, wc = qn.shape
    cw = hg * HEAD
    assert z_off % cw == 0 and wc % cw == 0
    blk = (1, t, cw)
    sblk = (None, 1, t, cw)
    return pl.pallas_call(
        functools.partial(_dil_prompt_kernel, hg=hg, t=t, group=8),
        grid=(bn, wc // cw),
        in_specs=[pl.BlockSpec(blk, lambda b, h: (b, 0, h)),
                  pl.BlockSpec(sblk, lambda b, h: (layer, b, 0, h)),
                  pl.BlockSpec(sblk, lambda b, h: (layer, b, 0, h)),
                  pl.BlockSpec(blk, lambda b, h: (b, 0, z_off // cw + h))],
        out_specs=pl.BlockSpec(blk, lambda b, h: (b, 0, h)),
        out_shape=jax.ShapeDtypeStruct((bn, t, wc), BF16),
        scratch_shapes=[pltpu.VMEM((len(SWA_PATTERNS), t, HEAD), F32),
                        pltpu.VMEM((len(SWA_PATTERNS), t, HEAD), F32)],
        compiler_params=_cparams(2),
        name="dil_prompt",
    )(qn, k_stack, v_stack, proj3d)


MASKED_MAX = -1e30


def _dil_sample_kernel(q_ref, kn_ref, vn_ref, z_ref, kc_ref, vc_ref, o_ref, m_ref, l_ref, acc_ref,
                       *, n_heads, t, wb, tw, group):
    j = pl.program_id(1)
    n_pat = len(SWA_PATTERNS)

    @pl.when(j == 0)
    def _():
        m_ref[...] = jnp.full(m_ref.shape, MASKED_MAX, F32)
        l_ref[...] = jnp.zeros(l_ref.shape, F32)
        acc_ref[...] = jnp.zeros(acc_ref.shape, F32)

    def head_rows(ref, h):
        return ref[0, pl.ds(h, tw, stride=n_heads), :].astype(BF16)

    def q16(h):
        return (q_ref[0, :, h * HEAD:(h + 1) * HEAD] * (HEAD ** -0.5)).astype(BF16)

    def update(hs, p, scores, ok, values):
        idx = [h * n_pat + p for h in hs]
        s = [jnp.where(ok, x, -jnp.inf) for x in scores]
        m_old = [m_ref[i] for i in idx]
        m_new = [jnp.maximum(a, jnp.max(x, axis=-1, keepdims=True)) for a, x in zip(m_old, s)]
        alpha = [jnp.exp(a - b) for a, b in zip(m_old, m_new)]
        e = [jnp.exp(x - b[:, :1]) for x, b in zip(s, m_new)]
        pv = [_dot(x.astype(BF16), v) for x, v in zip(e, values)]
        for i, a, x, y, b in zip(idx, alpha, e, pv, m_new):
            l_ref[i] = l_ref[i] * a + jnp.sum(x, axis=-1, keepdims=True)
            acc_ref[i] = acc_ref[i] * a + y
            m_ref[i] = b

    row = lax.broadcasted_iota(jnp.int32, (t, tw), 0)
    col = lax.broadcasted_iota(jnp.int32, (t, tw), 1)
    delta = wb + row - (j * tw + col)
    nearest = wb - (j + 1) * tw + 1
    for p, (window, dil) in enumerate(SWA_PATTERNS):
        def tile_update(p=p, window=window, dil=dil):
            ok = ((delta & (dil - 1)) == 0) & (delta <= window)
            for g0 in range(0, n_heads, group):
                hs = list(range(g0, g0 + group))
                scores = [_dot_nt(q16(h), head_rows(kc_ref, h)) for h in hs]
                update(hs, p, scores, ok, [head_rows(vc_ref, h) for h in hs])
        if window >= wb - tw + 1:
            tile_update()
        else:
            pl.when(nearest <= window)(tile_update)

    @pl.when(j == pl.num_programs(1) - 1)
    def _():
        dn = lax.broadcasted_iota(jnp.int32, (t, t), 0) - lax.broadcasted_iota(jnp.int32, (t, t), 1)
        for g0 in range(0, n_heads, group):
            hs = list(range(g0, g0 + group))
            cols = [slice(h * HEAD, (h + 1) * HEAD) for h in hs]
            scores = [_dot_nt(q16(h), kn_ref[0, :, c].astype(BF16)) for h, c in zip(hs, cols)]
            values = [vn_ref[0, :, c].astype(BF16) for c in cols]
            for p, (window, dil) in enumerate(SWA_PATTERNS):
                ok = (dn >= 0) & ((dn & (dil - 1)) == 0) & (dn <= window)
                update(hs, p, scores, ok, values)
            for h, c in zip(hs, cols):
                idx = [h * n_pat + p for p in range(n_pat)]
                lse = [m_ref[i] + jnp.log(l_ref[i]) for i in idx]
                top = functools.reduce(jnp.maximum, lse)
                wgt = [jnp.exp(x - top) for x in lse]
                tot = functools.reduce(lambda a, b: a + b, wgt)
                mix = functools.reduce(lambda a, b: a + b,
                                       [(w / tot) * (acc_ref[i] / l_ref[i]) for w, i in zip(wgt, idx)])
                o_ref[0, :, c] = (mix * _silu(z_ref[0, :, c])).astype(o_ref.dtype)


def dil_sample(qn, k_stack, v_stack, layer, proj3d, z_off, cache_k, cache_v, n_heads, tw):
    bn, t, wc = qn.shape
    wb = cache_k.shape[2] // n_heads
    assert z_off % wc == 0 and wb % tw == 0 and wc == n_heads * HEAD
    for window, dil in SWA_PATTERNS:
        assert dil & (dil - 1) == 0
    blk = (1, t, wc)
    sblk = (None, 1, t, wc)
    cblk = (None, 1, tw * n_heads, HEAD)
    state = pltpu.VMEM((n_heads * len(SWA_PATTERNS), t, HEAD), F32)
    return pl.pallas_call(
        functools.partial(_dil_sample_kernel, n_heads=n_heads, t=t, wb=wb, tw=tw, group=8),
        grid=(bn, wb // tw),
        in_specs=[pl.BlockSpec(blk, lambda b, j: (b, 0, 0)),
                  pl.BlockSpec(sblk, lambda b, j: (layer, b, 0, 0)),
                  pl.BlockSpec(sblk, lambda b, j: (layer, b, 0, 0)),
                  pl.BlockSpec(blk, lambda b, j: (b, 0, z_off // wc)),
                  pl.BlockSpec(cblk, lambda b, j: (layer, b, j, 0)),
                  pl.BlockSpec(cblk, lambda b, j: (layer, b, j, 0))],
        out_specs=pl.BlockSpec(blk, lambda b, j: (b, 0, 0)),
        out_shape=jax.ShapeDtypeStruct((bn, t, wc), BF16),
        scratch_shapes=[state, state, state],
        compiler_params=_cparams(2),
        name="dil_sample",
    )(qn, k_stack, v_stack, proj3d, cache_k, cache_v)


CACHE_ROLL_CHUNK_BYTES = 4 * 1024 * 1024


def _cache_roll_kernel(ck_hbm, cv_hbm, kn_ref, vn_ref, ok_hbm, ov_hbm, buf, stage, in_sems, out_sems, new_sems,
                       *, n_heads, t, chunk, n_chunks):
    l = pl.program_id(0)
    b = pl.program_id(1)
    new = t * n_heads
    srcs = (ck_hbm, cv_hbm)
    dsts = (ok_hbm, ov_hbm)

    def read(a, c):
        return pltpu.make_async_copy(srcs[a].at[l, b, pl.ds(new + c * chunk, chunk)], buf.at[a, c % 2],
                                     in_sems.at[a, c % 2])

    def write(a, c):
        return pltpu.make_async_copy(buf.at[a, c % 2], dsts[a].at[l, b, pl.ds(c * chunk, chunk)],
                                     out_sems.at[a, c % 2])

    def write_new(a):
        return pltpu.make_async_copy(stage.at[a], dsts[a].at[l, b, pl.ds(n_chunks * chunk, new)], new_sems.at[a])

    for a in range(2):
        read(a, 0).start()
    for h in range(n_heads):
        stage[0, pl.ds(h, t, stride=n_heads), :] = kn_ref[:, h * HEAD:(h + 1) * HEAD]
        stage[1, pl.ds(h, t, stride=n_heads), :] = vn_ref[:, h * HEAD:(h + 1) * HEAD]
    for a in range(2):
        write_new(a).start()
    for c in range(n_chunks):
        for a in range(2):
            read(a, c).wait()
            write(a, c).start()
            if c + 1 < n_chunks:
                if c >= 1:
                    write(a, c - 1).wait()
                read(a, c + 1).start()
    for a in range(2):
        if n_chunks >= 2:
            write(a, n_chunks - 2).wait()
        write(a, n_chunks - 1).wait()
        write_new(a).wait()


def cache_roll(cache_k, cache_v, k_new, v_new, n_heads):
    n_l, bn, rows, _ = cache_k.shape
    t = k_new.shape[2]
    wb = rows // n_heads
    keep_tokens = wb - t
    token_bytes = n_heads * HEAD * cache_k.dtype.itemsize
    chunk_tokens = max(d for d in range(1, keep_tokens + 1)
                       if keep_tokens % d == 0 and d * token_bytes <= CACHE_ROLL_CHUNK_BYTES)
    chunk = chunk_tokens * n_heads
    nblk = (None, None, t, n_heads * HEAD)
    any_spec = pl.BlockSpec(memory_space=pl.ANY)
    out = jax.ShapeDtypeStruct(cache_k.shape, cache_k.dtype)
    return pl.pallas_call(
        functools.partial(_cache_roll_kernel, n_heads=n_heads, t=t, chunk=chunk, n_chunks=keep_tokens // chunk_tokens),
        grid=(n_l, bn),
        in_specs=[any_spec, any_spec,
                  pl.BlockSpec(nblk, lambda l, b: (l, b, 0, 0)),
                  pl.BlockSpec(nblk, lambda l, b: (l, b, 0, 0))],
        out_specs=[any_spec, any_spec],
        out_shape=[out, out],
        scratch_shapes=[pltpu.VMEM((2, 2, chunk, HEAD), cache_k.dtype),
                        pltpu.VMEM((2, t * n_heads, HEAD), cache_k.dtype),
                        pltpu.SemaphoreType.DMA((2, 2)), pltpu.SemaphoreType.DMA((2, 2)),
                        pltpu.SemaphoreType.DMA((2,))],
        compiler_params=_cparams(2),
        name="cache_roll",
    )(cache_k, cache_v, k_new, v_new)


def _row_tile(m, pref):
    return pref if m % pref == 0 else m


def _flat(x):
    return x.reshape(x.shape[0] * x.shape[1], x.shape[2])


def _in_proj(x_big, x_small, g_norm, w_stacks, layer):
    xb, xs = _flat(x_big), _flat(x_small)
    hb = rms_cast(xb, g_norm, _row_tile(xb.shape[0], 256))
    hs = rms_cast(xs, g_norm, _row_tile(xs.shape[0], 256))
    tm = _row_tile(xb.shape[0], 512)
    return [matmul([hb], w, layer, 0, w.shape[2], tm, min(1024, w.shape[2]), side_a=[hs]) for w in w_stacks]


def _out_proj(x_big, x_small, parts_big, parts_small, w_out, layer):
    xb, xs = _flat(x_big), _flat(x_small)
    yb, ys = matmul([_flat(p) for p in parts_big], w_out, layer, 0, xb.shape[1], _row_tile(xb.shape[0], 512), 1024,
                    res=xb, side_a=[_flat(p) for p in parts_small], side_res=xs)
    return yb.reshape(x_big.shape), ys.reshape(x_small.shape)


def _pad_state(buf):
    return jnp.pad(buf, ((0, 0), (SUBLANES - buf.shape[1], 0), (0, 0)))


def _even_branches(shape, proj_a, proj_b, small, conv_a_buf, s0, conv_b_buf, mem_k, mem_v, conv_a_w, a_log, dt_bias,
                   gdn_gain, conv_b_w, q_gain_m):
    bn, t, _ = shape
    m = bn * t
    n_heads = a_log.shape[0]
    w_a = n_heads * HEAD
    w_b = conv_b_w.shape[1]
    w_m = H_M * HEAD
    pa = proj_a.reshape(bn, t, proj_a.shape[1])
    pb = proj_b.reshape(bn, t, proj_b.shape[1])
    tt = _row_tile(t, 512)
    cs = min(GDN_CHUNK, t)
    nc = t // cs
    beta_rep, gam_rep, gam_small = gdn_gates(small, a_log, dt_bias, _row_tile(m, 512), cs)
    gam_row4 = (gam_small.reshape(bn, t, HEAD)[:, :, n_heads:2 * n_heads]
                .transpose(0, 2, 1).reshape(bn, n_heads, nc, cs))
    o_a, new_s = gdn(pa, 3 * w_a, conv_a_w, _pad_state(conv_a_buf), beta_rep.reshape(bn, t, w_a),
                     gam_rep.reshape(bn, t, w_a), gam_row4, s0, gdn_gain, 4, _row_tile(t, 2 * GDN_CHUNK))
    new_conv_a = pa[:, t - (CONV_A - 1):, :3 * w_a]
    o_b, nb8 = short_conv(pb, 0, w_b, 2 * w_b, 3 * w_b, w_b, _pad_state(conv_b_buf), conv_b_w, tt, 512)
    new_conv_b = nb8[:, SUBLANES - (CONV_B - 1):, :]
    o_m = mem_attn(pb, 4 * w_b, 4 * w_b + w_m, mem_k, mem_v, q_gain_m, tt)
    return (o_a, o_b, o_m), (new_conv_a, new_s, new_conv_b)


def even_layer(x_big, x_small, states_big, states_small, layer, g_norm, w_a, w_b, w_small, conv_a_w, a_log, dt_bias,
               gdn_gain, conv_b_w, w_out, q_gain_m):
    (pa_b, pa_s), (pb_b, pb_s), (sm_b, sm_s) = _in_proj(x_big, x_small, g_norm, (w_a, w_b, w_small), layer)
    wts = (conv_a_w, a_log, dt_bias, gdn_gain, conv_b_w, q_gain_m)
    parts_b, new_b = _even_branches(x_big.shape, pa_b, pb_b, sm_b, *states_big, *wts)
    parts_s, new_s = _even_branches(x_small.shape, pa_s, pb_s, sm_s, *states_small, *wts)
    y_b, y_s = _out_proj(x_big, x_small, parts_b, parts_s, w_out, layer)
    return y_b, y_s, new_b, new_s


def _odd_branches(shape, proj, cache_k, cache_v, mem_k, mem_v, layer, n_layers, k_prev, v_prev, q_gain_c, k_gain_c,
                  q_gain_m):
    bn, t, w_c = shape
    m = bn * t
    w_m = H_M * HEAD
    off_z = 3 * w_c
    off_qm = 4 * w_c
    proj3d = proj.reshape(bn, t, proj.shape[1])
    qn, k_stack, v_stack = odd_qkv(proj, w_c, q_gain_c, k_gain_c, layer, n_layers, k_prev, v_prev,
                                   _row_tile(m, 512), 512)
    qn = qn.reshape(bn, t, w_c)
    k4 = k_stack.reshape(n_layers, bn, t, w_c)
    v4 = v_stack.reshape(n_layers, bn, t, w_c)
    if cache_k is None:
        o_c = dil_prompt(qn, k4, v4, layer, proj3d, off_z, 1)
    else:
        o_c = dil_sample(qn, k4, v4, layer, proj3d, off_z, cache_k, cache_v, w_c // HEAD, 512)
    o_m = mem_attn(proj3d, off_qm, off_qm + w_m, mem_k, mem_v, q_gain_m, _row_tile(t, 512))
    return (o_c, o_m), k_stack, v_stack


def odd_layer(x_big, x_small, cache_k, cache_v, mem_big, mem_small, layer, n_layers, kv_big, kv_small, g_norm, w_main,
              q_gain_c, k_gain_c, w_out, q_gain_m):
    ((p_b, p_s),) = _in_proj(x_big, x_small, g_norm, (w_main,), layer)
    gains = (q_gain_c, k_gain_c, q_gain_m)
    parts_b, kb, vb = _odd_branches(x_big.shape, p_b, None, None, *mem_big, layer, n_layers, *kv_big, *gains)
    parts_s, ks, vs = _odd_branches(x_small.shape, p_s, cache_k, cache_v, *mem_small, layer, n_layers, *kv_small,
                                    *gains)
    y_b, y_s = _out_proj(x_big, x_small, parts_b, parts_s, w_out, layer)
    return y_b, y_s, (kb, vb), (ks, vs)


def mem_project(mem, g, w_kv_stack, layer, k_gain):
    bn, n_mem, d = mem.shape
    m = bn * n_mem
    w_m = H_M * HEAD
    hm = rms_cast(mem.reshape(m, d), g, _row_tile(m, 256))
    kv = matmul([hm], w_kv_stack, layer, 0, 2 * w_m, _row_tile(m, 512), w_m)
    k = headnorm(kv, 0, w_m, k_gain, _row_tile(m, 512), w_m)
    return k.reshape(bn, n_mem, H_M, HEAD), kv[:, w_m:].reshape(bn, n_mem, H_M, HEAD)


def _cast_kernel(x_ref, o_ref):
    o_ref[...] = x_ref[...].astype(o_ref.dtype)


def cast_bf16(w, n_size, tk, tn):
    n_l, k, _ = w.shape
    assert k % tk == 0 and n_size % tn == 0
    return pl.pallas_call(
        _cast_kernel,
        grid=(n_l, k // tk, n_size // tn),
        in_specs=[pl.BlockSpec((None, tk, tn), lambda l, i, j: (l, i, j))],
        out_specs=pl.BlockSpec((None, tk, tn), lambda l, i, j: (l, i, j)),
        out_shape=jax.ShapeDtypeStruct((n_l, k, n_size), BF16),
        compiler_params=_cparams(3),
        name="cast_bf16",
    )(w)


def _prep_even_weights(w_in, n_heads):
    cut = 4 * n_heads * HEAD
    w_a = w_in[:, :, :cut].astype(BF16)
    w_b = w_in[:, :, cut + 2 * n_heads:].astype(BF16)
    w_small = jnp.pad(w_in[:, :, cut:cut + 2 * n_heads], ((0, 0), (0, 0), (0, HEAD - 2 * n_heads))).astype(BF16)
    return w_a, w_b, w_small


def kernel(x_prompt, x_sample, state_gdn, state_gdn_conv, state_sconv, cache_swa_k, cache_swa_v, cache_mem_k, cache_mem_v, mem_prompt, norm_even, w_in_even, conv_a, a_log, dt_bias, gdn_norm, conv_b, w_out_even, norm_odd, w_in_odd, q_norm_c, k_norm_c, w_out_odd, mem_norm, w_mem_kv, q_norm_m, k_norm_m):
    bp, t_p, d = x_prompt.shape
    bs, t_s, _ = x_sample.shape
    depth = mem_norm.shape[0]
    n_odd = norm_odd.shape[0]
    n_heads_a = a_log.shape[1]
    n_heads_c = cache_swa_k.shape[3]
    wb = cache_swa_k.shape[2]
    w_even_a, w_even_b, w_even_small = _prep_even_weights(w_in_even, n_heads_a)
    w_odd_main = cast_bf16(w_in_odd, w_in_odd.shape[2], 1024, 1024)
    w_out_e = cast_bf16(w_out_even, d, 512, 2048)
    w_out_o = cast_bf16(w_out_odd, d, 512, 2048)
    w_kv = cast_bf16(w_mem_kv, w_mem_kv.shape[2], 1024, 1024)
    cache_k = cache_swa_k.reshape(n_odd, bs, wb * n_heads_c, HEAD)
    cache_v = cache_swa_v.reshape(n_odd, bs, wb * n_heads_c, HEAD)
    y_p, y_s = x_prompt, x_sample
    gdn_p, gdn_s, gconv_p, gconv_s, sconv_p, sconv_s = [], [], [], [], [], []
    memk_p, memv_p = [], []
    kv_p = tuple(jnp.zeros((n_odd, bp * t_p, d), F32) for _ in range(2))
    kv_s = tuple(jnp.zeros((n_odd, bs * t_s, d), F32) for _ in range(2))
    for i in range(depth):
        mk_p, mv_p = mem_project(mem_prompt, mem_norm[i], w_kv, i, k_norm_m[i])
        memk_p.append(mk_p)
        memv_p.append(mv_p)
        if i % 2 == 0:
            e = i // 2
            states_p = (jnp.zeros((bp,) + state_gdn_conv.shape[2:], state_gdn_conv.dtype),
                        jnp.zeros((bp,) + state_gdn.shape[2:], state_gdn.dtype),
                        jnp.zeros((bp,) + state_sconv.shape[2:], state_sconv.dtype), mk_p, mv_p)
            states_s = (state_gdn_conv[e], state_gdn[e], state_sconv[e], cache_mem_k[i], cache_mem_v[i])
            y_p, y_s, (ca_p, sa_p, cb_p), (ca_s, sa_s, cb_s) = even_layer(
                y_p, y_s, states_p, states_s, e, norm_even[e], w_even_a, w_even_b, w_even_small, conv_a[e], a_log[e],
                dt_bias[e], gdn_norm[e], conv_b[e], w_out_e, q_norm_m[i])
            gconv_p.append(ca_p)
            gdn_p.append(sa_p)
            sconv_p.append(cb_p)
            gconv_s.append(ca_s)
            gdn_s.append(sa_s)
            sconv_s.append(cb_s)
        else:
            o = i // 2
            y_p, y_s, kv_p, kv_s = odd_layer(
                y_p, y_s, cache_k, cache_v, (mk_p, mv_p), (cache_mem_k[i], cache_mem_v[i]), o, n_odd, kv_p, kv_s,
                norm_odd[o], w_odd_main, q_norm_c[o], k_norm_c[o], w_out_o, q_norm_m[i])
    (kp, vp), (ks, vs) = kv_p, kv_s
    n_keep = min(max(w for w, _ in SWA_PATTERNS), t_p)
    swa_k_p = kp.reshape(n_odd, bp, t_p, n_heads_c, HEAD)[:, :, t_p - n_keep:]
    swa_v_p = vp.reshape(n_odd, bp, t_p, n_heads_c, HEAD)[:, :, t_p - n_keep:]
    swa_k_s, swa_v_s = cache_roll(cache_k, cache_v, ks.reshape(n_odd, bs, t_s, d), vs.reshape(n_odd, bs, t_s, d),
                                  n_heads_c)
    swa_k_s = swa_k_s.reshape(n_odd, bs, wb, n_heads_c, HEAD)
    swa_v_s = swa_v_s.reshape(n_odd, bs, wb, n_heads_c, HEAD)
    return (y_p, y_s, jnp.stack(gdn_p), jnp.stack(gdn_s), jnp.stack(gconv_p), jnp.stack(gconv_s),
            jnp.stack(sconv_p), jnp.stack(sconv_s), swa_k_p, swa_k_s, swa_v_p, swa_v_s,
            jnp.stack(memk_p), jnp.stack(memv_p))
```

```python
import functools
import math

import jax
import jax.numpy as jnp
from jax import lax
from jax.experimental import pallas as pl
from jax.experimental.pallas import tpu as pltpu

F32 = jnp.float32
BF16 = jnp.bfloat16
EPS = 1e-6
HEAD = 128
SUBLANES = 8
CONV_A = 4
CONV_B = 3
GDN_CHUNK = 64
SWA_PATTERNS = ((128, 1), (512, 4), (2048, 16))
SWA_BLOCK = 128
H_M = 4
VMEM_LIMIT = 48 * 1024 * 1024


def _cparams(n_axes):
    return pltpu.CompilerParams(dimension_semantics=("arbitrary",) * n_axes,
                                vmem_limit_bytes=VMEM_LIMIT)


def _silu(x):
    return x * jax.nn.sigmoid(x)


def _dot(a, b, precision=None):
    return jnp.dot(a, b, preferred_element_type=F32, precision=precision)


def _dot_nt(a, b, precision=None):
    return lax.dot_general(a, b, (((1,), (1,)), ((), ())), preferred_element_type=F32,
                           precision=precision)


def _dot_tn(a, b, precision=None):
    return lax.dot_general(a, b, (((0,), (0,)), ((), ())), preferred_element_type=F32,
                           precision=precision)


def _rms_cast_kernel(x_ref, g_ref, o_ref):
    x = x_ref[...]
    ms = jnp.mean(x * x, axis=-1, keepdims=True)
    o_ref[...] = (x * lax.rsqrt(ms + EPS) * g_ref[...]).astype(o_ref.dtype)


def rms_cast(x2d, g, tm):
    m, d = x2d.shape
    return pl.pallas_call(
        _rms_cast_kernel,
        grid=(m // tm,),
        in_specs=[pl.BlockSpec((tm, d), lambda i: (i, 0)),
                  pl.BlockSpec((1, d), lambda i: (0, 0))],
        out_specs=pl.BlockSpec((tm, d), lambda i: (i, 0)),
        out_shape=jax.ShapeDtypeStruct((m, d), BF16),
        compiler_params=_cparams(1),
        name="rms_cast",
    )(x2d, g.reshape(1, d))


def _mm_kernel(*refs, n_in, has_res, has_side):
    n_out = 2 if has_side else 1
    ins, outs = refs[:-n_out], refs[-n_out:]
    w_refs = ins[n_in:2 * n_in]
    per_group = n_in + (1 if has_res else 0)

    def run(a_refs, res_ref, o_ref):
        acc = _dot(a_refs[0][...], w_refs[0][...])
        for a_ref, w_ref in zip(a_refs[1:], w_refs[1:]):
            acc = acc + _dot(a_ref[...], w_ref[...])
        if res_ref is not None:
            acc = acc + res_ref[...]
        o_ref[...] = acc.astype(o_ref.dtype)

    run(ins[:n_in], ins[2 * n_in] if has_res else None, outs[0])
    if has_side:
        side = ins[n_in + per_group:]

        @pl.when(pl.program_id(1) == 0)
        def _():
            run(side[:n_in], side[n_in] if has_res else None, outs[1])


def matmul(a_list, w, layer, n_off, n_size, tm, tn, res=None, side_a=None, side_res=None):
    m = a_list[0].shape[0]
    n_in = len(a_list)
    has_side = side_a is not None
    assert (side_res is not None) == (has_side and res is not None)
    in_specs, w_specs = [], []
    row = 0
    for a in a_list:
        k = a.shape[1]
        assert row % k == 0 and n_off % tn == 0 and n_size % tn == 0 and m % tm == 0
        in_specs.append(pl.BlockSpec((tm, k), lambda n, i: (i, 0)))
        w_specs.append(pl.BlockSpec((None, k, tn), functools.partial(
            lambda n, i, rb, nb: (layer, rb, n + nb), rb=row // k, nb=n_off // tn)))
        row += k
    args = list(a_list) + [w] * n_in
    specs = in_specs + w_specs
    out_specs = [pl.BlockSpec((tm, tn), lambda n, i: (i, n))]
    out_shape = [jax.ShapeDtypeStruct((m, n_size), F32)]
    if res is not None:
        specs.append(pl.BlockSpec((tm, tn), lambda n, i: (i, n)))
        args.append(res)
    if has_side:
        ms = side_a[0].shape[0]
        for a in side_a:
            specs.append(pl.BlockSpec((ms, a.shape[1]), lambda n, i: (0, 0)))
            args.append(a)
        if side_res is not None:
            specs.append(pl.BlockSpec((ms, tn), lambda n, i: (0, n)))
            args.append(side_res)
        out_specs.append(pl.BlockSpec((ms, tn), lambda n, i: (0, n)))
        out_shape.append(jax.ShapeDtypeStruct((ms, n_size), F32))
    out = pl.pallas_call(
        functools.partial(_mm_kernel, n_in=n_in, has_res=res is not None, has_side=has_side),
        grid=(n_size // tn, m // tm),
        in_specs=specs,
        out_specs=out_specs,
        out_shape=out_shape,
        compiler_params=_cparams(2),
        name="matmul",
    )(*args)
    return tuple(out) if has_side else out[0]


def _headnorm_kernel(x_ref, g_ref, o_ref, *, heads):
    g = g_ref[...]
    for h in range(heads):
        cols = slice(h * HEAD, (h + 1) * HEAD)
        x = x_ref[:, cols]
        ms = jnp.mean(x * x, axis=-1, keepdims=True)
        o_ref[:, cols] = x * lax.rsqrt(ms + EPS) * g


def headnorm(x2d, col_off, width, gain, tm, cw):
    m = x2d.shape[0]
    assert col_off % cw == 0 and width % cw == 0 and m % tm == 0
    return pl.pallas_call(
        functools.partial(_headnorm_kernel, heads=cw // HEAD),
        grid=(m // tm, width // cw),
        in_specs=[pl.BlockSpec((tm, cw), lambda i, c: (i, c + col_off // cw)),
                  pl.BlockSpec((1, HEAD), lambda i, c: (0, 0))],
        out_specs=pl.BlockSpec((tm, cw), lambda i, c: (i, c)),
        out_shape=jax.ShapeDtypeStruct((m, width), F32),
        compiler_params=_cparams(2),
        name="headnorm",
    )(x2d, gain.reshape(1, HEAD))


def _odd_qkv_kernel(q_ref, k_ref, v_ref, qg_ref, kg_ref, k_prev_hbm, v_prev_hbm, qo_ref, ko_ref, vo_ref, *, heads):
    del k_prev_hbm, v_prev_hbm
    qg = qg_ref[...]
    kg = kg_ref[...]
    for h in range(heads):
        cols = slice(h * HEAD, (h + 1) * HEAD)
        q = q_ref[:, cols]
        qo_ref[:, cols] = q * lax.rsqrt(jnp.mean(q * q, axis=-1, keepdims=True) + EPS) * qg
        k = k_ref[:, cols]
        ko_ref[:, cols] = k * lax.rsqrt(jnp.mean(k * k, axis=-1, keepdims=True) + EPS) * kg
    vo_ref[...] = v_ref[...]


def odd_qkv(proj, w_c, q_gain, k_gain, layer, n_layers, k_prev, v_prev, tm, cw):
    m = proj.shape[0]
    assert w_c % cw == 0 and m % tm == 0 and k_prev.shape == v_prev.shape == (n_layers, m, w_c)
    nb = w_c // cw
    blk = (tm, cw)
    any_spec = pl.BlockSpec(memory_space=pl.ANY)
    stack = jax.ShapeDtypeStruct((n_layers, m, w_c), F32)
    return pl.pallas_call(
        functools.partial(_odd_qkv_kernel, heads=cw // HEAD),
        grid=(m // tm, nb),
        in_specs=[pl.BlockSpec(blk, lambda i, c: (i, c)),
                  pl.BlockSpec(blk, lambda i, c: (i, c + nb)),
                  pl.BlockSpec(blk, lambda i, c: (i, c + 2 * nb)),
                  pl.BlockSpec((1, HEAD), lambda i, c: (0, 0)),
                  pl.BlockSpec((1, HEAD), lambda i, c: (0, 0)),
                  any_spec, any_spec],
        out_specs=[pl.BlockSpec(blk, lambda i, c: (i, c)),
                   pl.BlockSpec((None,) + blk, lambda i, c: (layer, i, c)),
                   pl.BlockSpec((None,) + blk, lambda i, c: (layer, i, c))],
        out_shape=[jax.ShapeDtypeStruct((m, w_c), F32), stack, stack],
        input_output_aliases={5: 1, 6: 2},
        compiler_params=_cparams(2),
        name="odd_qkv",
    )(proj, proj, proj, q_gain.reshape(1, HEAD), k_gain.reshape(1, HEAD), k_prev, v_prev)


def _gate_kernel(x_ref, alog_ref, dt_ref, beta_ref, gam_ref, gsm_ref, tri_ref, eb_ref, eg_ref, *, n_heads, cs):
    tm = x_ref.shape[0]
    wide = n_heads * HEAD

    @pl.when(pl.program_id(0) == 0)
    def _():
        ri = lax.broadcasted_iota(jnp.int32, (tm, tm), 0)
        ci = lax.broadcasted_iota(jnp.int32, (tm, tm), 1)
        tri_ref[...] = jnp.where((ci <= ri) & (ci >= (ri // cs) * cs), 1.0, 0.0).astype(BF16)
        src = lax.broadcasted_iota(jnp.int32, (HEAD, wide), 0)
        dst = lax.broadcasted_iota(jnp.int32, (HEAD, wide), 1) // HEAD
        eb_ref[...] = jnp.where(src == dst, 1.0, 0.0).astype(BF16)
        eg_ref[...] = jnp.where(src == dst + n_heads, 1.0, 0.0).astype(BF16)

    def parts(v):
        p0 = v.astype(BF16)
        r1 = v - p0.astype(F32)
        p1 = r1.astype(BF16)
        return p0, p1, (r1 - p1.astype(F32)).astype(BF16)

    x = x_ref[...]
    beta = jax.nn.sigmoid(x)
    g = -jnp.exp(alog_ref[...]) * jax.nn.softplus(x + dt_ref[...])
    tri = tri_ref[...]
    gam = functools.reduce(lambda a, b: a + b, [_dot(tri, p) for p in reversed(parts(g))])
    gsm_ref[...] = gam
    beta_ref[...] = functools.reduce(lambda a, b: a + b, [_dot(p, eb_ref[...]) for p in reversed(parts(beta))])
    gam_ref[...] = functools.reduce(lambda a, b: a + b, [_dot(p, eg_ref[...]) for p in reversed(parts(gam))])


def gdn_gates(small2d, a_log, dt_bias, tm, cs):
    m = small2d.shape[0]
    n_heads = a_log.shape[0]
    assert tm % cs == 0 and m % tm == 0
    pad = HEAD - 2 * n_heads
    alog_row = jnp.concatenate([jnp.zeros((n_heads,), F32), a_log, jnp.zeros((pad,), F32)]).reshape(1, HEAD)
    dt_row = jnp.concatenate([jnp.zeros((n_heads,), F32), dt_bias, jnp.zeros((pad,), F32)]).reshape(1, HEAD)
    wide = n_heads * HEAD
    return pl.pallas_call(
        functools.partial(_gate_kernel, n_heads=n_heads, cs=cs),
        grid=(m // tm,),
        in_specs=[pl.BlockSpec((tm, HEAD), lambda i: (i, 0)),
                  pl.BlockSpec((1, HEAD), lambda i: (0, 0)),
                  pl.BlockSpec((1, HEAD), lambda i: (0, 0))],
        out_specs=[pl.BlockSpec((tm, wide), lambda i: (i, 0)),
                   pl.BlockSpec((tm, wide), lambda i: (i, 0)),
                   pl.BlockSpec((tm, HEAD), lambda i: (i, 0))],
        out_shape=[jax.ShapeDtypeStruct((m, wide), F32), jax.ShapeDtypeStruct((m, wide), F32),
                   jax.ShapeDtypeStruct((m, HEAD), F32)],
        scratch_shapes=[pltpu.VMEM((tm, tm), BF16), pltpu.VMEM((HEAD, wide), BF16), pltpu.VMEM((HEAD, wide), BF16)],
        compiler_params=_cparams(1),
        name="gdn_gates",
    )(small2d, alog_row, dt_row)


def _sconv_kernel(bg_ref, cg_ref, xb_ref, zb_ref, hc_ref, hx_ref, st_ref, w_ref, o_ref, nb_ref, buf_ref, *, tt):
    t = pl.program_id(2)
    buf_ref[0:SUBLANES, :] = jnp.where(t == 0, st_ref[0], hc_ref[0] * hx_ref[0])
    buf_ref[SUBLANES:SUBLANES + tt, :] = cg_ref[0] * xb_ref[0]
    w = w_ref[...]
    first = SUBLANES - (CONV_B - 1)
    y = buf_ref[first:first + tt, :] * w[0:1, :]
    for j in range(1, CONV_B):
        y = y + buf_ref[first + j:first + j + tt, :] * w[j:j + 1, :]
    o_ref[0] = (bg_ref[0] * y * _silu(zb_ref[0])).astype(o_ref.dtype)
    nb_ref[0] = buf_ref[tt:tt + SUBLANES, :]


def short_conv(proj3d, off_b, off_c, off_x, off_z, width, state_pad, conv_w, tt, cw):
    bn, t, _ = proj3d.shape
    assert t % tt == 0 and width % cw == 0
    rb = tt // SUBLANES

    def col(off):
        assert off % cw == 0
        return functools.partial(lambda b, c, i, o: (b, i, c + o), o=off // cw)

    def halo(off):
        return functools.partial(lambda b, c, i, o: (b, jnp.maximum(i * rb - 1, 0), c + o), o=off // cw)

    blk = (1, tt, cw)
    hblk = (1, SUBLANES, cw)
    return pl.pallas_call(
        functools.partial(_sconv_kernel, tt=tt),
        grid=(bn, width // cw, t // tt),
        in_specs=[pl.BlockSpec(blk, col(off_b)), pl.BlockSpec(blk, col(off_c)),
                  pl.BlockSpec(blk, col(off_x)), pl.BlockSpec(blk, col(off_z)),
                  pl.BlockSpec(hblk, halo(off_c)), pl.BlockSpec(hblk, halo(off_x)),
                  pl.BlockSpec(hblk, lambda b, c, i: (b, 0, c)),
                  pl.BlockSpec((CONV_B, cw), lambda b, c, i: (0, c))],
        out_specs=[pl.BlockSpec(blk, lambda b, c, i: (b, i, c)),
                   pl.BlockSpec(hblk, lambda b, c, i: (b, 0, c))],
        out_shape=[jax.ShapeDtypeStruct((bn, t, width), BF16),
                   jax.ShapeDtypeStruct((bn, SUBLANES, width), F32)],
        scratch_shapes=[pltpu.VMEM((tt + SUBLANES, cw), F32)],
        compiler_params=_cparams(3),
        name="short_conv",
    )(proj3d, proj3d, proj3d, proj3d, proj3d, proj3d, state_pad, conv_w)


def _split_bf16(x):
    hi = x.astype(BF16)
    return hi, (x - hi.astype(F32)).astype(BF16)


def _dot_3pass(a, b):
    a_hi, a_lo = _split_bf16(a)
    b_hi, b_lo = _split_bf16(b)
    return _dot(a_hi, b_hi) + (_dot(a_hi, b_lo) + _dot(a_lo, b_hi))


def _gdn_kernel(xq_ref, xk_ref, xv_ref, wq_ref, wk_ref, wv_ref, cq_ref, ck_ref, cv_ref, z_ref, beta_ref, gam_ref,
                grow_ref, s0_ref, gain_ref, o_ref, s_ref, xbuf, act, carry, *, hb, cs, nct):
    ti = pl.program_id(2)
    tt = nct * cs

    @pl.when(ti == 0)
    def _():
        s_ref[...] = s0_ref[...]
        for a, c_ref in enumerate((cq_ref, ck_ref, cv_ref)):
            carry[a] = c_ref[0]

    first = SUBLANES - (CONV_A - 1)
    for a, (x_ref, w_ref) in enumerate(((xq_ref, wq_ref), (xk_ref, wk_ref), (xv_ref, wv_ref))):
        xbuf[a, 0:SUBLANES, :] = carry[a]
        xbuf[a, SUBLANES:SUBLANES + tt, :] = x_ref[0]
        carry[a] = x_ref[0, tt - SUBLANES:tt, :]
        w = w_ref[...]
        y = xbuf[a, first:first + tt, :] * w[0:1, :]
        for j in range(1, CONV_A):
            y = y + xbuf[a, first + j:first + j + tt, :] * w[j:j + 1, :]
        y = _silu(y)
        if a == 2:
            act[a] = y
        else:
            scale = HEAD ** -0.5 if a == 0 else 1.0
            for h in range(hb):
                yh = y[:, h * HEAD:(h + 1) * HEAD]
                ss = jnp.sum(yh * yh, axis=-1, keepdims=True)
                act[a, :, h * HEAD:(h + 1) * HEAD] = yh * lax.rsqrt(ss + EPS) * scale

    ii = lax.broadcasted_iota(jnp.int32, (cs, cs), 0)
    jj = lax.broadcasted_iota(jnp.int32, (cs, cs), 1)
    causal = ii >= jj
    strict = ii > jj
    eye_f = (ii == jj).astype(F32)
    gain = gain_ref[...]
    n_double = int(math.log2(cs)) - 1
    chains = [(j, h) for j in range(nct) for h in range(hb)]

    def tile(ref, c):
        return ref[0, c[0] * cs:(c[0] + 1) * cs, c[1] * HEAD:(c[1] + 1) * HEAD]

    def act_tile(a, c):
        return act[a, c[0] * cs:(c[0] + 1) * cs, c[1] * HEAD:(c[1] + 1) * HEAD]

    k = [act_tile(1, c) for c in chains]
    beta = [tile(beta_ref, c) for c in chains]
    gam = [tile(gam_ref, c) for c in chains]
    kb = [a * b for a, b in zip(k, beta)]
    raw = [_dot_nt(jnp.concatenate([b, act_tile(0, c)], axis=0).astype(BF16), a.astype(BF16))
           for a, b, c in zip(k, kb, chains)]
    decay = []
    for g, (j, h) in zip(gam, chains):
        gam_row = grow_ref[0, h, pl.ds(ti * nct + j, 1), :]
        decay.append(jnp.where(causal, jnp.exp(jnp.where(causal, g[:, :cs] - gam_row, 0.0)), 0.0))
    qk = [jnp.where(causal, r[cs:] * d, 0.0) for r, d in zip(raw, decay)]
    pw = [jnp.where(strict, -(r[:cs] * d), 0.0) for r, d in zip(raw, decay)]
    inv = [eye_f + p for p in pw]
    pw = [_dot(p.astype(BF16), p.astype(BF16)) for p in pw]
    for _ in range(n_double - 1):
        prod = [_dot(jnp.concatenate([p, x], axis=0).astype(BF16), p.astype(BF16)) for p, x in zip(pw, inv)]
        pw = [p[:cs] for p in prod]
        inv = [x + p[cs:] for x, p in zip(inv, prod)]
    inv = [x + _dot(x.astype(BF16), p.astype(BF16)) for x, p in zip(inv, pw)]
    e_gam = [jnp.exp(g) for g in gam]
    uw = [_dot_3pass(x, jnp.concatenate([act_tile(2, c) * b, a * e], axis=1))
          for x, c, b, a, e in zip(inv, chains, beta, kb, e_gam)]
    wq = [jnp.concatenate([x[:, HEAD:], act_tile(0, c) * e], axis=0).astype(BF16)
          for x, c, e in zip(uw, chains, e_gam)]
    k_dec_t = [(a * jnp.exp(g[cs - 1:cs, :] - g)).T.astype(BF16) for a, g in zip(k, gam)]
    state = [s_ref[0, h] for h in range(hb)]
    for j in range(nct):
        idx = [j * hb + h for h in range(hb)]
        ws = [_dot(wq[i], state[h].astype(BF16)) for h, i in enumerate(idx)]
        v16 = [(uw[i][:, :HEAD] - x[:cs]).astype(BF16) for x, i in zip(ws, idx)]
        o = [x[cs:] + _dot(qk[i].astype(BF16), y) for x, y, i in zip(ws, v16, idx)]
        state = [s * jnp.exp(gam[i][cs - 1:cs, :]) + _dot(k_dec_t[i], y) for s, y, i in zip(state, v16, idx)]
        for h, i in enumerate(idx):
            ms = jnp.mean(o[h] * o[h], axis=-1, keepdims=True)
            on = o[h] * lax.rsqrt(ms + EPS) * gain
            o_ref[0, j * cs:(j + 1) * cs, h * HEAD:(h + 1) * HEAD] = (
                on * _silu(tile(z_ref, (j, h)))).astype(o_ref.dtype)
    for h in range(hb):
        s_ref[0, h] = state[h]


def gdn(proj3d, z_off, conv_w, conv_state, beta_rep, gam_rep, gam_row4, s0, gain, hb, tt):
    bn, t, _ = proj3d.shape
    n_heads = s0.shape[1]
    nc, cs = gam_row4.shape[2], gam_row4.shape[3]
    cw = hb * HEAD
    ng = n_heads // hb
    assert z_off % cw == 0 and n_heads % hb == 0 and t % tt == 0 and tt % cs == 0 and tt % SUBLANES == 0
    blk = (1, tt, cw)
    sblk = (1, hb, HEAD, HEAD)
    qkv_specs = [pl.BlockSpec(blk, functools.partial(lambda b, h, i, a: (b, i, a * ng + h), a=a)) for a in range(3)]
    w_specs = [pl.BlockSpec((CONV_A, cw), functools.partial(lambda b, h, i, a: (0, a * ng + h), a=a)) for a in range(3)]
    c_specs = [pl.BlockSpec((1, SUBLANES, cw), functools.partial(lambda b, h, i, a: (b, 0, a * ng + h), a=a))
               for a in range(3)]
    return pl.pallas_call(
        functools.partial(_gdn_kernel, hb=hb, cs=cs, nct=tt // cs),
        grid=(bn, ng, t // tt),
        in_specs=qkv_specs + w_specs + c_specs + [
                  pl.BlockSpec(blk, lambda b, h, i: (b, i, z_off // cw + h)),
                  pl.BlockSpec(blk, lambda b, h, i: (b, i, h)),
                  pl.BlockSpec(blk, lambda b, h, i: (b, i, h)),
                  pl.BlockSpec((1, hb, nc, cs), lambda b, h, i: (b, h, 0, 0)),
                  pl.BlockSpec(sblk, lambda b, h, i: (b, h, 0, 0)),
                  pl.BlockSpec((1, HEAD), lambda b, h, i: (0, 0))],
        out_specs=[pl.BlockSpec(blk, lambda b, h, i: (b, i, h)),
                   pl.BlockSpec(sblk, lambda b, h, i: (b, h, 0, 0))],
        out_shape=[jax.ShapeDtypeStruct((bn, t, n_heads * HEAD), BF16),
                   jax.ShapeDtypeStruct((bn, n_heads, HEAD, HEAD), F32)],
        scratch_shapes=[pltpu.VMEM((3, tt + SUBLANES, cw), F32), pltpu.VMEM((3, tt, cw), F32),
                        pltpu.VMEM((3, SUBLANES, cw), F32)],
        compiler_params=_cparams(3),
        name="gdn",
    )(proj3d, proj3d, proj3d, conv_w, conv_w, conv_w, conv_state, conv_state, conv_state, proj3d, beta_rep,
      gam_rep, gam_row4, s0, gain.reshape(1, HEAD))


def _mem_attn_kernel(q_ref, z_ref, mk_ref, mv_ref, gain_ref, o_ref):
    gain = gain_ref[...]
    for h in range(H_M):
        cols = slice(h * HEAD, (h + 1) * HEAD)
        q = q_ref[0, :, cols]
        ms = jnp.mean(q * q, axis=-1, keepdims=True)
        qn = (q * lax.rsqrt(ms + EPS) * gain).astype(BF16)
        sc = _dot_nt(qn, mk_ref[0, :, cols].astype(BF16)) * (HEAD ** -0.5)
        m = jnp.max(sc, axis=-1, keepdims=True)
        p = jnp.exp(sc - m)
        p = p / jnp.sum(p, axis=-1, keepdims=True)
        o = _dot(p.astype(BF16), mv_ref[0, :, cols].astype(BF16))
        o_ref[0, :, cols] = (o * _silu(z_ref[0, :, cols])).astype(o_ref.dtype)


def mem_attn(proj3d, q_off, z_off, mem_k, mem_v, q_gain, tt):
    bn, t, _ = proj3d.shape
    n_mem = mem_k.shape[1]
    wm = H_M * HEAD
    assert q_off % wm == 0 and z_off % wm == 0 and t % tt == 0
    return pl.pallas_call(
        _mem_attn_kernel,
        grid=(bn, t // tt),
        in_specs=[pl.BlockSpec((1, tt, wm), lambda b, i: (b, i, q_off // wm)),
                  pl.BlockSpec((1, tt, wm), lambda b, i: (b, i, z_off // wm)),
                  pl.BlockSpec((1, n_mem, wm), lambda b, i: (b, 0, 0)),
                  pl.BlockSpec((1, n_mem, wm), lambda b, i: (b, 0, 0)),
                  pl.BlockSpec((1, HEAD), lambda b, i: (0, 0))],
        out_specs=pl.BlockSpec((1, tt, wm), lambda b, i: (b, i, 0)),
        out_shape=jax.ShapeDtypeStruct((bn, t, wm), BF16),
        compiler_params=_cparams(2),
        name="mem_attn",
    )(proj3d, proj3d, mem_k.reshape(bn, n_mem, wm), mem_v.reshape(bn, n_mem, wm), q_gain.reshape(1, HEAD))


def _dil_prompt_kernel(q_ref, k_ref, v_ref, z_ref, o_ref, acc_ref, lse_ref, *, hg, t, group):
    qi = lax.broadcasted_iota(jnp.int32, (SWA_BLOCK, SWA_BLOCK), 0)
    ki = lax.broadcasted_iota(jnp.int32, (SWA_BLOCK, SWA_BLOCK), 1)
    n_pat = len(SWA_PATTERNS)
    for h in range(hg):
        cols = slice(h * HEAD, (h + 1) * HEAD)
        for p, (window, dil) in enumerate(SWA_PATTERNS):
            assert window // dil == SWA_BLOCK
            l_sub = t // dil
            nb = l_sub // SWA_BLOCK
            assert nb * SWA_BLOCK * dil == t

            def rows_of(r, c, dil=dil):
                return pl.ds(r + c * (SWA_BLOCK * dil), SWA_BLOCK, stride=dil)

            blocks = [(r, c) for r in range(dil) for c in range(nb)]
            for g0 in range(0, len(blocks), group):
                grp = blocks[g0:g0 + group]
                q = [(q_ref[0, rows_of(r, c), cols] * (HEAD ** -0.5)).astype(BF16) for r, c in grp]
                s_cur = [jnp.where(ki <= qi, _dot_nt(x, k_ref[0, rows_of(r, c), cols].astype(BF16)), -jnp.inf)
                         for x, (r, c) in zip(q, grp)]
                s_prev = [jnp.where(ki >= qi, _dot_nt(x, k_ref[0, rows_of(r, c - 1), cols].astype(BF16)), -jnp.inf)
                          if c > 0 else None for x, (r, c) in zip(q, grp)]
                m = [jnp.max(a, axis=-1, keepdims=True) if b is None else
                     jnp.maximum(jnp.max(a, axis=-1, keepdims=True), jnp.max(b, axis=-1, keepdims=True))
                     for a, b in zip(s_cur, s_prev)]
                e_cur = [jnp.exp(a - x) for a, x in zip(s_cur, m)]
                e_prev = [None if b is None else jnp.exp(b - x) for b, x in zip(s_prev, m)]
                den = [jnp.sum(a, axis=-1, keepdims=True) if b is None else
                       jnp.sum(a, axis=-1, keepdims=True) + jnp.sum(b, axis=-1, keepdims=True)
                       for a, b in zip(e_cur, e_prev)]
                o = [_dot(a.astype(BF16), v_ref[0, rows_of(r, c), cols].astype(BF16)) for a, (r, c) in zip(e_cur, grp)]
                o = [x if b is None else x + _dot(b.astype(BF16), v_ref[0, rows_of(r, c - 1), cols].astype(BF16))
                     for x, b, (r, c) in zip(o, e_prev, grp)]
                for x, d, mx, (r, c) in zip(o, den, m, grp):
                    acc_ref[p, rows_of(r, c), :] = x / d
                    lse_ref[p, rows_of(r, c), :] = jnp.broadcast_to(mx + jnp.log(d), (SWA_BLOCK, HEAD))
        lse = [lse_ref[p] for p in range(n_pat)]
        top = functools.reduce(jnp.maximum, lse)
        wgt = [jnp.exp(x - top) for x in lse]
        tot = functools.reduce(lambda a, b: a + b, wgt)
        mix = functools.reduce(lambda a, b: a + b, [(wgt[p] / tot) * acc_ref[p] for p in range(n_pat)])
        o_ref[0, :, cols] = (mix * _silu(z_ref[0, :, cols])).astype(o_ref.dtype)


def dil_prompt(qn, k_stack, v_stack, layer, proj3d, z_off, hg):
    bn, t, wc = qn.shape
    cw = hg * HEAD
    assert z_off % cw == 0 and wc % cw == 0
    blk = (1, t, cw)
    sblk = (None, 1, t, cw)
    return pl.pallas_call(
        functools.partial(_dil_prompt_kernel, hg=hg, t=t, group=8),
        grid=(bn, wc // cw),
        in_specs=[pl.BlockSpec(blk, lambda b, h: (b, 0, h)),
                  pl.BlockSpec(sblk, lambda b, h: (layer, b, 0, h)),
                  pl.BlockSpec(sblk, lambda b, h: (layer, b, 0, h)),
                  pl.BlockSpec(blk, lambda b, h: (b, 0, z_off // cw + h))],
        out_specs=pl.BlockSpec(blk, lambda b, h: (b, 0, h)),
        out_shape=jax.ShapeDtypeStruct((bn, t, wc), BF16),
        scratch_shapes=[pltpu.VMEM((len(SWA_PATTERNS), t, HEAD), F32),
                        pltpu.VMEM((len(SWA_PATTERNS), t, HEAD), F32)],
        compiler_params=_cparams(2),
        name="dil_prompt",
    )(qn, k_stack, v_stack, proj3d)


MASKED_MAX = -1e30


def _dil_sample_kernel(q_ref, kn_ref, vn_ref, z_ref, kc_ref, vc_ref, o_ref, m_ref, l_ref, acc_ref,
                       *, n_heads, t, wb, tw, group):
    j = pl.program_id(1)
    n_pat = len(SWA_PATTERNS)

    @pl.when(j == 0)
    def _():
        m_ref[...] = jnp.full(m_ref.shape, MASKED_MAX, F32)
        l_ref[...] = jnp.zeros(l_ref.shape, F32)
        acc_ref[...] = jnp.zeros(acc_ref.shape, F32)

    def head_rows(ref, h):
        return ref[0, pl.ds(h, tw, stride=n_heads), :].astype(BF16)

    def q16(h):
        return (q_ref[0, :, h * HEAD:(h + 1) * HEAD] * (HEAD ** -0.5)).astype(BF16)

    def update(hs, p, scores, ok, values):
        idx = [h * n_pat + p for h in hs]
        s = [jnp.where(ok, x, -jnp.inf) for x in scores]
        m_old = [m_ref[i] for i in idx]
        m_new = [jnp.maximum(a, jnp.max(x, axis=-1, keepdims=True)) for a, x in zip(m_old, s)]
        alpha = [jnp.exp(a - b) for a, b in zip(m_old, m_new)]
        e = [jnp.exp(x - b[:, :1]) for x, b in zip(s, m_new)]
        pv = [_dot(x.astype(BF16), v) for x, v in zip(e, values)]
        for i, a, x, y, b in zip(idx, alpha, e, pv, m_new):
            l_ref[i] = l_ref[i] * a + jnp.sum(x, axis=-1, keepdims=True)
            acc_ref[i] = acc_ref[i] * a + y
            m_ref[i] = b

    row = lax.broadcasted_iota(jnp.int32, (t, tw), 0)
    col = lax.broadcasted_iota(jnp.int32, (t, tw), 1)
    delta = wb + row - (j * tw + col)
    nearest = wb - (j + 1) * tw + 1
    for p, (window, dil) in enumerate(SWA_PATTERNS):
        def tile_update(p=p, window=window, dil=dil):
            ok = ((delta & (dil - 1)) == 0) & (delta <= window)
            for g0 in range(0, n_heads, group):
                hs = list(range(g0, g0 + group))
                scores = [_dot_nt(q16(h), head_rows(kc_ref, h)) for h in hs]
                update(hs, p, scores, ok, [head_rows(vc_ref, h) for h in hs])
        if window >= wb - tw + 1:
            tile_update()
        else:
            pl.when(nearest <= window)(tile_update)

    @pl.when(j == pl.num_programs(1) - 1)
    def _():
        dn = lax.broadcasted_iota(jnp.int32, (t, t), 0) - lax.broadcasted_iota(jnp.int32, (t, t), 1)
        for g0 in range(0, n_heads, group):
            hs = list(range(g0, g0 + group))
            cols = [slice(h * HEAD, (h + 1) * HEAD) for h in hs]
            scores = [_dot_nt(q16(h), kn_ref[0, :, c].astype(BF16)) for h, c in zip(hs, cols)]
            values = [vn_ref[0, :, c].astype(BF16) for c in cols]
            for p, (window, dil) in enumerate(SWA_PATTERNS):
                ok = (dn >= 0) & ((dn & (dil - 1)) == 0) & (dn <= window)
                update(hs, p, scores, ok, values)
            for h, c in zip(hs, cols):
                idx = [h * n_pat + p for p in range(n_pat)]
                lse = [m_ref[i] + jnp.log(l_ref[i]) for i in idx]
                top = functools.reduce(jnp.maximum, lse)
                wgt = [jnp.exp(x - top) for x in lse]
                tot = functools.reduce(lambda a, b: a + b, wgt)
                mix = functools.reduce(lambda a, b: a + b,
                                       [(w / tot) * (acc_ref[i] / l_ref[i]) for w, i in zip(wgt, idx)])
                o_ref[0, :, c] = (mix * _silu(z_ref[0, :, c])).astype(o_ref.dtype)


def dil_sample(qn, k_stack, v_stack, layer, proj3d, z_off, cache_k, cache_v, n_heads, tw):
    bn, t, wc = qn.shape
    wb = cache_k.shape[2] // n_heads
    assert z_off % wc == 0 and wb % tw == 0 and wc == n_heads * HEAD
    for window, dil in SWA_PATTERNS:
        assert dil & (dil - 1) == 0
    blk = (1, t, wc)
    sblk = (None, 1, t, wc)
    cblk = (None, 1, tw * n_heads, HEAD)
    state = pltpu.VMEM((n_heads * len(SWA_PATTERNS), t, HEAD), F32)
    return pl.pallas_call(
        functools.partial(_dil_sample_kernel, n_heads=n_heads, t=t, wb=wb, tw=tw, group=8),
        grid=(bn, wb // tw),
        in_specs=[pl.BlockSpec(blk, lambda b, j: (b, 0, 0)),
                  pl.BlockSpec(sblk, lambda b, j: (layer, b, 0, 0)),
                  pl.BlockSpec(sblk, lambda b, j: (layer, b, 0, 0)),
                  pl.BlockSpec(blk, lambda b, j: (b, 0, z_off // wc)),
                  pl.BlockSpec(cblk, lambda b, j: (layer, b, j, 0)),
                  pl.BlockSpec(cblk, lambda b, j: (layer, b, j, 0))],
        out_specs=pl.BlockSpec(blk, lambda b, j: (b, 0, 0)),
        out_shape=jax.ShapeDtypeStruct((bn, t, wc), BF16),
        scratch_shapes=[state, state, state],
        compiler_params=_cparams(2),
        name="dil_sample",
    )(qn, k_stack, v_stack, proj3d, cache_k, cache_v)


CACHE_ROLL_CHUNK_BYTES = 4 * 1024 * 1024


def _cache_roll_kernel(ck_hbm, cv_hbm, kn_ref, vn_ref, ok_hbm, ov_hbm, buf, stage, in_sems, out_sems, new_sems,
                       *, n_heads, t, chunk, n_chunks):
    l = pl.program_id(0)
    b = pl.program_id(1)
    new = t * n_heads
    srcs = (ck_hbm, cv_hbm)
    dsts = (ok_hbm, ov_hbm)

    def read(a, c):
        return pltpu.make_async_copy(srcs[a].at[l, b, pl.ds(new + c * chunk, chunk)], buf.at[a, c % 2],
                                     in_sems.at[a, c % 2])

    def write(a, c):
        return pltpu.make_async_copy(buf.at[a, c % 2], dsts[a].at[l, b, pl.ds(c * chunk, chunk)],
                                     out_sems.at[a, c % 2])

    def write_new(a):
        return pltpu.make_async_copy(stage.at[a], dsts[a].at[l, b, pl.ds(n_chunks * chunk, new)], new_sems.at[a])

    for a in range(2):
        read(a, 0).start()
    for h in range(n_heads):
        stage[0, pl.ds(h, t, stride=n_heads), :] = kn_ref[:, h * HEAD:(h + 1) * HEAD]
        stage[1, pl.ds(h, t, stride=n_heads), :] = vn_ref[:, h * HEAD:(h + 1) * HEAD]
    for a in range(2):
        write_new(a).start()
    for c in range(n_chunks):
        for a in range(2):
            read(a, c).wait()
            write(a, c).start()
            if c + 1 < n_chunks:
                if c >= 1:
                    write(a, c - 1).wait()
                read(a, c + 1).start()
    for a in range(2):
        if n_chunks >= 2:
            write(a, n_chunks - 2).wait()
        write(a, n_chunks - 1).wait()
        write_new(a).wait()


def cache_roll(cache_k, cache_v, k_new, v_new, n_heads):
    n_l, bn, rows, _ = cache_k.shape
    t = k_new.shape[2]
    wb = rows // n_heads
    keep_tokens = wb - t
    token_bytes = n_heads * HEAD * cache_k.dtype.itemsize
    chunk_tokens = max(d for d in range(1, keep_tokens + 1)
                       if keep_tokens % d == 0 and d * token_bytes <= CACHE_ROLL_CHUNK_BYTES)
    chunk = chunk_tokens * n_heads
    nblk = (None, None, t, n_heads * HEAD)
    any_spec = pl.BlockSpec(memory_space=pl.ANY)
    out = jax.ShapeDtypeStruct(cache_k.shape, cache_k.dtype)
    return pl.pallas_call(
        functools.partial(_cache_roll_kernel, n_heads=n_heads, t=t, chunk=chunk, n_chunks=keep_tokens // chunk_tokens),
        grid=(n_l, bn),
        in_specs=[any_spec, any_spec,
                  pl.BlockSpec(nblk, lambda l, b: (l, b, 0, 0)),
                  pl.BlockSpec(nblk, lambda l, b: (l, b, 0, 0))],
        out_specs=[any_spec, any_spec],
        out_shape=[out, out],
        scratch_shapes=[pltpu.VMEM((2, 2, chunk, HEAD), cache_k.dtype),
                        pltpu.VMEM((2, t * n_heads, HEAD), cache_k.dtype),
                        pltpu.SemaphoreType.DMA((2, 2)), pltpu.SemaphoreType.DMA((2, 2)),
                        pltpu.SemaphoreType.DMA((2,))],
        compiler_params=_cparams(2),
        name="cache_roll",
    )(cache_k, cache_v, k_new, v_new)


def _row_tile(m, pref):
    return pref if m % pref == 0 else m


def _flat(x):
    return x.reshape(x.shape[0] * x.shape[1], x.shape[2])


def _in_proj(x_big, x_small, g_norm, w_stacks, layer):
    xb, xs = _flat(x_big), _flat(x_small)
    hb = rms_cast(xb, g_norm, _row_tile(xb.shape[0], 256))
    hs = rms_cast(xs, g_norm, _row_tile(xs.shape[0], 256))
    tm = _row_tile(xb.shape[0], 512)
    return [matmul([hb], w, layer, 0, w.shape[2], tm, min(1024, w.shape[2]), side_a=[hs]) for w in w_stacks]


def _out_proj(x_big, x_small, parts_big, parts_small, w_out, layer):
    xb, xs = _flat(x_big), _flat(x_small)
    yb, ys = matmul([_flat(p) for p in parts_big], w_out, layer, 0, xb.shape[1], _row_tile(xb.shape[0], 512), 1024,
                    res=xb, side_a=[_flat(p) for p in parts_small], side_res=xs)
    return yb.reshape(x_big.shape), ys.reshape(x_small.shape)


def _pad_state(buf):
    return jnp.pad(buf, ((0, 0), (SUBLANES - buf.shape[1], 0), (0, 0)))


def _even_branches(shape, proj_a, proj_b, small, conv_a_buf, s0, conv_b_buf, mem_k, mem_v, conv_a_w, a_log, dt_bias,
                   gdn_gain, conv_b_w, q_gain_m):
    bn, t, _ = shape
    m = bn * t
    n_heads = a_log.shape[0]
    w_a = n_heads * HEAD
    w_b = conv_b_w.shape[1]
    w_m = H_M * HEAD
    pa = proj_a.reshape(bn, t, proj_a.shape[1])
    pb = proj_b.reshape(bn, t, proj_b.shape[1])
    tt = _row_tile(t, 512)
    cs = min(GDN_CHUNK, t)
    nc = t // cs
    beta_rep, gam_rep, gam_small = gdn_gates(small, a_log, dt_bias, _row_tile(m, 512), cs)
    gam_row4 = (gam_small.reshape(bn, t, HEAD)[:, :, n_heads:2 * n_heads]
                .transpose(0, 2, 1).reshape(bn, n_heads, nc, cs))
    o_a, new_s = gdn(pa, 3 * w_a, conv_a_w, _pad_state(conv_a_buf), beta_rep.reshape(bn, t, w_a),
                     gam_rep.reshape(bn, t, w_a), gam_row4, s0, gdn_gain, 4, _row_tile(t, 2 * GDN_CHUNK))
    new_conv_a = pa[:, t - (CONV_A - 1):, :3 * w_a]
    o_b, nb8 = short_conv(pb, 0, w_b, 2 * w_b, 3 * w_b, w_b, _pad_state(conv_b_buf), conv_b_w, tt, 512)
    new_conv_b = nb8[:, SUBLANES - (CONV_B - 1):, :]
    o_m = mem_attn(pb, 4 * w_b, 4 * w_b + w_m, mem_k, mem_v, q_gain_m, tt)
    return (o_a, o_b, o_m), (new_conv_a, new_s, new_conv_b)


def even_layer(x_big, x_small, states_big, states_small, layer, g_norm, w_a, w_b, w_small, conv_a_w, a_log, dt_bias,
               gdn_gain, conv_b_w, w_out, q_gain_m):
    (pa_b, pa_s), (pb_b, pb_s), (sm_b, sm_s) = _in_proj(x_big, x_small, g_norm, (w_a, w_b, w_small), layer)
    wts = (conv_a_w, a_log, dt_bias, gdn_gain, conv_b_w, q_gain_m)
    parts_b, new_b = _even_branches(x_big.shape, pa_b, pb_b, sm_b, *states_big, *wts)
    parts_s, new_s = _even_branches(x_small.shape, pa_s, pb_s, sm_s, *states_small, *wts)
    y_b, y_s = _out_proj(x_big, x_small, parts_b, parts_s, w_out, layer)
    return y_b, y_s, new_b, new_s


def _odd_branches(shape, proj, cache_k, cache_v, mem_k, mem_v, layer, n_layers, k_prev, v_prev, q_gain_c, k_gain_c,
                  q_gain_m):
    bn, t, w_c = shape
    m = bn * t
    w_m = H_M * HEAD
    off_z = 3 * w_c
    off_qm = 4 * w_c
    proj3d = proj.reshape(bn, t, proj.shape[1])
    qn, k_stack, v_stack = odd_qkv(proj, w_c, q_gain_c, k_gain_c, layer, n_layers, k_prev, v_prev,
                                   _row_tile(m, 512), 512)
    qn = qn.reshape(bn, t, w_c)
    k4 = k_stack.reshape(n_layers, bn, t, w_c)
    v4 = v_stack.reshape(n_layers, bn, t, w_c)
    if cache_k is None:
        o_c = dil_prompt(qn, k4, v4, layer, proj3d, off_z, 1)
    else:
        o_c = dil_sample(qn, k4, v4, layer, proj3d, off_z, cache_k, cache_v, w_c // HEAD, 256)
    o_m = mem_attn(proj3d, off_qm, off_qm + w_m, mem_k, mem_v, q_gain_m, _row_tile(t, 512))
    return (o_c, o_m), k_stack, v_stack


def odd_layer(x_big, x_small, cache_k, cache_v, mem_big, mem_small, layer, n_layers, kv_big, kv_small, g_norm, w_main,
              q_gain_c, k_gain_c, w_out, q_gain_m):
    ((p_b, p_s),) = _in_proj(x_big, x_small, g_norm, (w_main,), layer)
    gains = (q_gain_c, k_gain_c, q_gain_m)
    parts_b, kb, vb = _odd_branches(x_big.shape, p_b, None, None, *mem_big, layer, n_layers, *kv_big, *gains)
    parts_s, ks, vs = _odd_branches(x_small.shape, p_s, cache_k, cache_v, *mem_small, layer, n_layers, *kv_small,
                                    *gains)
    y_b, y_s = _out_proj(x_big, x_small, parts_b, parts_s, w_out, layer)
    return y_b, y_s, (kb, vb), (ks, vs)


def mem_project(mem, g, w_kv_stack, layer, k_gain):
    bn, n_mem, d = mem.shape
    m = bn * n_mem
    w_m = H_M * HEAD
    hm = rms_cast(mem.reshape(m, d), g, _row_tile(m, 256))
    kv = matmul([hm], w_kv_stack, layer, 0, 2 * w_m, _row_tile(m, 512), w_m)
    k = headnorm(kv, 0, w_m, k_gain, _row_tile(m, 512), w_m)
    return k.reshape(bn, n_mem, H_M, HEAD), kv[:, w_m:].reshape(bn, n_mem, H_M, HEAD)


def _cast_kernel(x_ref, o_ref):
    o_ref[...] = x_ref[...].astype(o_ref.dtype)


def cast_bf16(w, n_size, tk, tn):
    n_l, k, _ = w.shape
    assert k % tk == 0 and n_size % tn == 0
    return pl.pallas_call(
        _cast_kernel,
        grid=(n_l, k // tk, n_size // tn),
        in_specs=[pl.BlockSpec((None, tk, tn), lambda l, i, j: (l, i, j))],
        out_specs=pl.BlockSpec((None, tk, tn), lambda l, i, j: (l, i, j)),
        out_shape=jax.ShapeDtypeStruct((n_l, k, n_size), BF16),
        compiler_params=_cparams(3),
        name="cast_bf16",
    )(w)


def _prep_even_weights(w_in, n_heads):
    cut = 4 * n_heads * HEAD
    w_a = w_in[:, :, :cut].astype(BF16)
    w_b = w_in[:, :, cut + 2 * n_heads:].astype(BF16)
    w_small = jnp.pad(w_in[:, :, cut:cut + 2 * n_heads], ((0, 0), (0, 0), (0, HEAD - 2 * n_heads))).astype(BF16)
    return w_a, w_b, w_small


def kernel(x_prompt, x_sample, state_gdn, state_gdn_conv, state_sconv, cache_swa_k, cache_swa_v, cache_mem_k, cache_mem_v, mem_prompt, norm_even, w_in_even, conv_a, a_log, dt_bias, gdn_norm, conv_b, w_out_even, norm_odd, w_in_odd, q_norm_c, k_norm_c, w_out_odd, mem_norm, w_mem_kv, q_norm_m, k_norm_m):
    bp, t_p, d = x_prompt.shape
    bs, t_s, _ = x_sample.shape
    depth = mem_norm.shape[0]
    n_odd = norm_odd.shape[0]
    n_heads_a = a_log.shape[1]
    n_heads_c = cache_swa_k.shape[3]
    wb = cache_swa_k.shape[2]
    w_even_a, w_even_b, w_even_small = _prep_even_weights(w_in_even, n_heads_a)
    w_odd_main = cast_bf16(w_in_odd, w_in_odd.shape[2], 1024, 1024)
    w_out_e = cast_bf16(w_out_even, d, 512, 2048)
    w_out_o = cast_bf16(w_out_odd, d, 512, 2048)
    w_kv = cast_bf16(w_mem_kv, w_mem_kv.shape[2], 1024, 1024)
    cache_k = cache_swa_k.reshape(n_odd, bs, wb * n_heads_c, HEAD)
    cache_v = cache_swa_v.reshape(n_odd, bs, wb * n_heads_c, HEAD)
    y_p, y_s = x_prompt, x_sample
    gdn_p, gdn_s, gconv_p, gconv_s, sconv_p, sconv_s = [], [], [], [], [], []
    memk_p, memv_p = [], []
    kv_p = tuple(jnp.zeros((n_odd, bp * t_p, d), F32) for _ in range(2))
    kv_s = tuple(jnp.zeros((n_odd, bs * t_s, d), F32) for _ in range(2))
    for i in range(depth):
        mk_p, mv_p = mem_project(mem_prompt, mem_norm[i], w_kv, i, k_norm_m[i])
        memk_p.append(mk_p)
        memv_p.append(mv_p)
        if i % 2 == 0:
            e = i // 2
            states_p = (jnp.zeros((bp,) + state_gdn_conv.shape[2:], state_gdn_conv.dtype),
                        jnp.zeros((bp,) + state_gdn.shape[2:], state_gdn.dtype),
                        jnp.zeros((bp,) + state_sconv.shape[2:], state_sconv.dtype), mk_p, mv_p)
            states_s = (state_gdn_conv[e], state_gdn[e], state_sconv[e], cache_mem_k[i], cache_mem_v[i])
            y_p, y_s, (ca_p, sa_p, cb_p), (ca_s, sa_s, cb_s) = even_layer(
                y_p, y_s, states_p, states_s, e, norm_even[e], w_even_a, w_even_b, w_even_small, conv_a[e], a_log[e],
                dt_bias[e], gdn_norm[e], conv_b[e], w_out_e, q_norm_m[i])
            gconv_p.append(ca_p)
            gdn_p.append(sa_p)
            sconv_p.append(cb_p)
            gconv_s.append(ca_s)
            gdn_s.append(sa_s)
            sconv_s.append(cb_s)
        else:
            o = i // 2
            y_p, y_s, kv_p, kv_s = odd_layer(
                y_p, y_s, cache_k, cache_v, (mk_p, mv_p), (cache_mem_k[i], cache_mem_v[i]), o, n_odd, kv_p, kv_s,
                norm_odd[o], w_odd_main, q_norm_c[o], k_norm_c[o], w_out_o, q_norm_m[i])
    (kp, vp), (ks, vs) = kv_p, kv_s
    n_keep = min(max(w for w, _ in SWA_PATTERNS), t_p)
    swa_k_p = kp.reshape(n_odd, bp, t_p, n_heads_c, HEAD)[:, :, t_p - n_keep:]
    swa_v_p = vp.reshape(n_odd, bp, t_p, n_heads_c, HEAD)[:, :, t_p - n_keep:]
    swa_k_s, swa_v_s = cache_roll(cache_k, cache_v, ks.reshape(n_odd, bs, t_s, d), vs.reshape(n_odd, bs, t_s, d),
                                  n_heads_c)
    swa_k_s = swa_k_s.reshape(n_odd, bs, wb, n_heads_c, HEAD)
    swa_v_s = swa_v_s.reshape(n_odd, bs, wb, n_heads_c, HEAD)
    return (y_p, y_s, jnp.stack(gdn_p), jnp.stack(gdn_s), jnp.stack(gconv_p), jnp.stack(gconv_s),
            jnp.stack(sconv_p), jnp.stack(sconv_s), swa_k_p, swa_k_s, swa_v_p, swa_v_s,
            jnp.stack(memk_p), jnp.stack(memv_p))
```

```python
import functools
import math

import jax
import jax.numpy as jnp
from jax import lax
from jax.experimental import pallas as pl
from jax.experimental.pallas import tpu as pltpu

F32 = jnp.float32
BF16 = jnp.bfloat16
EPS = 1e-6
HEAD = 128
SUBLANES = 8
CONV_A = 4
CONV_B = 3
GDN_CHUNK = 64
SWA_PATTERNS = ((128, 1), (512, 4), (2048, 16))
SWA_BLOCK = 128
H_M = 4
VMEM_LIMIT = 56 * 1024 * 1024


def _cparams(n_axes):
    return pltpu.CompilerParams(dimension_semantics=("arbitrary",) * n_axes,
                                vmem_limit_bytes=VMEM_LIMIT)


def _silu(x):
    return x * jax.nn.sigmoid(x)


def _dot(a, b, precision=None):
    return jnp.dot(a, b, preferred_element_type=F32, precision=precision)


def _dot_nt(a, b, precision=None):
    return lax.dot_general(a, b, (((1,), (1,)), ((), ())), preferred_element_type=F32,
                           precision=precision)


def _dot_tn(a, b, precision=None):
    return lax.dot_general(a, b, (((0,), (0,)), ((), ())), preferred_element_type=F32,
                           precision=precision)


def _rms_cast_kernel(x_ref, g_ref, o_ref):
    x = x_ref[...]
    ms = jnp.mean(x * x, axis=-1, keepdims=True)
    o_ref[...] = (x * lax.rsqrt(ms + EPS) * g_ref[...]).astype(o_ref.dtype)


def rms_cast(x2d, g, tm):
    m, d = x2d.shape
    return pl.pallas_call(
        _rms_cast_kernel,
        grid=(m // tm,),
        in_specs=[pl.BlockSpec((tm, d), lambda i: (i, 0)),
                  pl.BlockSpec((1, d), lambda i: (0, 0))],
        out_specs=pl.BlockSpec((tm, d), lambda i: (i, 0)),
        out_shape=jax.ShapeDtypeStruct((m, d), BF16),
        compiler_params=_cparams(1),
        name="rms_cast",
    )(x2d, g.reshape(1, d))


def _mm_kernel(*refs, n_in, has_res, has_side):
    n_out = 2 if has_side else 1
    ins, outs = refs[:-n_out], refs[-n_out:]
    w_refs = ins[n_in:2 * n_in]
    per_group = n_in + (1 if has_res else 0)

    def run(a_refs, res_ref, o_ref):
        acc = _dot(a_refs[0][...], w_refs[0][...])
        for a_ref, w_ref in zip(a_refs[1:], w_refs[1:]):
            acc = acc + _dot(a_ref[...], w_ref[...])
        if res_ref is not None:
            acc = acc + res_ref[...]
        o_ref[...] = acc.astype(o_ref.dtype)

    run(ins[:n_in], ins[2 * n_in] if has_res else None, outs[0])
    if has_side:
        side = ins[n_in + per_group:]

        @pl.when(pl.program_id(1) == 0)
        def _():
            run(side[:n_in], side[n_in] if has_res else None, outs[1])


def matmul(a_list, w, layer, n_off, n_size, tm, tn, res=None, side_a=None, side_res=None):
    m = a_list[0].shape[0]
    n_in = len(a_list)
    has_side = side_a is not None
    assert (side_res is not None) == (has_side and res is not None)
    in_specs, w_specs = [], []
    row = 0
    for a in a_list:
        k = a.shape[1]
        assert row % k == 0 and n_off % tn == 0 and n_size % tn == 0 and m % tm == 0
        in_specs.append(pl.BlockSpec((tm, k), lambda n, i: (i, 0)))
        w_specs.append(pl.BlockSpec((None, k, tn), functools.partial(
            lambda n, i, rb, nb: (layer, rb, n + nb), rb=row // k, nb=n_off // tn)))
        row += k
    args = list(a_list) + [w] * n_in
    specs = in_specs + w_specs
    out_specs = [pl.BlockSpec((tm, tn), lambda n, i: (i, n))]
    out_shape = [jax.ShapeDtypeStruct((m, n_size), F32)]
    if res is not None:
        specs.append(pl.BlockSpec((tm, tn), lambda n, i: (i, n)))
        args.append(res)
    if has_side:
        ms = side_a[0].shape[0]
        for a in side_a:
            specs.append(pl.BlockSpec((ms, a.shape[1]), lambda n, i: (0, 0)))
            args.append(a)
        if side_res is not None:
            specs.append(pl.BlockSpec((ms, tn), lambda n, i: (0, n)))
            args.append(side_res)
        out_specs.append(pl.BlockSpec((ms, tn), lambda n, i: (0, n)))
        out_shape.append(jax.ShapeDtypeStruct((ms, n_size), F32))
    out = pl.pallas_call(
        functools.partial(_mm_kernel, n_in=n_in, has_res=res is not None, has_side=has_side),
        grid=(n_size // tn, m // tm),
        in_specs=specs,
        out_specs=out_specs,
        out_shape=out_shape,
        compiler_params=_cparams(2),
        name="matmul",
    )(*args)
    return tuple(out) if has_side else out[0]


def _headnorm_kernel(x_ref, g_ref, o_ref, *, heads):
    g = g_ref[...]
    for h in range(heads):
        cols = slice(h * HEAD, (h + 1) * HEAD)
        x = x_ref[:, cols]
        ms = jnp.mean(x * x, axis=-1, keepdims=True)
        o_ref[:, cols] = x * lax.rsqrt(ms + EPS) * g


def headnorm(x2d, col_off, width, gain, tm, cw):
    m = x2d.shape[0]
    assert col_off % cw == 0 and width % cw == 0 and m % tm == 0
    return pl.pallas_call(
        functools.partial(_headnorm_kernel, heads=cw // HEAD),
        grid=(m // tm, width // cw),
        in_specs=[pl.BlockSpec((tm, cw), lambda i, c: (i, c + col_off // cw)),
                  pl.BlockSpec((1, HEAD), lambda i, c: (0, 0))],
        out_specs=pl.BlockSpec((tm, cw), lambda i, c: (i, c)),
        out_shape=jax.ShapeDtypeStruct((m, width), F32),
        compiler_params=_cparams(2),
        name="headnorm",
    )(x2d, gain.reshape(1, HEAD))


def _odd_qkv_kernel(q_ref, k_ref, v_ref, qg_ref, kg_ref, k_prev_hbm, v_prev_hbm, qo_ref, ko_ref, vo_ref, *, heads):
    del k_prev_hbm, v_prev_hbm
    qg = qg_ref[...]
    kg = kg_ref[...]
    for h in range(heads):
        cols = slice(h * HEAD, (h + 1) * HEAD)
        q = q_ref[:, cols]
        qo_ref[:, cols] = q * lax.rsqrt(jnp.mean(q * q, axis=-1, keepdims=True) + EPS) * qg
        k = k_ref[:, cols]
        ko_ref[:, cols] = k * lax.rsqrt(jnp.mean(k * k, axis=-1, keepdims=True) + EPS) * kg
    vo_ref[...] = v_ref[...]


def odd_qkv(proj, w_c, q_gain, k_gain, layer, n_layers, k_prev, v_prev, tm, cw):
    m = proj.shape[0]
    assert w_c % cw == 0 and m % tm == 0 and k_prev.shape == v_prev.shape == (n_layers, m, w_c)
    nb = w_c // cw
    blk = (tm, cw)
    any_spec = pl.BlockSpec(memory_space=pl.ANY)
    stack = jax.ShapeDtypeStruct((n_layers, m, w_c), F32)
    return pl.pallas_call(
        functools.partial(_odd_qkv_kernel, heads=cw // HEAD),
        grid=(m // tm, nb),
        in_specs=[pl.BlockSpec(blk, lambda i, c: (i, c)),
                  pl.BlockSpec(blk, lambda i, c: (i, c + nb)),
                  pl.BlockSpec(blk, lambda i, c: (i, c + 2 * nb)),
                  pl.BlockSpec((1, HEAD), lambda i, c: (0, 0)),
                  pl.BlockSpec((1, HEAD), lambda i, c: (0, 0)),
                  any_spec, any_spec],
        out_specs=[pl.BlockSpec(blk, lambda i, c: (i, c)),
                   pl.BlockSpec((None,) + blk, lambda i, c: (layer, i, c)),
                   pl.BlockSpec((None,) + blk, lambda i, c: (layer, i, c))],
        out_shape=[jax.ShapeDtypeStruct((m, w_c), F32), stack, stack],
        input_output_aliases={5: 1, 6: 2},
        compiler_params=_cparams(2),
        name="odd_qkv",
    )(proj, proj, proj, q_gain.reshape(1, HEAD), k_gain.reshape(1, HEAD), k_prev, v_prev)


def _gate_kernel(x_ref, alog_ref, dt_ref, beta_ref, gam_ref, gsm_ref, tri_ref, eb_ref, eg_ref, *, n_heads, cs):
    tm = x_ref.shape[0]
    wide = n_heads * HEAD

    @pl.when(pl.program_id(0) == 0)
    def _():
        ri = lax.broadcasted_iota(jnp.int32, (tm, tm), 0)
        ci = lax.broadcasted_iota(jnp.int32, (tm, tm), 1)
        tri_ref[...] = jnp.where((ci <= ri) & (ci >= (ri // cs) * cs), 1.0, 0.0).astype(BF16)
        src = lax.broadcasted_iota(jnp.int32, (HEAD, wide), 0)
        dst = lax.broadcasted_iota(jnp.int32, (HEAD, wide), 1) // HEAD
        eb_ref[...] = jnp.where(src == dst, 1.0, 0.0).astype(BF16)
        eg_ref[...] = jnp.where(src == dst + n_heads, 1.0, 0.0).astype(BF16)

    def parts(v):
        p0 = v.astype(BF16)
        r1 = v - p0.astype(F32)
        p1 = r1.astype(BF16)
        return p0, p1, (r1 - p1.astype(F32)).astype(BF16)

    x = x_ref[...]
    beta = jax.nn.sigmoid(x)
    g = -jnp.exp(alog_ref[...]) * jax.nn.softplus(x + dt_ref[...])
    tri = tri_ref[...]
    gam = functools.reduce(lambda a, b: a + b, [_dot(tri, p) for p in reversed(parts(g))])
    gsm_ref[...] = gam
    beta_ref[...] = functools.reduce(lambda a, b: a + b, [_dot(p, eb_ref[...]) for p in reversed(parts(beta))])
    gam_ref[...] = functools.reduce(lambda a, b: a + b, [_dot(p, eg_ref[...]) for p in reversed(parts(gam))])


def gdn_gates(small2d, a_log, dt_bias, tm, cs):
    m = small2d.shape[0]
    n_heads = a_log.shape[0]
    assert tm % cs == 0 and m % tm == 0
    pad = HEAD - 2 * n_heads
    alog_row = jnp.concatenate([jnp.zeros((n_heads,), F32), a_log, jnp.zeros((pad,), F32)]).reshape(1, HEAD)
    dt_row = jnp.concatenate([jnp.zeros((n_heads,), F32), dt_bias, jnp.zeros((pad,), F32)]).reshape(1, HEAD)
    wide = n_heads * HEAD
    return pl.pallas_call(
        functools.partial(_gate_kernel, n_heads=n_heads, cs=cs),
        grid=(m // tm,),
        in_specs=[pl.BlockSpec((tm, HEAD), lambda i: (i, 0)),
                  pl.BlockSpec((1, HEAD), lambda i: (0, 0)),
                  pl.BlockSpec((1, HEAD), lambda i: (0, 0))],
        out_specs=[pl.BlockSpec((tm, wide), lambda i: (i, 0)),
                   pl.BlockSpec((tm, wide), lambda i: (i, 0)),
                   pl.BlockSpec((tm, HEAD), lambda i: (i, 0))],
        out_shape=[jax.ShapeDtypeStruct((m, wide), F32), jax.ShapeDtypeStruct((m, wide), F32),
                   jax.ShapeDtypeStruct((m, HEAD), F32)],
        scratch_shapes=[pltpu.VMEM((tm, tm), BF16), pltpu.VMEM((HEAD, wide), BF16), pltpu.VMEM((HEAD, wide), BF16)],
        compiler_params=_cparams(1),
        name="gdn_gates",
    )(small2d, alog_row, dt_row)


def _sconv_kernel(bg_ref, cg_ref, xb_ref, zb_ref, hc_ref, hx_ref, st_ref, w_ref, o_ref, nb_ref, buf_ref, *, tt):
    t = pl.program_id(2)
    buf_ref[0:SUBLANES, :] = jnp.where(t == 0, st_ref[0], hc_ref[0] * hx_ref[0])
    buf_ref[SUBLANES:SUBLANES + tt, :] = cg_ref[0] * xb_ref[0]
    w = w_ref[...]
    first = SUBLANES - (CONV_B - 1)
    y = buf_ref[first:first + tt, :] * w[0:1, :]
    for j in range(1, CONV_B):
        y = y + buf_ref[first + j:first + j + tt, :] * w[j:j + 1, :]
    o_ref[0] = (bg_ref[0] * y * _silu(zb_ref[0])).astype(o_ref.dtype)
    nb_ref[0] = buf_ref[tt:tt + SUBLANES, :]


def short_conv(proj3d, off_b, off_c, off_x, off_z, width, state_pad, conv_w, tt, cw):
    bn, t, _ = proj3d.shape
    assert t % tt == 0 and width % cw == 0
    rb = tt // SUBLANES

    def col(off):
        assert off % cw == 0
        return functools.partial(lambda b, c, i, o: (b, i, c + o), o=off // cw)

    def halo(off):
        return functools.partial(lambda b, c, i, o: (b, jnp.maximum(i * rb - 1, 0), c + o), o=off // cw)

    blk = (1, tt, cw)
    hblk = (1, SUBLANES, cw)
    return pl.pallas_call(
        functools.partial(_sconv_kernel, tt=tt),
        grid=(bn, width // cw, t // tt),
        in_specs=[pl.BlockSpec(blk, col(off_b)), pl.BlockSpec(blk, col(off_c)),
                  pl.BlockSpec(blk, col(off_x)), pl.BlockSpec(blk, col(off_z)),
                  pl.BlockSpec(hblk, halo(off_c)), pl.BlockSpec(hblk, halo(off_x)),
                  pl.BlockSpec(hblk, lambda b, c, i: (b, 0, c)),
                  pl.BlockSpec((CONV_B, cw), lambda b, c, i: (0, c))],
        out_specs=[pl.BlockSpec(blk, lambda b, c, i: (b, i, c)),
                   pl.BlockSpec(hblk, lambda b, c, i: (b, 0, c))],
        out_shape=[jax.ShapeDtypeStruct((bn, t, width), BF16),
                   jax.ShapeDtypeStruct((bn, SUBLANES, width), F32)],
        scratch_shapes=[pltpu.VMEM((tt + SUBLANES, cw), F32)],
        compiler_params=_cparams(3),
        name="short_conv",
    )(proj3d, proj3d, proj3d, proj3d, proj3d, proj3d, state_pad, conv_w)


def _split_bf16(x):
    hi = x.astype(BF16)
    return hi, (x - hi.astype(F32)).astype(BF16)


def _dot_3pass(a, b):
    a_hi, a_lo = _split_bf16(a)
    b_hi, b_lo = _split_bf16(b)
    return _dot(a_hi, b_hi) + (_dot(a_hi, b_lo) + _dot(a_lo, b_hi))


def _gdn_kernel(xq_ref, xk_ref, xv_ref, wq_ref, wk_ref, wv_ref, cq_ref, ck_ref, cv_ref, z_ref, beta_ref, gam_ref,
                grow_ref, s0_ref, gain_ref, o_ref, s_ref, xbuf, act, carry, *, hb, cs, nct):
    ti = pl.program_id(2)
    tt = nct * cs

    @pl.when(ti == 0)
    def _():
        s_ref[...] = s0_ref[...]
        for a, c_ref in enumerate((cq_ref, ck_ref, cv_ref)):
            carry[a] = c_ref[0]

    first = SUBLANES - (CONV_A - 1)
    for a, (x_ref, w_ref) in enumerate(((xq_ref, wq_ref), (xk_ref, wk_ref), (xv_ref, wv_ref))):
        xbuf[a, 0:SUBLANES, :] = carry[a]
        xbuf[a, SUBLANES:SUBLANES + tt, :] = x_ref[0]
        carry[a] = x_ref[0, tt - SUBLANES:tt, :]
        w = w_ref[...]
        y = xbuf[a, first:first + tt, :] * w[0:1, :]
        for j in range(1, CONV_A):
            y = y + xbuf[a, first + j:first + j + tt, :] * w[j:j + 1, :]
        y = _silu(y)
        if a == 2:
            act[a] = y
        else:
            scale = HEAD ** -0.5 if a == 0 else 1.0
            for h in range(hb):
                yh = y[:, h * HEAD:(h + 1) * HEAD]
                ss = jnp.sum(yh * yh, axis=-1, keepdims=True)
                act[a, :, h * HEAD:(h + 1) * HEAD] = yh * lax.rsqrt(ss + EPS) * scale

    ii = lax.broadcasted_iota(jnp.int32, (cs, cs), 0)
    jj = lax.broadcasted_iota(jnp.int32, (cs, cs), 1)
    causal = ii >= jj
    strict = ii > jj
    eye_f = (ii == jj).astype(F32)
    gain = gain_ref[...]
    n_double = int(math.log2(cs)) - 1
    chains = [(j, h) for j in range(nct) for h in range(hb)]

    def tile(ref, c):
        return ref[0, c[0] * cs:(c[0] + 1) * cs, c[1] * HEAD:(c[1] + 1) * HEAD]

    def act_tile(a, c):
        return act[a, c[0] * cs:(c[0] + 1) * cs, c[1] * HEAD:(c[1] + 1) * HEAD]

    k = [act_tile(1, c) for c in chains]
    beta = [tile(beta_ref, c) for c in chains]
    gam = [tile(gam_ref, c) for c in chains]
    kb = [a * b for a, b in zip(k, beta)]
    raw = [_dot_nt(jnp.concatenate([b, act_tile(0, c)], axis=0).astype(BF16), a.astype(BF16))
           for a, b, c in zip(k, kb, chains)]
    decay = []
    for g, (j, h) in zip(gam, chains):
        gam_row = grow_ref[0, h, pl.ds(ti * nct + j, 1), :]
        decay.append(jnp.where(causal, jnp.exp(jnp.where(causal, g[:, :cs] - gam_row, 0.0)), 0.0))
    qk = [jnp.where(causal, r[cs:] * d, 0.0) for r, d in zip(raw, decay)]
    pw = [jnp.where(strict, -(r[:cs] * d), 0.0) for r, d in zip(raw, decay)]
    inv = [eye_f + p for p in pw]
    pw = [_dot(p.astype(BF16), p.astype(BF16)) for p in pw]
    for _ in range(n_double - 1):
        prod = [_dot(jnp.concatenate([p, x], axis=0).astype(BF16), p.astype(BF16)) for p, x in zip(pw, inv)]
        pw = [p[:cs] for p in prod]
        inv = [x + p[cs:] for x, p in zip(inv, prod)]
    inv = [x + _dot(x.astype(BF16), p.astype(BF16)) for x, p in zip(inv, pw)]
    e_gam = [jnp.exp(g) for g in gam]
    uw = [_dot_3pass(x, jnp.concatenate([act_tile(2, c) * b, a * e], axis=1))
          for x, c, b, a, e in zip(inv, chains, beta, kb, e_gam)]
    wq = [jnp.concatenate([x[:, HEAD:], act_tile(0, c) * e], axis=0).astype(BF16)
          for x, c, e in zip(uw, chains, e_gam)]
    k_dec_t = [(a * jnp.exp(g[cs - 1:cs, :] - g)).T.astype(BF16) for a, g in zip(k, gam)]
    state = [s_ref[0, h] for h in range(hb)]
    for j in range(nct):
        idx = [j * hb + h for h in range(hb)]
        ws = [_dot(wq[i], state[h].astype(BF16)) for h, i in enumerate(idx)]
        v16 = [(uw[i][:, :HEAD] - x[:cs]).astype(BF16) for x, i in zip(ws, idx)]
        o = [x[cs:] + _dot(qk[i].astype(BF16), y) for x, y, i in zip(ws, v16, idx)]
        state = [s * jnp.exp(gam[i][cs - 1:cs, :]) + _dot(k_dec_t[i], y) for s, y, i in zip(state, v16, idx)]
        for h, i in enumerate(idx):
            ms = jnp.mean(o[h] * o[h], axis=-1, keepdims=True)
            on = o[h] * lax.rsqrt(ms + EPS) * gain
            o_ref[0, j * cs:(j + 1) * cs, h * HEAD:(h + 1) * HEAD] = (
                on * _silu(tile(z_ref, (j, h)))).astype(o_ref.dtype)
    for h in range(hb):
        s_ref[0, h] = state[h]


def gdn(proj3d, z_off, conv_w, conv_state, beta_rep, gam_rep, gam_row4, s0, gain, hb, tt):
    bn, t, _ = proj3d.shape
    n_heads = s0.shape[1]
    nc, cs = gam_row4.shape[2], gam_row4.shape[3]
    cw = hb * HEAD
    ng = n_heads // hb
    assert z_off % cw == 0 and n_heads % hb == 0 and t % tt == 0 and tt % cs == 0 and tt % SUBLANES == 0
    blk = (1, tt, cw)
    sblk = (1, hb, HEAD, HEAD)
    qkv_specs = [pl.BlockSpec(blk, functools.partial(lambda b, h, i, a: (b, i, a * ng + h), a=a)) for a in range(3)]
    w_specs = [pl.BlockSpec((CONV_A, cw), functools.partial(lambda b, h, i, a: (0, a * ng + h), a=a)) for a in range(3)]
    c_specs = [pl.BlockSpec((1, SUBLANES, cw), functools.partial(lambda b, h, i, a: (b, 0, a * ng + h), a=a))
               for a in range(3)]
    return pl.pallas_call(
        functools.partial(_gdn_kernel, hb=hb, cs=cs, nct=tt // cs),
        grid=(bn, ng, t // tt),
        in_specs=qkv_specs + w_specs + c_specs + [
                  pl.BlockSpec(blk, lambda b, h, i: (b, i, z_off // cw + h)),
                  pl.BlockSpec(blk, lambda b, h, i: (b, i, h)),
                  pl.BlockSpec(blk, lambda b, h, i: (b, i, h)),
                  pl.BlockSpec((1, hb, nc, cs), lambda b, h, i: (b, h, 0, 0)),
                  pl.BlockSpec(sblk, lambda b, h, i: (b, h, 0, 0)),
                  pl.BlockSpec((1, HEAD), lambda b, h, i: (0, 0))],
        out_specs=[pl.BlockSpec(blk, lambda b, h, i: (b, i, h)),
                   pl.BlockSpec(sblk, lambda b, h, i: (b, h, 0, 0))],
        out_shape=[jax.ShapeDtypeStruct((bn, t, n_heads * HEAD), BF16),
                   jax.ShapeDtypeStruct((bn, n_heads, HEAD, HEAD), F32)],
        scratch_shapes=[pltpu.VMEM((3, tt + SUBLANES, cw), F32), pltpu.VMEM((3, tt, cw), F32),
                        pltpu.VMEM((3, SUBLANES, cw), F32)],
        compiler_params=_cparams(3),
        name="gdn",
    )(proj3d, proj3d, proj3d, conv_w, conv_w, conv_w, conv_state, conv_state, conv_state, proj3d, beta_rep,
      gam_rep, gam_row4, s0, gain.reshape(1, HEAD))


def _mem_attn_kernel(q_ref, z_ref, mk_ref, mv_ref, gain_ref, o_ref):
    gain = gain_ref[...]
    for h in range(H_M):
        cols = slice(h * HEAD, (h + 1) * HEAD)
        q = q_ref[0, :, cols]
        ms = jnp.mean(q * q, axis=-1, keepdims=True)
        qn = (q * lax.rsqrt(ms + EPS) * gain).astype(BF16)
        sc = _dot_nt(qn, mk_ref[0, :, cols].astype(BF16)) * (HEAD ** -0.5)
        m = jnp.max(sc, axis=-1, keepdims=True)
        p = jnp.exp(sc - m)
        p = p / jnp.sum(p, axis=-1, keepdims=True)
        o = _dot(p.astype(BF16), mv_ref[0, :, cols].astype(BF16))
        o_ref[0, :, cols] = (o * _silu(z_ref[0, :, cols])).astype(o_ref.dtype)


def mem_attn(proj3d, q_off, z_off, mem_k, mem_v, q_gain, tt):
    bn, t, _ = proj3d.shape
    n_mem = mem_k.shape[1]
    wm = H_M * HEAD
    assert q_off % wm == 0 and z_off % wm == 0 and t % tt == 0
    return pl.pallas_call(
        _mem_attn_kernel,
        grid=(bn, t // tt),
        in_specs=[pl.BlockSpec((1, tt, wm), lambda b, i: (b, i, q_off // wm)),
                  pl.BlockSpec((1, tt, wm), lambda b, i: (b, i, z_off // wm)),
                  pl.BlockSpec((1, n_mem, wm), lambda b, i: (b, 0, 0)),
                  pl.BlockSpec((1, n_mem, wm), lambda b, i: (b, 0, 0)),
                  pl.BlockSpec((1, HEAD), lambda b, i: (0, 0))],
        out_specs=pl.BlockSpec((1, tt, wm), lambda b, i: (b, i, 0)),
        out_shape=jax.ShapeDtypeStruct((bn, t, wm), BF16),
        compiler_params=_cparams(2),
        name="mem_attn",
    )(proj3d, proj3d, mem_k.reshape(bn, n_mem, wm), mem_v.reshape(bn, n_mem, wm), q_gain.reshape(1, HEAD))


def _dil_prompt_kernel(q_ref, k_ref, v_ref, z_ref, o_ref, acc_ref, lse_ref, *, hg, t, group):
    qi = lax.broadcasted_iota(jnp.int32, (SWA_BLOCK, SWA_BLOCK), 0)
    ki = lax.broadcasted_iota(jnp.int32, (SWA_BLOCK, SWA_BLOCK), 1)
    n_pat = len(SWA_PATTERNS)
    for h in range(hg):
        cols = slice(h * HEAD, (h + 1) * HEAD)
        for p, (window, dil) in enumerate(SWA_PATTERNS):
            assert window // dil == SWA_BLOCK
            l_sub = t // dil
            nb = l_sub // SWA_BLOCK
            assert nb * SWA_BLOCK * dil == t

            def rows_of(r, c, dil=dil):
                return pl.ds(r + c * (SWA_BLOCK * dil), SWA_BLOCK, stride=dil)

            blocks = [(r, c) for r in range(dil) for c in range(nb)]
            for g0 in range(0, len(blocks), group):
                grp = blocks[g0:g0 + group]
                q = [(q_ref[0, rows_of(r, c), cols] * (HEAD ** -0.5)).astype(BF16) for r, c in grp]
                s_cur = [jnp.where(ki <= qi, _dot_nt(x, k_ref[0, rows_of(r, c), cols].astype(BF16)), -jnp.inf)
                         for x, (r, c) in zip(q, grp)]
                s_prev = [jnp.where(ki >= qi, _dot_nt(x, k_ref[0, rows_of(r, c - 1), cols].astype(BF16)), -jnp.inf)
                          if c > 0 else None for x, (r, c) in zip(q, grp)]
                m = [jnp.max(a, axis=-1, keepdims=True) if b is None else
                     jnp.maximum(jnp.max(a, axis=-1, keepdims=True), jnp.max(b, axis=-1, keepdims=True))
                     for a, b in zip(s_cur, s_prev)]
                e_cur = [jnp.exp(a - x) for a, x in zip(s_cur, m)]
                e_prev = [None if b is None else jnp.exp(b - x) for b, x in zip(s_prev, m)]
                den = [jnp.sum(a, axis=-1, keepdims=True) if b is None else
                       jnp.sum(a, axis=-1, keepdims=True) + jnp.sum(b, axis=-1, keepdims=True)
                       for a, b in zip(e_cur, e_prev)]
                o = [_dot(a.astype(BF16), v_ref[0, rows_of(r, c), cols].astype(BF16)) for a, (r, c) in zip(e_cur, grp)]
                o = [x if b is None else x + _dot(b.astype(BF16), v_ref[0, rows_of(r, c - 1), cols].astype(BF16))
                     for x, b, (r, c) in zip(o, e_prev, grp)]
                for x, d, mx, (r, c) in zip(o, den, m, grp):
                    acc_ref[p, rows_of(r, c), :] = x / d
                    lse_ref[p, rows_of(r, c), :] = jnp.broadcast_to(mx + jnp.log(d), (SWA_BLOCK, HEAD))
        lse = [lse_ref[p] for p in range(n_pat)]
        top = functools.reduce(jnp.maximum, lse)
        wgt = [jnp.exp(x - top) for x in lse]
        tot = functools.reduce(lambda a, b: a + b, wgt)
        mix = functools.reduce(lambda a, b: a + b, [(wgt[p] / tot) * acc_ref[p] for p in range(n_pat)])
        o_ref[0, :, cols] = (mix * _silu(z_ref[0, :, cols])).astype(o_ref.dtype)


def dil_prompt(qn, k_stack, v_stack, layer, proj3d, z_off, hg):
    bn, t, wc = qn.shape
    cw = hg * HEAD
    assert z_off % cw == 0 and wc % cw == 0
    blk = (1, t, cw)
    sblk = (None, 1, t, cw)
    return pl.pallas_call(
        functools.partial(_dil_prompt_kernel, hg=hg, t=t, group=8),
        grid=(bn, wc // cw),
        in_specs=[pl.BlockSpec(blk, lambda b, h: (b, 0, h)),
                  pl.BlockSpec(sblk, lambda b, h: (layer, b, 0, h)),
                  pl.BlockSpec(sblk, lambda b, h: (layer, b, 0, h)),
                  pl.BlockSpec(blk, lambda b, h: (b, 0, z_off // cw + h))],
        out_specs=pl.BlockSpec(blk, lambda b, h: (b, 0, h)),
        out_shape=jax.ShapeDtypeStruct((bn, t, wc), BF16),
        scratch_shapes=[pltpu.VMEM((len(SWA_PATTERNS), t, HEAD), F32),
                        pltpu.VMEM((len(SWA_PATTERNS), t, HEAD), F32)],
        compiler_params=_cparams(2),
        name="dil_prompt",
    )(qn, k_stack, v_stack, proj3d)


MASKED_MAX = -1e30


def _dil_sample_kernel(q_ref, kn_ref, vn_ref, z_ref, kc_ref, vc_ref, o_ref, m_ref, l_ref, acc_ref,
                       *, n_heads, t, wb, tw, group):
    j = pl.program_id(1)
    n_pat = len(SWA_PATTERNS)

    @pl.when(j == 0)
    def _():
        m_ref[...] = jnp.full(m_ref.shape, MASKED_MAX, F32)
        l_ref[...] = jnp.zeros(l_ref.shape, F32)
        acc_ref[...] = jnp.zeros(acc_ref.shape, F32)

    def head_rows(ref, h):
        return ref[0, pl.ds(h, tw, stride=n_heads), :].astype(BF16)

    def q16(h):
        return (q_ref[0, :, h * HEAD:(h + 1) * HEAD] * (HEAD ** -0.5)).astype(BF16)

    def update(hs, p, scores, ok, values):
        idx = [h * n_pat + p for h in hs]
        s = [jnp.where(ok, x, -jnp.inf) for x in scores]
        m_old = [m_ref[i] for i in idx]
        m_new = [jnp.maximum(a, jnp.max(x, axis=-1, keepdims=True)) for a, x in zip(m_old, s)]
        alpha = [jnp.exp(a - b) for a, b in zip(m_old, m_new)]
        e = [jnp.exp(x - b[:, :1]) for x, b in zip(s, m_new)]
        pv = [_dot(x.astype(BF16), v) for x, v in zip(e, values)]
        for i, a, x, y, b in zip(idx, alpha, e, pv, m_new):
            l_ref[i] = l_ref[i] * a + jnp.sum(x, axis=-1, keepdims=True)
            acc_ref[i] = acc_ref[i] * a + y
            m_ref[i] = b

    row = lax.broadcasted_iota(jnp.int32, (t, tw), 0)
    col = lax.broadcasted_iota(jnp.int32, (t, tw), 1)
    delta = wb + row - (j * tw + col)
    nearest = wb - (j + 1) * tw + 1
    for p, (window, dil) in enumerate(SWA_PATTERNS):
        def tile_update(p=p, window=window, dil=dil):
            ok = ((delta & (dil - 1)) == 0) & (delta <= window)
            for g0 in range(0, n_heads, group):
                hs = list(range(g0, g0 + group))
                scores = [_dot_nt(q16(h), head_rows(kc_ref, h)) for h in hs]
                update(hs, p, scores, ok, [head_rows(vc_ref, h) for h in hs])
        if window >= wb - tw + 1:
            tile_update()
        else:
            pl.when(nearest <= window)(tile_update)

    @pl.when(j == pl.num_programs(1) - 1)
    def _():
        dn = lax.broadcasted_iota(jnp.int32, (t, t), 0) - lax.broadcasted_iota(jnp.int32, (t, t), 1)
        for g0 in range(0, n_heads, group):
            hs = list(range(g0, g0 + group))
            cols = [slice(h * HEAD, (h + 1) * HEAD) for h in hs]
            scores = [_dot_nt(q16(h), kn_ref[0, :, c].astype(BF16)) for h, c in zip(hs, cols)]
            values = [vn_ref[0, :, c].astype(BF16) for c in cols]
            for p, (window, dil) in enumerate(SWA_PATTERNS):
                ok = (dn >= 0) & ((dn & (dil - 1)) == 0) & (dn <= window)
                update(hs, p, scores, ok, values)
            for h, c in zip(hs, cols):
                idx = [h * n_pat + p for p in range(n_pat)]
                lse = [m_ref[i] + jnp.log(l_ref[i]) for i in idx]
                top = functools.reduce(jnp.maximum, lse)
                wgt = [jnp.exp(x - top) for x in lse]
                tot = functools.reduce(lambda a, b: a + b, wgt)
                mix = functools.reduce(lambda a, b: a + b,
                                       [(w / tot) * (acc_ref[i] / l_ref[i]) for w, i in zip(wgt, idx)])
                o_ref[0, :, c] = (mix * _silu(z_ref[0, :, c])).astype(o_ref.dtype)


def dil_sample(qn, k_stack, v_stack, layer, proj3d, z_off, cache_k, cache_v, n_heads, tw):
    bn, t, wc = qn.shape
    wb = cache_k.shape[2] // n_heads
    assert z_off % wc == 0 and wb % tw == 0 and wc == n_heads * HEAD
    for window, dil in SWA_PATTERNS:
        assert dil & (dil - 1) == 0
    blk = (1, t, wc)
    sblk = (None, 1, t, wc)
    cblk = (None, 1, tw * n_heads, HEAD)
    state = pltpu.VMEM((n_heads * len(SWA_PATTERNS), t, HEAD), F32)
    return pl.pallas_call(
        functools.partial(_dil_sample_kernel, n_heads=n_heads, t=t, wb=wb, tw=tw, group=8),
        grid=(bn, wb // tw),
        in_specs=[pl.BlockSpec(blk, lambda b, j: (b, 0, 0)),
                  pl.BlockSpec(sblk, lambda b, j: (layer, b, 0, 0)),
                  pl.BlockSpec(sblk, lambda b, j: (layer, b, 0, 0)),
                  pl.BlockSpec(blk, lambda b, j: (b, 0, z_off // wc)),
                  pl.BlockSpec(cblk, lambda b, j: (layer, b, j, 0)),
                  pl.BlockSpec(cblk, lambda b, j: (layer, b, j, 0))],
        out_specs=pl.BlockSpec(blk, lambda b, j: (b, 0, 0)),
        out_shape=jax.ShapeDtypeStruct((bn, t, wc), BF16),
        scratch_shapes=[state, state, state],
        compiler_params=_cparams(2),
        name="dil_sample",
    )(qn, k_stack, v_stack, proj3d, cache_k, cache_v)


CACHE_ROLL_CHUNK_BYTES = 4 * 1024 * 1024


def _cache_roll_kernel(ck_hbm, cv_hbm, kn_ref, vn_ref, ok_hbm, ov_hbm, buf, stage, in_sems, out_sems, new_sems,
                       *, n_heads, t, chunk, n_chunks):
    l = pl.program_id(0)
    b = pl.program_id(1)
    new = t * n_heads
    srcs = (ck_hbm, cv_hbm)
    dsts = (ok_hbm, ov_hbm)

    def read(a, c):
        return pltpu.make_async_copy(srcs[a].at[l, b, pl.ds(new + c * chunk, chunk)], buf.at[a, c % 2],
                                     in_sems.at[a, c % 2])

    def write(a, c):
        return pltpu.make_async_copy(buf.at[a, c % 2], dsts[a].at[l, b, pl.ds(c * chunk, chunk)],
                                     out_sems.at[a, c % 2])

    def write_new(a):
        return pltpu.make_async_copy(stage.at[a], dsts[a].at[l, b, pl.ds(n_chunks * chunk, new)], new_sems.at[a])

    for a in range(2):
        read(a, 0).start()
    for h in range(n_heads):
        stage[0, pl.ds(h, t, stride=n_heads), :] = kn_ref[:, h * HEAD:(h + 1) * HEAD]
        stage[1, pl.ds(h, t, stride=n_heads), :] = vn_ref[:, h * HEAD:(h + 1) * HEAD]
    for a in range(2):
        write_new(a).start()
    for c in range(n_chunks):
        for a in range(2):
            read(a, c).wait()
            write(a, c).start()
            if c + 1 < n_chunks:
                if c >= 1:
                    write(a, c - 1).wait()
                read(a, c + 1).start()
    for a in range(2):
        if n_chunks >= 2:
            write(a, n_chunks - 2).wait()
        write(a, n_chunks - 1).wait()
        write_new(a).wait()


def cache_roll(cache_k, cache_v, k_new, v_new, n_heads):
    n_l, bn, rows, _ = cache_k.shape
    t = k_new.shape[2]
    wb = rows // n_heads
    keep_tokens = wb - t
    token_bytes = n_heads * HEAD * cache_k.dtype.itemsize
    chunk_tokens = max(d for d in range(1, keep_tokens + 1)
                       if keep_tokens % d == 0 and d * token_bytes <= CACHE_ROLL_CHUNK_BYTES)
    chunk = chunk_tokens * n_heads
    nblk = (None, None, t, n_heads * HEAD)
    any_spec = pl.BlockSpec(memory_space=pl.ANY)
    out = jax.ShapeDtypeStruct(cache_k.shape, cache_k.dtype)
    return pl.pallas_call(
        functools.partial(_cache_roll_kernel, n_heads=n_heads, t=t, chunk=chunk, n_chunks=keep_tokens // chunk_tokens),
        grid=(n_l, bn),
        in_specs=[any_spec, any_spec,
                  pl.BlockSpec(nblk, lambda l, b: (l, b, 0, 0)),
                  pl.BlockSpec(nblk, lambda l, b: (l, b, 0, 0))],
        out_specs=[any_spec, any_spec],
        out_shape=[out, out],
        scratch_shapes=[pltpu.VMEM((2, 2, chunk, HEAD), cache_k.dtype),
                        pltpu.VMEM((2, t * n_heads, HEAD), cache_k.dtype),
                        pltpu.SemaphoreType.DMA((2, 2)), pltpu.SemaphoreType.DMA((2, 2)),
                        pltpu.SemaphoreType.DMA((2,))],
        compiler_params=_cparams(2),
        name="cache_roll",
    )(cache_k, cache_v, k_new, v_new)


def _row_tile(m, pref):
    return pref if m % pref == 0 else m


def _flat(x):
    return x.reshape(x.shape[0] * x.shape[1], x.shape[2])


def _in_proj(x_big, x_small, g_norm, w_stacks, layer):
    xb, xs = _flat(x_big), _flat(x_small)
    hb = rms_cast(xb, g_norm, _row_tile(xb.shape[0], 256))
    hs = rms_cast(xs, g_norm, _row_tile(xs.shape[0], 256))
    tm = _row_tile(xb.shape[0], 1024)
    return [matmul([hb], w, layer, 0, w.shape[2], tm, min(1024, w.shape[2]), side_a=[hs]) for w in w_stacks]


def _out_proj(x_big, x_small, parts_big, parts_small, w_out, layer):
    xb, xs = _flat(x_big), _flat(x_small)
    yb, ys = matmul([_flat(p) for p in parts_big], w_out, layer, 0, xb.shape[1], _row_tile(xb.shape[0], 512), 1024,
                    res=xb, side_a=[_flat(p) for p in parts_small], side_res=xs)
    return yb.reshape(x_big.shape), ys.reshape(x_small.shape)


def _pad_state(buf):
    return jnp.pad(buf, ((0, 0), (SUBLANES - buf.shape[1], 0), (0, 0)))


def _even_branches(shape, proj_a, proj_b, small, conv_a_buf, s0, conv_b_buf, mem_k, mem_v, conv_a_w, a_log, dt_bias,
                   gdn_gain, conv_b_w, q_gain_m):
    bn, t, _ = shape
    m = bn * t
    n_heads = a_log.shape[0]
    w_a = n_heads * HEAD
    w_b = conv_b_w.shape[1]
    w_m = H_M * HEAD
    pa = proj_a.reshape(bn, t, proj_a.shape[1])
    pb = proj_b.reshape(bn, t, proj_b.shape[1])
    tt = _row_tile(t, 512)
    cs = min(GDN_CHUNK, t)
    nc = t // cs
    beta_rep, gam_rep, gam_small = gdn_gates(small, a_log, dt_bias, _row_tile(m, 512), cs)
    gam_row4 = (gam_small.reshape(bn, t, HEAD)[:, :, n_heads:2 * n_heads]
                .transpose(0, 2, 1).reshape(bn, n_heads, nc, cs))
    o_a, new_s = gdn(pa, 3 * w_a, conv_a_w, _pad_state(conv_a_buf), beta_rep.reshape(bn, t, w_a),
                     gam_rep.reshape(bn, t, w_a), gam_row4, s0, gdn_gain, 4, _row_tile(t, 2 * GDN_CHUNK))
    new_conv_a = pa[:, t - (CONV_A - 1):, :3 * w_a]
    o_b, nb8 = short_conv(pb, 0, w_b, 2 * w_b, 3 * w_b, w_b, _pad_state(conv_b_buf), conv_b_w, tt, 512)
    new_conv_b = nb8[:, SUBLANES - (CONV_B - 1):, :]
    o_m = mem_attn(pb, 4 * w_b, 4 * w_b + w_m, mem_k, mem_v, q_gain_m, tt)
    return (o_a, o_b, o_m), (new_conv_a, new_s, new_conv_b)


def even_layer(x_big, x_small, states_big, states_small, layer, g_norm, w_a, w_b, w_small, conv_a_w, a_log, dt_bias,
               gdn_gain, conv_b_w, w_out, q_gain_m):
    (pa_b, pa_s), (pb_b, pb_s), (sm_b, sm_s) = _in_proj(x_big, x_small, g_norm, (w_a, w_b, w_small), layer)
    wts = (conv_a_w, a_log, dt_bias, gdn_gain, conv_b_w, q_gain_m)
    parts_b, new_b = _even_branches(x_big.shape, pa_b, pb_b, sm_b, *states_big, *wts)
    parts_s, new_s = _even_branches(x_small.shape, pa_s, pb_s, sm_s, *states_small, *wts)
    y_b, y_s = _out_proj(x_big, x_small, parts_b, parts_s, w_out, layer)
    return y_b, y_s, new_b, new_s


def _odd_branches(shape, proj, cache_k, cache_v, mem_k, mem_v, layer, n_layers, k_prev, v_prev, q_gain_c, k_gain_c,
                  q_gain_m):
    bn, t, w_c = shape
    m = bn * t
    w_m = H_M * HEAD
    off_z = 3 * w_c
    off_qm = 4 * w_c
    proj3d = proj.reshape(bn, t, proj.shape[1])
    qn, k_stack, v_stack = odd_qkv(proj, w_c, q_gain_c, k_gain_c, layer, n_layers, k_prev, v_prev,
                                   _row_tile(m, 512), 512)
    qn = qn.reshape(bn, t, w_c)
    k4 = k_stack.reshape(n_layers, bn, t, w_c)
    v4 = v_stack.reshape(n_layers, bn, t, w_c)
    if cache_k is None:
        o_c = dil_prompt(qn, k4, v4, layer, proj3d, off_z, 1)
    else:
        o_c = dil_sample(qn, k4, v4, layer, proj3d, off_z, cache_k, cache_v, w_c // HEAD, 256)
    o_m = mem_attn(proj3d, off_qm, off_qm + w_m, mem_k, mem_v, q_gain_m, _row_tile(t, 512))
    return (o_c, o_m), k_stack, v_stack


def odd_layer(x_big, x_small, cache_k, cache_v, mem_big, mem_small, layer, n_layers, kv_big, kv_small, g_norm, w_main,
              q_gain_c, k_gain_c, w_out, q_gain_m):
    ((p_b, p_s),) = _in_proj(x_big, x_small, g_norm, (w_main,), layer)
    gains = (q_gain_c, k_gain_c, q_gain_m)
    parts_b, kb, vb = _odd_branches(x_big.shape, p_b, None, None, *mem_big, layer, n_layers, *kv_big, *gains)
    parts_s, ks, vs = _odd_branches(x_small.shape, p_s, cache_k, cache_v, *mem_small, layer, n_layers, *kv_small,
                                    *gains)
    y_b, y_s = _out_proj(x_big, x_small, parts_b, parts_s, w_out, layer)
    return y_b, y_s, (kb, vb), (ks, vs)


def mem_project(mem, g, w_kv_stack, layer, k_gain):
    bn, n_mem, d = mem.shape
    m = bn * n_mem
    w_m = H_M * HEAD
    hm = rms_cast(mem.reshape(m, d), g, _row_tile(m, 256))
    kv = matmul([hm], w_kv_stack, layer, 0, 2 * w_m, _row_tile(m, 512), w_m)
    k = headnorm(kv, 0, w_m, k_gain, _row_tile(m, 512), w_m)
    return k.reshape(bn, n_mem, H_M, HEAD), kv[:, w_m:].reshape(bn, n_mem, H_M, HEAD)


def _cast_kernel(x_ref, o_ref):
    o_ref[...] = x_ref[...].astype(o_ref.dtype)


def cast_bf16(w, n_size, tk, tn):
    n_l, k, _ = w.shape
    assert k % tk == 0 and n_size % tn == 0
    return pl.pallas_call(
        _cast_kernel,
        grid=(n_l, k // tk, n_size // tn),
        in_specs=[pl.BlockSpec((None, tk, tn), lambda l, i, j: (l, i, j))],
        out_specs=pl.BlockSpec((None, tk, tn), lambda l, i, j: (l, i, j)),
        out_shape=jax.ShapeDtypeStruct((n_l, k, n_size), BF16),
        compiler_params=_cparams(3),
        name="cast_bf16",
    )(w)


def _prep_even_weights(w_in, n_heads):
    cut = 4 * n_heads * HEAD
    w_a = w_in[:, :, :cut].astype(BF16)
    w_b = w_in[:, :, cut + 2 * n_heads:].astype(BF16)
    w_small = jnp.pad(w_in[:, :, cut:cut + 2 * n_heads], ((0, 0), (0, 0), (0, HEAD - 2 * n_heads))).astype(BF16)
    return w_a, w_b, w_small


def kernel(x_prompt, x_sample, state_gdn, state_gdn_conv, state_sconv, cache_swa_k, cache_swa_v, cache_mem_k, cache_mem_v, mem_prompt, norm_even, w_in_even, conv_a, a_log, dt_bias, gdn_norm, conv_b, w_out_even, norm_odd, w_in_odd, q_norm_c, k_norm_c, w_out_odd, mem_norm, w_mem_kv, q_norm_m, k_norm_m):
    bp, t_p, d = x_prompt.shape
    bs, t_s, _ = x_sample.shape
    depth = mem_norm.shape[0]
    n_odd = norm_odd.shape[0]
    n_heads_a = a_log.shape[1]
    n_heads_c = cache_swa_k.shape[3]
    wb = cache_swa_k.shape[2]
    w_even_a, w_even_b, w_even_small = _prep_even_weights(w_in_even, n_heads_a)
    w_odd_main = cast_bf16(w_in_odd, w_in_odd.shape[2], 1024, 1024)
    w_out_e = cast_bf16(w_out_even, d, 512, 2048)
    w_out_o = cast_bf16(w_out_odd, d, 512, 2048)
    w_kv = cast_bf16(w_mem_kv, w_mem_kv.shape[2], 1024, 1024)
    cache_k = cache_swa_k.reshape(n_odd, bs, wb * n_heads_c, HEAD)
    cache_v = cache_swa_v.reshape(n_odd, bs, wb * n_heads_c, HEAD)
    y_p, y_s = x_prompt, x_sample
    gdn_p, gdn_s, gconv_p, gconv_s, sconv_p, sconv_s = [], [], [], [], [], []
    memk_p, memv_p = [], []
    kv_p = tuple(jnp.zeros((n_odd, bp * t_p, d), F32) for _ in range(2))
    kv_s = tuple(jnp.zeros((n_odd, bs * t_s, d), F32) for _ in range(2))
    for i in range(depth):
        mk_p, mv_p = mem_project(mem_prompt, mem_norm[i], w_kv, i, k_norm_m[i])
        memk_p.append(mk_p)
        memv_p.append(mv_p)
        if i % 2 == 0:
            e = i // 2
            states_p = (jnp.zeros((bp,) + state_gdn_conv.shape[2:], state_gdn_conv.dtype),
                        jnp.zeros((bp,) + state_gdn.shape[2:], state_gdn.dtype),
                        jnp.zeros((bp,) + state_sconv.shape[2:], state_sconv.dtype), mk_p, mv_p)
            states_s = (state_gdn_conv[e], state_gdn[e], state_sconv[e], cache_mem_k[i], cache_mem_v[i])
            y_p, y_s, (ca_p, sa_p, cb_p), (ca_s, sa_s, cb_s) = even_layer(
                y_p, y_s, states_p, states_s, e, norm_even[e], w_even_a, w_even_b, w_even_small, conv_a[e], a_log[e],
                dt_bias[e], gdn_norm[e], conv_b[e], w_out_e, q_norm_m[i])
            gconv_p.append(ca_p)
            gdn_p.append(sa_p)
            sconv_p.append(cb_p)
            gconv_s.append(ca_s)
            gdn_s.append(sa_s)
            sconv_s.append(cb_s)
        else:
            o = i // 2
            y_p, y_s, kv_p, kv_s = odd_layer(
                y_p, y_s, cache_k, cache_v, (mk_p, mv_p), (cache_mem_k[i], cache_mem_v[i]), o, n_odd, kv_p, kv_s,
                norm_odd[o], w_odd_main, q_norm_c[o], k_norm_c[o], w_out_o, q_norm_m[i])
    (kp, vp), (ks, vs) = kv_p, kv_s
    n_keep = min(max(w for w, _ in SWA_PATTERNS), t_p)
    swa_k_p = kp.reshape(n_odd, bp, t_p, n_heads_c, HEAD)[:, :, t_p - n_keep:]
    swa_v_p = vp.reshape(n_odd, bp, t_p, n_heads_c, HEAD)[:, :, t_p - n_keep:]
    swa_k_s, swa_v_s = cache_roll(cache_k, cache_v, ks.reshape(n_odd, bs, t_s, d), vs.reshape(n_odd, bs, t_s, d),
                                  n_heads_c)
    swa_k_s = swa_k_s.reshape(n_odd, bs, wb, n_heads_c, HEAD)
    swa_v_s = swa_v_s.reshape(n_odd, bs, wb, n_heads_c, HEAD)
    return (y_p, y_s, jnp.stack(gdn_p), jnp.stack(gdn_s), jnp.stack(gconv_p), jnp.stack(gconv_s),
            jnp.stack(sconv_p), jnp.stack(sconv_s), swa_k_p, swa_k_s, swa_v_p, swa_v_s,
            jnp.stack(memk_p), jnp.stack(memv_p))
```

```python
import functools
import math

import jax
import jax.numpy as jnp
from jax import lax
from jax.experimental import pallas as pl
from jax.experimental.pallas import tpu as pltpu

F32 = jnp.float32
BF16 = jnp.bfloat16
EPS = 1e-6
HEAD = 128
SUBLANES = 8
CONV_A = 4
CONV_B = 3
GDN_CHUNK = 64
SWA_PATTERNS = ((128, 1), (512, 4), (2048, 16))
SWA_BLOCK = 128
H_M = 4
VMEM_LIMIT = 56 * 1024 * 1024


def _cparams(n_axes):
    return pltpu.CompilerParams(dimension_semantics=("arbitrary",) * n_axes,
                                vmem_limit_bytes=VMEM_LIMIT)


def _silu(x):
    return x * jax.nn.sigmoid(x)


def _dot(a, b, precision=None):
    return jnp.dot(a, b, preferred_element_type=F32, precision=precision)


def _dot_nt(a, b, precision=None):
    return lax.dot_general(a, b, (((1,), (1,)), ((), ())), preferred_element_type=F32,
                           precision=precision)


def _dot_tn(a, b, precision=None):
    return lax.dot_general(a, b, (((0,), (0,)), ((), ())), preferred_element_type=F32,
                           precision=precision)


def _rms_cast_kernel(x_ref, g_ref, o_ref):
    x = x_ref[...]
    ms = jnp.mean(x * x, axis=-1, keepdims=True)
    o_ref[...] = (x * lax.rsqrt(ms + EPS) * g_ref[...]).astype(o_ref.dtype)


def rms_cast(x2d, g, tm):
    m, d = x2d.shape
    return pl.pallas_call(
        _rms_cast_kernel,
        grid=(m // tm,),
        in_specs=[pl.BlockSpec((tm, d), lambda i: (i, 0)),
                  pl.BlockSpec((1, d), lambda i: (0, 0))],
        out_specs=pl.BlockSpec((tm, d), lambda i: (i, 0)),
        out_shape=jax.ShapeDtypeStruct((m, d), BF16),
        compiler_params=_cparams(1),
        name="rms_cast",
    )(x2d, g.reshape(1, d))


def _mm_kernel(*refs, n_in, has_res, has_side):
    n_out = 2 if has_side else 1
    ins, outs = refs[:-n_out], refs[-n_out:]
    w_refs = ins[n_in:2 * n_in]
    per_group = n_in + (1 if has_res else 0)

    def run(a_refs, res_ref, o_ref):
        acc = _dot(a_refs[0][...], w_refs[0][...])
        for a_ref, w_ref in zip(a_refs[1:], w_refs[1:]):
            acc = acc + _dot(a_ref[...], w_ref[...])
        if res_ref is not None:
            acc = acc + res_ref[...]
        o_ref[...] = acc.astype(o_ref.dtype)

    run(ins[:n_in], ins[2 * n_in] if has_res else None, outs[0])
    if has_side:
        side = ins[n_in + per_group:]

        @pl.when(pl.program_id(1) == 0)
        def _():
            run(side[:n_in], side[n_in] if has_res else None, outs[1])


def matmul(a_list, w, layer, n_off, n_size, tm, tn, res=None, side_a=None, side_res=None):
    m = a_list[0].shape[0]
    n_in = len(a_list)
    has_side = side_a is not None
    assert (side_res is not None) == (has_side and res is not None)
    in_specs, w_specs = [], []
    row = 0
    for a in a_list:
        k = a.shape[1]
        assert row % k == 0 and n_off % tn == 0 and n_size % tn == 0 and m % tm == 0
        in_specs.append(pl.BlockSpec((tm, k), lambda n, i: (i, 0)))
        w_specs.append(pl.BlockSpec((None, k, tn), functools.partial(
            lambda n, i, rb, nb: (layer, rb, n + nb), rb=row // k, nb=n_off // tn)))
        row += k
    args = list(a_list) + [w] * n_in
    specs = in_specs + w_specs
    out_specs = [pl.BlockSpec((tm, tn), lambda n, i: (i, n))]
    out_shape = [jax.ShapeDtypeStruct((m, n_size), F32)]
    if res is not None:
        specs.append(pl.BlockSpec((tm, tn), lambda n, i: (i, n)))
        args.append(res)
    if has_side:
        ms = side_a[0].shape[0]
        for a in side_a:
            specs.append(pl.BlockSpec((ms, a.shape[1]), lambda n, i: (0, 0)))
            args.append(a)
        if side_res is not None:
            specs.append(pl.BlockSpec((ms, tn), lambda n, i: (0, n)))
            args.append(side_res)
        out_specs.append(pl.BlockSpec((ms, tn), lambda n, i: (0, n)))
        out_shape.append(jax.ShapeDtypeStruct((ms, n_size), F32))
    out = pl.pallas_call(
        functools.partial(_mm_kernel, n_in=n_in, has_res=res is not None, has_side=has_side),
        grid=(n_size // tn, m // tm),
        in_specs=specs,
        out_specs=out_specs,
        out_shape=out_shape,
        compiler_params=_cparams(2),
        name="matmul",
    )(*args)
    return tuple(out) if has_side else out[0]


def _headnorm_kernel(x_ref, g_ref, o_ref, *, heads):
    g = g_ref[...]
    for h in range(heads):
        cols = slice(h * HEAD, (h + 1) * HEAD)
        x = x_ref[:, cols]
        ms = jnp.mean(x * x, axis=-1, keepdims=True)
        o_ref[:, cols] = x * lax.rsqrt(ms + EPS) * g


def headnorm(x2d, col_off, width, gain, tm, cw):
    m = x2d.shape[0]
    assert col_off % cw == 0 and width % cw == 0 and m % tm == 0
    return pl.pallas_call(
        functools.partial(_headnorm_kernel, heads=cw // HEAD),
        grid=(m // tm, width // cw),
        in_specs=[pl.BlockSpec((tm, cw), lambda i, c: (i, c + col_off // cw)),
                  pl.BlockSpec((1, HEAD), lambda i, c: (0, 0))],
        out_specs=pl.BlockSpec((tm, cw), lambda i, c: (i, c)),
        out_shape=jax.ShapeDtypeStruct((m, width), F32),
        compiler_params=_cparams(2),
        name="headnorm",
    )(x2d, gain.reshape(1, HEAD))


def _odd_qkv_kernel(q_ref, k_ref, v_ref, qg_ref, kg_ref, k_prev_hbm, v_prev_hbm, qo_ref, ko_ref, vo_ref, *, heads):
    del k_prev_hbm, v_prev_hbm
    qg = qg_ref[...]
    kg = kg_ref[...]
    for h in range(heads):
        cols = slice(h * HEAD, (h + 1) * HEAD)
        q = q_ref[:, cols]
        qo_ref[:, cols] = q * lax.rsqrt(jnp.mean(q * q, axis=-1, keepdims=True) + EPS) * qg
        k = k_ref[:, cols]
        ko_ref[:, cols] = k * lax.rsqrt(jnp.mean(k * k, axis=-1, keepdims=True) + EPS) * kg
    vo_ref[...] = v_ref[...]


def odd_qkv(proj, w_c, q_gain, k_gain, layer, n_layers, k_prev, v_prev, tm, cw):
    m = proj.shape[0]
    assert w_c % cw == 0 and m % tm == 0 and k_prev.shape == v_prev.shape == (n_layers, m, w_c)
    nb = w_c // cw
    blk = (tm, cw)
    any_spec = pl.BlockSpec(memory_space=pl.ANY)
    stack = jax.ShapeDtypeStruct((n_layers, m, w_c), F32)
    return pl.pallas_call(
        functools.partial(_odd_qkv_kernel, heads=cw // HEAD),
        grid=(m // tm, nb),
        in_specs=[pl.BlockSpec(blk, lambda i, c: (i, c)),
                  pl.BlockSpec(blk, lambda i, c: (i, c + nb)),
                  pl.BlockSpec(blk, lambda i, c: (i, c + 2 * nb)),
                  pl.BlockSpec((1, HEAD), lambda i, c: (0, 0)),
                  pl.BlockSpec((1, HEAD), lambda i, c: (0, 0)),
                  any_spec, any_spec],
        out_specs=[pl.BlockSpec(blk, lambda i, c: (i, c)),
                   pl.BlockSpec((None,) + blk, lambda i, c: (layer, i, c)),
                   pl.BlockSpec((None,) + blk, lambda i, c: (layer, i, c))],
        out_shape=[jax.ShapeDtypeStruct((m, w_c), F32), stack, stack],
        input_output_aliases={5: 1, 6: 2},
        compiler_params=_cparams(2),
        name="odd_qkv",
    )(proj, proj, proj, q_gain.reshape(1, HEAD), k_gain.reshape(1, HEAD), k_prev, v_prev)


def _gate_kernel(x_ref, alog_ref, dt_ref, beta_ref, gam_ref, gsm_ref, tri_ref, eb_ref, eg_ref, *, n_heads, cs):
    tm = x_ref.shape[0]
    wide = n_heads * HEAD

    @pl.when(pl.program_id(0) == 0)
    def _():
        ri = lax.broadcasted_iota(jnp.int32, (tm, tm), 0)
        ci = lax.broadcasted_iota(jnp.int32, (tm, tm), 1)
        tri_ref[...] = jnp.where((ci <= ri) & (ci >= (ri // cs) * cs), 1.0, 0.0).astype(BF16)
        src = lax.broadcasted_iota(jnp.int32, (HEAD, wide), 0)
        dst = lax.broadcasted_iota(jnp.int32, (HEAD, wide), 1) // HEAD
        eb_ref[...] = jnp.where(src == dst, 1.0, 0.0).astype(BF16)
        eg_ref[...] = jnp.where(src == dst + n_heads, 1.0, 0.0).astype(BF16)

    def parts(v):
        p0 = v.astype(BF16)
        r1 = v - p0.astype(F32)
        p1 = r1.astype(BF16)
        return p0, p1, (r1 - p1.astype(F32)).astype(BF16)

    x = x_ref[...]
    beta = jax.nn.sigmoid(x)
    g = -jnp.exp(alog_ref[...]) * jax.nn.softplus(x + dt_ref[...])
    tri = tri_ref[...]
    gam = functools.reduce(lambda a, b: a + b, [_dot(tri, p) for p in reversed(parts(g))])
    gsm_ref[...] = gam
    beta_ref[...] = functools.reduce(lambda a, b: a + b, [_dot(p, eb_ref[...]) for p in reversed(parts(beta))])
    gam_ref[...] = functools.reduce(lambda a, b: a + b, [_dot(p, eg_ref[...]) for p in reversed(parts(gam))])


def gdn_gates(small2d, a_log, dt_bias, tm, cs):
    m = small2d.shape[0]
    n_heads = a_log.shape[0]
    assert tm % cs == 0 and m % tm == 0
    pad = HEAD - 2 * n_heads
    alog_row = jnp.concatenate([jnp.zeros((n_heads,), F32), a_log, jnp.zeros((pad,), F32)]).reshape(1, HEAD)
    dt_row = jnp.concatenate([jnp.zeros((n_heads,), F32), dt_bias, jnp.zeros((pad,), F32)]).reshape(1, HEAD)
    wide = n_heads * HEAD
    return pl.pallas_call(
        functools.partial(_gate_kernel, n_heads=n_heads, cs=cs),
        grid=(m // tm,),
        in_specs=[pl.BlockSpec((tm, HEAD), lambda i: (i, 0)),
                  pl.BlockSpec((1, HEAD), lambda i: (0, 0)),
                  pl.BlockSpec((1, HEAD), lambda i: (0, 0))],
        out_specs=[pl.BlockSpec((tm, wide), lambda i: (i, 0)),
                   pl.BlockSpec((tm, wide), lambda i: (i, 0)),
                   pl.BlockSpec((tm, HEAD), lambda i: (i, 0))],
        out_shape=[jax.ShapeDtypeStruct((m, wide), F32), jax.ShapeDtypeStruct((m, wide), F32),
                   jax.ShapeDtypeStruct((m, HEAD), F32)],
        scratch_shapes=[pltpu.VMEM((tm, tm), BF16), pltpu.VMEM((HEAD, wide), BF16), pltpu.VMEM((HEAD, wide), BF16)],
        compiler_params=_cparams(1),
        name="gdn_gates",
    )(small2d, alog_row, dt_row)


def _sconv_kernel(bg_ref, cg_ref, xb_ref, zb_ref, hc_ref, hx_ref, st_ref, w_ref, o_ref, nb_ref, buf_ref, *, tt):
    t = pl.program_id(2)
    buf_ref[0:SUBLANES, :] = jnp.where(t == 0, st_ref[0], hc_ref[0] * hx_ref[0])
    buf_ref[SUBLANES:SUBLANES + tt, :] = cg_ref[0] * xb_ref[0]
    w = w_ref[...]
    first = SUBLANES - (CONV_B - 1)
    y = buf_ref[first:first + tt, :] * w[0:1, :]
    for j in range(1, CONV_B):
        y = y + buf_ref[first + j:first + j + tt, :] * w[j:j + 1, :]
    o_ref[0] = (bg_ref[0] * y * _silu(zb_ref[0])).astype(o_ref.dtype)
    nb_ref[0] = buf_ref[tt:tt + SUBLANES, :]


def short_conv(proj3d, off_b, off_c, off_x, off_z, width, state_pad, conv_w, tt, cw):
    bn, t, _ = proj3d.shape
    assert t % tt == 0 and width % cw == 0
    rb = tt // SUBLANES

    def col(off):
        assert off % cw == 0
        return functools.partial(lambda b, c, i, o: (b, i, c + o), o=off // cw)

    def halo(off):
        return functools.partial(lambda b, c, i, o: (b, jnp.maximum(i * rb - 1, 0), c + o), o=off // cw)

    blk = (1, tt, cw)
    hblk = (1, SUBLANES, cw)
    return pl.pallas_call(
        functools.partial(_sconv_kernel, tt=tt),
        grid=(bn, width // cw, t // tt),
        in_specs=[pl.BlockSpec(blk, col(off_b)), pl.BlockSpec(blk, col(off_c)),
                  pl.BlockSpec(blk, col(off_x)), pl.BlockSpec(blk, col(off_z)),
                  pl.BlockSpec(hblk, halo(off_c)), pl.BlockSpec(hblk, halo(off_x)),
                  pl.BlockSpec(hblk, lambda b, c, i: (b, 0, c)),
                  pl.BlockSpec((CONV_B, cw), lambda b, c, i: (0, c))],
        out_specs=[pl.BlockSpec(blk, lambda b, c, i: (b, i, c)),
                   pl.BlockSpec(hblk, lambda b, c, i: (b, 0, c))],
        out_shape=[jax.ShapeDtypeStruct((bn, t, width), BF16),
                   jax.ShapeDtypeStruct((bn, SUBLANES, width), F32)],
        scratch_shapes=[pltpu.VMEM((tt + SUBLANES, cw), F32)],
        compiler_params=_cparams(3),
        name="short_conv",
    )(proj3d, proj3d, proj3d, proj3d, proj3d, proj3d, state_pad, conv_w)


def _split_bf16(x):
    hi = x.astype(BF16)
    return hi, (x - hi.astype(F32)).astype(BF16)


def _dot_3pass(a, b):
    a_hi, a_lo = _split_bf16(a)
    b_hi, b_lo = _split_bf16(b)
    return _dot(a_hi, b_hi) + (_dot(a_hi, b_lo) + _dot(a_lo, b_hi))


def _gdn_kernel(xq_ref, xk_ref, xv_ref, wq_ref, wk_ref, wv_ref, cq_ref, ck_ref, cv_ref, z_ref, beta_ref, gam_ref,
                grow_ref, s0_ref, gain_ref, o_ref, s_ref, xbuf, act, carry, *, hb, cs, nct):
    ti = pl.program_id(2)
    tt = nct * cs

    @pl.when(ti == 0)
    def _():
        s_ref[...] = s0_ref[...]
        for a, c_ref in enumerate((cq_ref, ck_ref, cv_ref)):
            carry[a] = c_ref[0]

    first = SUBLANES - (CONV_A - 1)
    for a, (x_ref, w_ref) in enumerate(((xq_ref, wq_ref), (xk_ref, wk_ref), (xv_ref, wv_ref))):
        xbuf[a, 0:SUBLANES, :] = carry[a]
        xbuf[a, SUBLANES:SUBLANES + tt, :] = x_ref[0]
        carry[a] = x_ref[0, tt - SUBLANES:tt, :]
        w = w_ref[...]
        y = xbuf[a, first:first + tt, :] * w[0:1, :]
        for j in range(1, CONV_A):
            y = y + xbuf[a, first + j:first + j + tt, :] * w[j:j + 1, :]
        y = _silu(y)
        if a == 2:
            act[a] = y
        else:
            scale = HEAD ** -0.5 if a == 0 else 1.0
            for h in range(hb):
                yh = y[:, h * HEAD:(h + 1) * HEAD]
                ss = jnp.sum(yh * yh, axis=-1, keepdims=True)
                act[a, :, h * HEAD:(h + 1) * HEAD] = yh * lax.rsqrt(ss + EPS) * scale

    ii = lax.broadcasted_iota(jnp.int32, (cs, cs), 0)
    jj = lax.broadcasted_iota(jnp.int32, (cs, cs), 1)
    causal = ii >= jj
    strict = ii > jj
    eye_f = (ii == jj).astype(F32)
    gain = gain_ref[...]
    n_double = int(math.log2(cs)) - 1
    chains = [(j, h) for j in range(nct) for h in range(hb)]

    def tile(ref, c):
        return ref[0, c[0] * cs:(c[0] + 1) * cs, c[1] * HEAD:(c[1] + 1) * HEAD]

    def act_tile(a, c):
        return act[a, c[0] * cs:(c[0] + 1) * cs, c[1] * HEAD:(c[1] + 1) * HEAD]

    k = [act_tile(1, c) for c in chains]
    beta = [tile(beta_ref, c) for c in chains]
    gam = [tile(gam_ref, c) for c in chains]
    kb = [a * b for a, b in zip(k, beta)]
    raw = [_dot_nt(jnp.concatenate([b, act_tile(0, c)], axis=0).astype(BF16), a.astype(BF16))
           for a, b, c in zip(k, kb, chains)]
    decay = []
    for g, (j, h) in zip(gam, chains):
        gam_row = grow_ref[0, h, pl.ds(ti * nct + j, 1), :]
        decay.append(jnp.where(causal, jnp.exp(jnp.where(causal, g[:, :cs] - gam_row, 0.0)), 0.0))
    qk = [jnp.where(causal, r[cs:] * d, 0.0) for r, d in zip(raw, decay)]
    pw = [jnp.where(strict, -(r[:cs] * d), 0.0) for r, d in zip(raw, decay)]
    inv = [eye_f + p for p in pw]
    pw = [_dot(p.astype(BF16), p.astype(BF16)) for p in pw]
    for _ in range(n_double - 1):
        prod = [_dot(jnp.concatenate([p, x], axis=0).astype(BF16), p.astype(BF16)) for p, x in zip(pw, inv)]
        pw = [p[:cs] for p in prod]
        inv = [x + p[cs:] for x, p in zip(inv, prod)]
    inv = [x + _dot(x.astype(BF16), p.astype(BF16)) for x, p in zip(inv, pw)]
    e_gam = [jnp.exp(g) for g in gam]
    uw = [_dot_3pass(x, jnp.concatenate([act_tile(2, c) * b, a * e], axis=1))
          for x, c, b, a, e in zip(inv, chains, beta, kb, e_gam)]
    wq = [jnp.concatenate([x[:, HEAD:], act_tile(0, c) * e], axis=0).astype(BF16)
          for x, c, e in zip(uw, chains, e_gam)]
    k_dec_t = [(a * jnp.exp(g[cs - 1:cs, :] - g)).T.astype(BF16) for a, g in zip(k, gam)]
    state = [s_ref[0, h] for h in range(hb)]
    for j in range(nct):
        idx = [j * hb + h for h in range(hb)]
        ws = [_dot(wq[i], state[h].astype(BF16)) for h, i in enumerate(idx)]
        v16 = [(uw[i][:, :HEAD] - x[:cs]).astype(BF16) for x, i in zip(ws, idx)]
        o = [x[cs:] + _dot(qk[i].astype(BF16), y) for x, y, i in zip(ws, v16, idx)]
        state = [s * jnp.exp(gam[i][cs - 1:cs, :]) + _dot(k_dec_t[i], y) for s, y, i in zip(state, v16, idx)]
        for h, i in enumerate(idx):
            ms = jnp.mean(o[h] * o[h], axis=-1, keepdims=True)
            on = o[h] * lax.rsqrt(ms + EPS) * gain
            o_ref[0, j * cs:(j + 1) * cs, h * HEAD:(h + 1) * HEAD] = (
                on * _silu(tile(z_ref, (j, h)))).astype(o_ref.dtype)
    for h in range(hb):
        s_ref[0, h] = state[h]


def gdn(proj3d, z_off, conv_w, conv_state, beta_rep, gam_rep, gam_row4, s0, gain, hb, tt):
    bn, t, _ = proj3d.shape
    n_heads = s0.shape[1]
    nc, cs = gam_row4.shape[2], gam_row4.shape[3]
    cw = hb * HEAD
    ng = n_heads // hb
    assert z_off % cw == 0 and n_heads % hb == 0 and t % tt == 0 and tt % cs == 0 and tt % SUBLANES == 0
    blk = (1, tt, cw)
    sblk = (1, hb, HEAD, HEAD)
    qkv_specs = [pl.BlockSpec(blk, functools.partial(lambda b, h, i, a: (b, i, a * ng + h), a=a)) for a in range(3)]
    w_specs = [pl.BlockSpec((CONV_A, cw), functools.partial(lambda b, h, i, a: (0, a * ng + h), a=a)) for a in range(3)]
    c_specs = [pl.BlockSpec((1, SUBLANES, cw), functools.partial(lambda b, h, i, a: (b, 0, a * ng + h), a=a))
               for a in range(3)]
    return pl.pallas_call(
        functools.partial(_gdn_kernel, hb=hb, cs=cs, nct=tt // cs),
        grid=(bn, ng, t // tt),
        in_specs=qkv_specs + w_specs + c_specs + [
                  pl.BlockSpec(blk, lambda b, h, i: (b, i, z_off // cw + h)),
                  pl.BlockSpec(blk, lambda b, h, i: (b, i, h)),
                  pl.BlockSpec(blk, lambda b, h, i: (b, i, h)),
                  pl.BlockSpec((1, hb, nc, cs), lambda b, h, i: (b, h, 0, 0)),
                  pl.BlockSpec(sblk, lambda b, h, i: (b, h, 0, 0)),
                  pl.BlockSpec((1, HEAD), lambda b, h, i: (0, 0))],
        out_specs=[pl.BlockSpec(blk, lambda b, h, i: (b, i, h)),
                   pl.BlockSpec(sblk, lambda b, h, i: (b, h, 0, 0))],
        out_shape=[jax.ShapeDtypeStruct((bn, t, n_heads * HEAD), BF16),
                   jax.ShapeDtypeStruct((bn, n_heads, HEAD, HEAD), F32)],
        scratch_shapes=[pltpu.VMEM((3, tt + SUBLANES, cw), F32), pltpu.VMEM((3, tt, cw), F32),
                        pltpu.VMEM((3, SUBLANES, cw), F32)],
        compiler_params=_cparams(3),
        name="gdn",
    )(proj3d, proj3d, proj3d, conv_w, conv_w, conv_w, conv_state, conv_state, conv_state, proj3d, beta_rep,
      gam_rep, gam_row4, s0, gain.reshape(1, HEAD))


def _mem_attn_kernel(q_ref, z_ref, mk_ref, mv_ref, gain_ref, o_ref):
    gain = gain_ref[...]
    for h in range(H_M):
        cols = slice(h * HEAD, (h + 1) * HEAD)
        q = q_ref[0, :, cols]
        ms = jnp.mean(q * q, axis=-1, keepdims=True)
        qn = (q * lax.rsqrt(ms + EPS) * gain).astype(BF16)
        sc = _dot_nt(qn, mk_ref[0, :, cols].astype(BF16)) * (HEAD ** -0.5)
        m = jnp.max(sc, axis=-1, keepdims=True)
        p = jnp.exp(sc - m)
        p = p / jnp.sum(p, axis=-1, keepdims=True)
        o = _dot(p.astype(BF16), mv_ref[0, :, cols].astype(BF16))
        o_ref[0, :, cols] = (o * _silu(z_ref[0, :, cols])).astype(o_ref.dtype)


def mem_attn(proj3d, q_off, z_off, mem_k, mem_v, q_gain, tt):
    bn, t, _ = proj3d.shape
    n_mem = mem_k.shape[1]
    wm = H_M * HEAD
    assert q_off % wm == 0 and z_off % wm == 0 and t % tt == 0
    return pl.pallas_call(
        _mem_attn_kernel,
        grid=(bn, t // tt),
        in_specs=[pl.BlockSpec((1, tt, wm), lambda b, i: (b, i, q_off // wm)),
                  pl.BlockSpec((1, tt, wm), lambda b, i: (b, i, z_off // wm)),
                  pl.BlockSpec((1, n_mem, wm), lambda b, i: (b, 0, 0)),
                  pl.BlockSpec((1, n_mem, wm), lambda b, i: (b, 0, 0)),
                  pl.BlockSpec((1, HEAD), lambda b, i: (0, 0))],
        out_specs=pl.BlockSpec((1, tt, wm), lambda b, i: (b, i, 0)),
        out_shape=jax.ShapeDtypeStruct((bn, t, wm), BF16),
        compiler_params=_cparams(2),
        name="mem_attn",
    )(proj3d, proj3d, mem_k.reshape(bn, n_mem, wm), mem_v.reshape(bn, n_mem, wm), q_gain.reshape(1, HEAD))


def _dil_prompt_kernel(q_ref, k_ref, v_ref, z_ref, o_ref, acc_ref, lse_ref, *, hg, t, group):
    qi = lax.broadcasted_iota(jnp.int32, (SWA_BLOCK, SWA_BLOCK), 0)
    ki = lax.broadcasted_iota(jnp.int32, (SWA_BLOCK, SWA_BLOCK), 1)
    n_pat = len(SWA_PATTERNS)
    for h in range(hg):
        cols = slice(h * HEAD, (h + 1) * HEAD)
        for p, (window, dil) in enumerate(SWA_PATTERNS):
            assert window // dil == SWA_BLOCK
            l_sub = t // dil
            nb = l_sub // SWA_BLOCK
            assert nb * SWA_BLOCK * dil == t

            def rows_of(r, c, dil=dil):
                return pl.ds(r + c * (SWA_BLOCK * dil), SWA_BLOCK, stride=dil)

            blocks = [(r, c) for r in range(dil) for c in range(nb)]
            for g0 in range(0, len(blocks), group):
                grp = blocks[g0:g0 + group]
                q = [(q_ref[0, rows_of(r, c), cols] * (HEAD ** -0.5)).astype(BF16) for r, c in grp]
                s_cur = [jnp.where(ki <= qi, _dot_nt(x, k_ref[0, rows_of(r, c), cols].astype(BF16)), -jnp.inf)
                         for x, (r, c) in zip(q, grp)]
                s_prev = [jnp.where(ki >= qi, _dot_nt(x, k_ref[0, rows_of(r, c - 1), cols].astype(BF16)), -jnp.inf)
                          if c > 0 else None for x, (r, c) in zip(q, grp)]
                m = [jnp.max(a, axis=-1, keepdims=True) if b is None else
                     jnp.maximum(jnp.max(a, axis=-1, keepdims=True), jnp.max(b, axis=-1, keepdims=True))
                     for a, b in zip(s_cur, s_prev)]
                e_cur = [jnp.exp(a - x) for a, x in zip(s_cur, m)]
                e_prev = [None if b is None else jnp.exp(b - x) for b, x in zip(s_prev, m)]
                den = [jnp.sum(a, axis=-1, keepdims=True) if b is None else
                       jnp.sum(a, axis=-1, keepdims=True) + jnp.sum(b, axis=-1, keepdims=True)
                       for a, b in zip(e_cur, e_prev)]
                o = [_dot(a.astype(BF16), v_ref[0, rows_of(r, c), cols].astype(BF16)) for a, (r, c) in zip(e_cur, grp)]
                o = [x if b is None else x + _dot(b.astype(BF16), v_ref[0, rows_of(r, c - 1), cols].astype(BF16))
                     for x, b, (r, c) in zip(o, e_prev, grp)]
                for x, d, mx, (r, c) in zip(o, den, m, grp):
                    acc_ref[p, rows_of(r, c), :] = x / d
                    lse_ref[p, rows_of(r, c), :] = jnp.broadcast_to(mx + jnp.log(d), (SWA_BLOCK, HEAD))
        lse = [lse_ref[p] for p in range(n_pat)]
        top = functools.reduce(jnp.maximum, lse)
        wgt = [jnp.exp(x - top) for x in lse]
        tot = functools.reduce(lambda a, b: a + b, wgt)
        mix = functools.reduce(lambda a, b: a + b, [(wgt[p] / tot) * acc_ref[p] for p in range(n_pat)])
        o_ref[0, :, cols] = (mix * _silu(z_ref[0, :, cols])).astype(o_ref.dtype)


def dil_prompt(qn, k_stack, v_stack, layer, proj3d, z_off, hg):
    bn, t, wc = qn.shape
    cw = hg * HEAD
    assert z_off % cw == 0 and wc % cw == 0
    blk = (1, t, cw)
    sblk = (None, 1, t, cw)
    return pl.pallas_call(
        functools.partial(_dil_prompt_kernel, hg=hg, t=t, group=8),
        grid=(bn, wc // cw),
        in_specs=[pl.BlockSpec(blk, lambda b, h: (b, 0, h)),
                  pl.BlockSpec(sblk, lambda b, h: (layer, b, 0, h)),
                  pl.BlockSpec(sblk, lambda b, h: (layer, b, 0, h)),
                  pl.BlockSpec(blk, lambda b, h: (b, 0, z_off // cw + h))],
        out_specs=pl.BlockSpec(blk, lambda b, h: (b, 0, h)),
        out_shape=jax.ShapeDtypeStruct((bn, t, wc), BF16),
        scratch_shapes=[pltpu.VMEM((len(SWA_PATTERNS), t, HEAD), F32),
                        pltpu.VMEM((len(SWA_PATTERNS), t, HEAD), F32)],
        compiler_params=_cparams(2),
        name="dil_prompt",
    )(qn, k_stack, v_stack, proj3d)


MASKED_MAX = -1e30


def _dil_sample_kernel(q_ref, kn_ref, vn_ref, z_ref, kc_ref, vc_ref, o_ref, m_ref, l_ref, acc_ref,
                       *, n_heads, t, wb, tw, group):
    j = pl.program_id(1)
    n_pat = len(SWA_PATTERNS)

    @pl.when(j == 0)
    def _():
        m_ref[...] = jnp.full(m_ref.shape, MASKED_MAX, F32)
        l_ref[...] = jnp.zeros(l_ref.shape, F32)
        acc_ref[...] = jnp.zeros(acc_ref.shape, F32)

    def head_rows(ref, h):
        return ref[0, pl.ds(h, tw, stride=n_heads), :].astype(BF16)

    def q16(h):
        return (q_ref[0, :, h * HEAD:(h + 1) * HEAD] * (HEAD ** -0.5)).astype(BF16)

    def update(hs, p, scores, ok, values):
        idx = [h * n_pat + p for h in hs]
        s = [jnp.where(ok, x, -jnp.inf) for x in scores]
        m_old = [m_ref[i] for i in idx]
        m_new = [jnp.maximum(a, jnp.max(x, axis=-1, keepdims=True)) for a, x in zip(m_old, s)]
        alpha = [jnp.exp(a - b) for a, b in zip(m_old, m_new)]
        e = [jnp.exp(x - b[:, :1]) for x, b in zip(s, m_new)]
        pv = [_dot(x.astype(BF16), v) for x, v in zip(e, values)]
        for i, a, x, y, b in zip(idx, alpha, e, pv, m_new):
            l_ref[i] = l_ref[i] * a + jnp.sum(x, axis=-1, keepdims=True)
            acc_ref[i] = acc_ref[i] * a + y
            m_ref[i] = b

    row = lax.broadcasted_iota(jnp.int32, (t, tw), 0)
    col = lax.broadcasted_iota(jnp.int32, (t, tw), 1)
    delta = wb + row - (j * tw + col)
    nearest = wb - (j + 1) * tw + 1
    for p, (window, dil) in enumerate(SWA_PATTERNS):
        def tile_update(p=p, window=window, dil=dil):
            ok = ((delta & (dil - 1)) == 0) & (delta <= window)
            for g0 in range(0, n_heads, group):
                hs = list(range(g0, g0 + group))
                scores = [_dot_nt(q16(h), head_rows(kc_ref, h)) for h in hs]
                update(hs, p, scores, ok, [head_rows(vc_ref, h) for h in hs])
        if window >= wb - tw + 1:
            tile_update()
        else:
            pl.when(nearest <= window)(tile_update)

    @pl.when(j == pl.num_programs(1) - 1)
    def _():
        dn = lax.broadcasted_iota(jnp.int32, (t, t), 0) - lax.broadcasted_iota(jnp.int32, (t, t), 1)
        for g0 in range(0, n_heads, group):
            hs = list(range(g0, g0 + group))
            cols = [slice(h * HEAD, (h + 1) * HEAD) for h in hs]
            scores = [_dot_nt(q16(h), kn_ref[0, :, c].astype(BF16)) for h, c in zip(hs, cols)]
            values = [vn_ref[0, :, c].astype(BF16) for c in cols]
            for p, (window, dil) in enumerate(SWA_PATTERNS):
                ok = (dn >= 0) & ((dn & (dil - 1)) == 0) & (dn <= window)
                update(hs, p, scores, ok, values)
            for h, c in zip(hs, cols):
                idx = [h * n_pat + p for p in range(n_pat)]
                lse = [m_ref[i] + jnp.log(l_ref[i]) for i in idx]
                top = functools.reduce(jnp.maximum, lse)
                wgt = [jnp.exp(x - top) for x in lse]
                tot = functools.reduce(lambda a, b: a + b, wgt)
                mix = functools.reduce(lambda a, b: a + b,
                                       [(w / tot) * (acc_ref[i] / l_ref[i]) for w, i in zip(wgt, idx)])
                o_ref[0, :, c] = (mix * _silu(z_ref[0, :, c])).astype(o_ref.dtype)


def dil_sample(qn, k_stack, v_stack, layer, proj3d, z_off, cache_k, cache_v, n_heads, tw):
    bn, t, wc = qn.shape
    wb = cache_k.shape[2] // n_heads
    assert z_off % wc == 0 and wb % tw == 0 and wc == n_heads * HEAD
    for window, dil in SWA_PATTERNS:
        assert dil & (dil - 1) == 0
    blk = (1, t, wc)
    sblk = (None, 1, t, wc)
    cblk = (None, 1, tw * n_heads, HEAD)
    state = pltpu.VMEM((n_heads * len(SWA_PATTERNS), t, HEAD), F32)
    return pl.pallas_call(
        functools.partial(_dil_sample_kernel, n_heads=n_heads, t=t, wb=wb, tw=tw, group=8),
        grid=(bn, wb // tw),
        in_specs=[pl.BlockSpec(blk, lambda b, j: (b, 0, 0)),
                  pl.BlockSpec(sblk, lambda b, j: (layer, b, 0, 0)),
                  pl.BlockSpec(sblk, lambda b, j: (layer, b, 0, 0)),
                  pl.BlockSpec(blk, lambda b, j: (b, 0, z_off // wc)),
                  pl.BlockSpec(cblk, lambda b, j: (layer, b, j, 0)),
                  pl.BlockSpec(cblk, lambda b, j: (layer, b, j, 0))],
        out_specs=pl.BlockSpec(blk, lambda b, j: (b, 0, 0)),
        out_shape=jax.ShapeDtypeStruct((bn, t, wc), BF16),
        scratch_shapes=[state, state, state],
        compiler_params=_cparams(2),
        name="dil_sample",
    )(qn, k_stack, v_stack, proj3d, cache_k, cache_v)


CACHE_ROLL_CHUNK_BYTES = 4 * 1024 * 1024


def _cache_roll_kernel(ck_hbm, cv_hbm, kn_ref, vn_ref, ok_hbm, ov_hbm, buf, stage, in_sems, out_sems, new_sems,
                       *, n_heads, t, chunk, n_chunks):
    l = pl.program_id(0)
    b = pl.program_id(1)
    new = t * n_heads
    srcs = (ck_hbm, cv_hbm)
    dsts = (ok_hbm, ov_hbm)

    def read(a, c):
        return pltpu.make_async_copy(srcs[a].at[l, b, pl.ds(new + c * chunk, chunk)], buf.at[a, c % 2],
                                     in_sems.at[a, c % 2])

    def write(a, c):
        return pltpu.make_async_copy(buf.at[a, c % 2], dsts[a].at[l, b, pl.ds(c * chunk, chunk)],
                                     out_sems.at[a, c % 2])

    def write_new(a):
        return pltpu.make_async_copy(stage.at[a], dsts[a].at[l, b, pl.ds(n_chunks * chunk, new)], new_sems.at[a])

    for a in range(2):
        read(a, 0).start()
    for h in range(n_heads):
        stage[0, pl.ds(h, t, stride=n_heads), :] = kn_ref[:, h * HEAD:(h + 1) * HEAD]
        stage[1, pl.ds(h, t, stride=n_heads), :] = vn_ref[:, h * HEAD:(h + 1) * HEAD]
    for a in range(2):
        write_new(a).start()
    for c in range(n_chunks):
        for a in range(2):
            read(a, c).wait()
            write(a, c).start()
            if c + 1 < n_chunks:
                if c >= 1:
                    write(a, c - 1).wait()
                read(a, c + 1).start()
    for a in range(2):
        if n_chunks >= 2:
            write(a, n_chunks - 2).wait()
        write(a, n_chunks - 1).wait()
        write_new(a).wait()


def cache_roll(cache_k, cache_v, k_new, v_new, n_heads):
    n_l, bn, rows, _ = cache_k.shape
    t = k_new.shape[2]
    wb = rows // n_heads
    keep_tokens = wb - t
    token_bytes = n_heads * HEAD * cache_k.dtype.itemsize
    chunk_tokens = max(d for d in range(1, keep_tokens + 1)
                       if keep_tokens % d == 0 and d * token_bytes <= CACHE_ROLL_CHUNK_BYTES)
    chunk = chunk_tokens * n_heads
    nblk = (None, None, t, n_heads * HEAD)
    any_spec = pl.BlockSpec(memory_space=pl.ANY)
    out = jax.ShapeDtypeStruct(cache_k.shape, cache_k.dtype)
    return pl.pallas_call(
        functools.partial(_cache_roll_kernel, n_heads=n_heads, t=t, chunk=chunk, n_chunks=keep_tokens // chunk_tokens),
        grid=(n_l, bn),
        in_specs=[any_spec, any_spec,
                  pl.BlockSpec(nblk, lambda l, b: (l, b, 0, 0)),
                  pl.BlockSpec(nblk, lambda l, b: (l, b, 0, 0))],
        out_specs=[any_spec, any_spec],
        out_shape=[out, out],
        scratch_shapes=[pltpu.VMEM((2, 2, chunk, HEAD), cache_k.dtype),
                        pltpu.VMEM((2, t * n_heads, HEAD), cache_k.dtype),
                        pltpu.SemaphoreType.DMA((2, 2)), pltpu.SemaphoreType.DMA((2, 2)),
                        pltpu.SemaphoreType.DMA((2,))],
        compiler_params=_cparams(2),
        name="cache_roll",
    )(cache_k, cache_v, k_new, v_new)


def _row_tile(m, pref):
    return pref if m % pref == 0 else m


def _flat(x):
    return x.reshape(x.shape[0] * x.shape[1], x.shape[2])


def _in_proj(x_big, x_small, g_norm, w_stacks, layer):
    xb, xs = _flat(x_big), _flat(x_small)
    hb = rms_cast(xb, g_norm, _row_tile(xb.shape[0], 256))
    hs = rms_cast(xs, g_norm, _row_tile(xs.shape[0], 256))
    tm = _row_tile(xb.shape[0], 1024)
    return [matmul([hb], w, layer, 0, w.shape[2], tm, min(1024, w.shape[2]), side_a=[hs]) for w in w_stacks]


def _out_proj(x_big, x_small, parts_big, parts_small, w_out, layer):
    xb, xs = _flat(x_big), _flat(x_small)
    yb, ys = matmul([_flat(p) for p in parts_big], w_out, layer, 0, xb.shape[1], _row_tile(xb.shape[0], 512), 1024,
                    res=xb, side_a=[_flat(p) for p in parts_small], side_res=xs)
    return yb.reshape(x_big.shape), ys.reshape(x_small.shape)


def _pad_state(buf):
    return jnp.pad(buf, ((0, 0), (SUBLANES - buf.shape[1], 0), (0, 0)))


def _even_branches(shape, proj_a, proj_b, small, conv_a_buf, s0, conv_b_buf, mem_k, mem_v, conv_a_w, a_log, dt_bias,
                   gdn_gain, conv_b_w, q_gain_m):
    bn, t, _ = shape
    m = bn * t
    n_heads = a_log.shape[0]
    w_a = n_heads * HEAD
    w_b = conv_b_w.shape[1]
    w_m = H_M * HEAD
    pa = proj_a.reshape(bn, t, proj_a.shape[1])
    pb = proj_b.reshape(bn, t, proj_b.shape[1])
    tt = _row_tile(t, 512)
    cs = min(GDN_CHUNK, t)
    nc = t // cs
    beta_rep, gam_rep, gam_small = gdn_gates(small, a_log, dt_bias, _row_tile(m, 512), cs)
    gam_row4 = (gam_small.reshape(bn, t, HEAD)[:, :, n_heads:2 * n_heads]
                .transpose(0, 2, 1).reshape(bn, n_heads, nc, cs))
    o_a, new_s = gdn(pa, 3 * w_a, conv_a_w, _pad_state(conv_a_buf), beta_rep.reshape(bn, t, w_a),
                     gam_rep.reshape(bn, t, w_a), gam_row4, s0, gdn_gain, 4, _row_tile(t, 4 * GDN_CHUNK))
    new_conv_a = pa[:, t - (CONV_A - 1):, :3 * w_a]
    o_b, nb8 = short_conv(pb, 0, w_b, 2 * w_b, 3 * w_b, w_b, _pad_state(conv_b_buf), conv_b_w, tt, 512)
    new_conv_b = nb8[:, SUBLANES - (CONV_B - 1):, :]
    o_m = mem_attn(pb, 4 * w_b, 4 * w_b + w_m, mem_k, mem_v, q_gain_m, tt)
    return (o_a, o_b, o_m), (new_conv_a, new_s, new_conv_b)


def even_layer(x_big, x_small, states_big, states_small, layer, g_norm, w_a, w_b, w_small, conv_a_w, a_log, dt_bias,
               gdn_gain, conv_b_w, w_out, q_gain_m):
    (pa_b, pa_s), (pb_b, pb_s), (sm_b, sm_s) = _in_proj(x_big, x_small, g_norm, (w_a, w_b, w_small), layer)
    wts = (conv_a_w, a_log, dt_bias, gdn_gain, conv_b_w, q_gain_m)
    parts_b, new_b = _even_branches(x_big.shape, pa_b, pb_b, sm_b, *states_big, *wts)
    parts_s, new_s = _even_branches(x_small.shape, pa_s, pb_s, sm_s, *states_small, *wts)
    y_b, y_s = _out_proj(x_big, x_small, parts_b, parts_s, w_out, layer)
    return y_b, y_s, new_b, new_s


def _odd_branches(shape, proj, cache_k, cache_v, mem_k, mem_v, layer, n_layers, k_prev, v_prev, q_gain_c, k_gain_c,
                  q_gain_m):
    bn, t, w_c = shape
    m = bn * t
    w_m = H_M * HEAD
    off_z = 3 * w_c
    off_qm = 4 * w_c
    proj3d = proj.reshape(bn, t, proj.shape[1])
    qn, k_stack, v_stack = odd_qkv(proj, w_c, q_gain_c, k_gain_c, layer, n_layers, k_prev, v_prev,
                                   _row_tile(m, 512), 512)
    qn = qn.reshape(bn, t, w_c)
    k4 = k_stack.reshape(n_layers, bn, t, w_c)
    v4 = v_stack.reshape(n_layers, bn, t, w_c)
    if cache_k is None:
        o_c = dil_prompt(qn, k4, v4, layer, proj3d, off_z, 1)
    else:
        o_c = dil_sample(qn, k4, v4, layer, proj3d, off_z, cache_k, cache_v, w_c // HEAD, 256)
    o_m = mem_attn(proj3d, off_qm, off_qm + w_m, mem_k, mem_v, q_gain_m, _row_tile(t, 512))
    return (o_c, o_m), k_stack, v_stack


def odd_layer(x_big, x_small, cache_k, cache_v, mem_big, mem_small, layer, n_layers, kv_big, kv_small, g_norm, w_main,
              q_gain_c, k_gain_c, w_out, q_gain_m):
    ((p_b, p_s),) = _in_proj(x_big, x_small, g_norm, (w_main,), layer)
    gains = (q_gain_c, k_gain_c, q_gain_m)
    parts_b, kb, vb = _odd_branches(x_big.shape, p_b, None, None, *mem_big, layer, n_layers, *kv_big, *gains)
    parts_s, ks, vs = _odd_branches(x_small.shape, p_s, cache_k, cache_v, *mem_small, layer, n_layers, *kv_small,
                                    *gains)
    y_b, y_s = _out_proj(x_big, x_small, parts_b, parts_s, w_out, layer)
    return y_b, y_s, (kb, vb), (ks, vs)


def mem_project(mem, g, w_kv_stack, layer, k_gain):
    bn, n_mem, d = mem.shape
    m = bn * n_mem
    w_m = H_M * HEAD
    hm = rms_cast(mem.reshape(m, d), g, _row_tile(m, 256))
    kv = matmul([hm], w_kv_stack, layer, 0, 2 * w_m, _row_tile(m, 512), w_m)
    k = headnorm(kv, 0, w_m, k_gain, _row_tile(m, 512), w_m)
    return k.reshape(bn, n_mem, H_M, HEAD), kv[:, w_m:].reshape(bn, n_mem, H_M, HEAD)


def _cast_kernel(x_ref, o_ref):
    o_ref[...] = x_ref[...].astype(o_ref.dtype)


def cast_bf16(w, n_size, tk, tn):
    n_l, k, _ = w.shape
    assert k % tk == 0 and n_size % tn == 0
    return pl.pallas_call(
        _cast_kernel,
        grid=(n_l, k // tk, n_size // tn),
        in_specs=[pl.BlockSpec((None, tk, tn), lambda l, i, j: (l, i, j))],
        out_specs=pl.BlockSpec((None, tk, tn), lambda l, i, j: (l, i, j)),
        out_shape=jax.ShapeDtypeStruct((n_l, k, n_size), BF16),
        compiler_params=_cparams(3),
        name="cast_bf16",
    )(w)


def _prep_even_weights(w_in, n_heads):
    cut = 4 * n_heads * HEAD
    w_a = w_in[:, :, :cut].astype(BF16)
    w_b = w_in[:, :, cut + 2 * n_heads:].astype(BF16)
    w_small = jnp.pad(w_in[:, :, cut:cut + 2 * n_heads], ((0, 0), (0, 0), (0, HEAD - 2 * n_heads))).astype(BF16)
    return w_a, w_b, w_small


def kernel(x_prompt, x_sample, state_gdn, state_gdn_conv, state_sconv, cache_swa_k, cache_swa_v, cache_mem_k, cache_mem_v, mem_prompt, norm_even, w_in_even, conv_a, a_log, dt_bias, gdn_norm, conv_b, w_out_even, norm_odd, w_in_odd, q_norm_c, k_norm_c, w_out_odd, mem_norm, w_mem_kv, q_norm_m, k_norm_m):
    bp, t_p, d = x_prompt.shape
    bs, t_s, _ = x_sample.shape
    depth = mem_norm.shape[0]
    n_odd = norm_odd.shape[0]
    n_heads_a = a_log.shape[1]
    n_heads_c = cache_swa_k.shape[3]
    wb = cache_swa_k.shape[2]
    w_even_a, w_even_b, w_even_small = _prep_even_weights(w_in_even, n_heads_a)
    w_odd_main = cast_bf16(w_in_odd, w_in_odd.shape[2], 1024, 1024)
    w_out_e = cast_bf16(w_out_even, d, 512, 2048)
    w_out_o = cast_bf16(w_out_odd, d, 512, 2048)
    w_kv = cast_bf16(w_mem_kv, w_mem_kv.shape[2], 1024, 1024)
    cache_k = cache_swa_k.reshape(n_odd, bs, wb * n_heads_c, HEAD)
    cache_v = cache_swa_v.reshape(n_odd, bs, wb * n_heads_c, HEAD)
    y_p, y_s = x_prompt, x_sample
    gdn_p, gdn_s, gconv_p, gconv_s, sconv_p, sconv_s = [], [], [], [], [], []
    memk_p, memv_p = [], []
    kv_p = tuple(jnp.zeros((n_odd, bp * t_p, d), F32) for _ in range(2))
    kv_s = tuple(jnp.zeros((n_odd, bs * t_s, d), F32) for _ in range(2))
    for i in range(depth):
        mk_p, mv_p = mem_project(mem_prompt, mem_norm[i], w_kv, i, k_norm_m[i])
        memk_p.append(mk_p)
        memv_p.append(mv_p)
        if i % 2 == 0:
            e = i // 2
            states_p = (jnp.zeros((bp,) + state_gdn_conv.shape[2:], state_gdn_conv.dtype),
                        jnp.zeros((bp,) + state_gdn.shape[2:], state_gdn.dtype),
                        jnp.zeros((bp,) + state_sconv.shape[2:], state_sconv.dtype), mk_p, mv_p)
            states_s = (state_gdn_conv[e], state_gdn[e], state_sconv[e], cache_mem_k[i], cache_mem_v[i])
            y_p, y_s, (ca_p, sa_p, cb_p), (ca_s, sa_s, cb_s) = even_layer(
                y_p, y_s, states_p, states_s, e, norm_even[e], w_even_a, w_even_b, w_even_small, conv_a[e], a_log[e],
                dt_bias[e], gdn_norm[e], conv_b[e], w_out_e, q_norm_m[i])
            gconv_p.append(ca_p)
            gdn_p.append(sa_p)
            sconv_p.append(cb_p)
            gconv_s.append(ca_s)
            gdn_s.append(sa_s)
            sconv_s.append(cb_s)
        else:
            o = i // 2
            y_p, y_s, kv_p, kv_s = odd_layer(
                y_p, y_s, cache_k, cache_v, (mk_p, mv_p), (cache_mem_k[i], cache_mem_v[i]), o, n_odd, kv_p, kv_s,
                norm_odd[o], w_odd_main, q_norm_c[o], k_norm_c[o], w_out_o, q_norm_m[i])
    (kp, vp), (ks, vs) = kv_p, kv_s
    n_keep = min(max(w for w, _ in SWA_PATTERNS), t_p)
    swa_k_p = kp.reshape(n_odd, bp, t_p, n_heads_c, HEAD)[:, :, t_p - n_keep:]
    swa_v_p = vp.reshape(n_odd, bp, t_p, n_heads_c, HEAD)[:, :, t_p - n_keep:]
    swa_k_s, swa_v_s = cache_roll(cache_k, cache_v, ks.reshape(n_odd, bs, t_s, d), vs.reshape(n_odd, bs, t_s, d),
                                  n_heads_c)
    swa_k_s = swa_k_s.reshape(n_odd, bs, wb, n_heads_c, HEAD)
    swa_v_s = swa_v_s.reshape(n_odd, bs, wb, n_heads_c, HEAD)
    return (y_p, y_s, jnp.stack(gdn_p), jnp.stack(gdn_s), jnp.stack(gconv_p), jnp.stack(gconv_s),
            jnp.stack(sconv_p), jnp.stack(sconv_s), swa_k_p, swa_k_s, swa_v_p, swa_v_s,
            jnp.stack(memk_p), jnp.stack(memv_p))
```
